```python
import jax, jax.numpy as jnp
from jax import lax
import numpy as np

D_MODEL = 1024
BATCH = 8
SEQ = 2048
DEPTH = 4
DEC_BATCH = 128
DEC_SEQ = 8
PAST_LEN = 16384
PAGE_SIZE = 128

MIX_WIDTH = D_MODEL
A_WIDTH = MIX_WIDTH // 2
B_WIDTH = MIX_WIDTH - A_WIDTH
CHUNK_A = 128
A_HEADS = 4
A_HEAD_DIM = A_WIDTH // A_HEADS
B_EXPAND = 128
B_HEADS = B_WIDTH // B_EXPAND
B_DK = B_EXPAND
B_DV = B_WIDTH // B_HEADS
B_QK = B_HEADS * B_DK
CHUNK_B = 64
IN_COLS = 2 * A_WIDTH + 2 * B_QK + 2 * B_WIDTH
SPLITS = [A_WIDTH, 2 * A_WIDTH, 2 * A_WIDTH + B_QK, 2 * A_WIDTH + 2 * B_QK, 2 * A_WIDTH + 2 * B_QK + B_WIDTH]
D_FF = 2816
N_EXPERTS = 8
TOP_K = 2
N_DENSE = (DEPTH + 1) // 2
N_MOE = DEPTH // 2
ALPHA = (2.0 * DEPTH) ** 0.25
BETA = (8.0 * DEPTH) ** -0.25
LN_EPS = 1e-5
RMS_EPS = 1e-6

kernel_name = "hymba_gmlp_hgrn2_deepnorm_adaln_moe_step"


def layer_norm(x, w, b):
    xf = x.astype(jnp.float32)
    mu = jnp.mean(xf, -1, keepdims=True)
    var = jnp.mean(jnp.square(xf - mu), -1, keepdims=True)
    return ((xf - mu) * lax.rsqrt(var + LN_EPS) * w.astype(jnp.float32) + b.astype(jnp.float32)).astype(x.dtype)


def chunk_spatial_gate(u, vn, w_s, b_s):
    bsz, t_len = u.shape[:2]
    n_chunks = -(-t_len // CHUNK_A)
    pad = n_chunks * CHUNK_A - t_len
    vp = jnp.pad(vn, ((0, 0), (0, pad), (0, 0), (0, 0))).reshape(bsz, n_chunks, CHUNK_A, A_HEADS, A_HEAD_DIM)
    causal = jnp.tril(jnp.ones((CHUNK_A, CHUNK_A), bool))
    w = jnp.where(causal[None], w_s, 0)
    mixed = jnp.einsum('hts,bcshd->bcthd', w, vp) + b_s.T[None, None, :, :, None]
    mixed = mixed.reshape(bsz, n_chunks * CHUNK_A, A_HEADS, A_HEAD_DIM)[:, :t_len]
    return u * mixed


def hgrn2_chunked(q, k, logf, v, s0):
    bsz, t_len = q.shape[:2]
    c_len = CHUNK_B if t_len >= CHUNK_B else t_len
    n_chunks = -(-t_len // c_len)
    pad = n_chunks * c_len - t_len

    def prep(a):
        a = jnp.pad(a, ((0, 0), (0, pad), (0, 0), (0, 0)))
        return a.reshape(bsz, n_chunks, c_len, a.shape[2], a.shape[3]).transpose(1, 0, 3, 2, 4)

    qc, kc, gc, vc = prep(q), prep(k), prep(logf), prep(v)
    causal = jnp.tril(jnp.ones((c_len, c_len), bool))[:, :, None]

    def step(s, xs):
        qb, kb, gb, vb = xs
        bcum = jnp.cumsum(gb, axis=2)
        diff = bcum[:, :, :, None, :] - bcum[:, :, None, :, :]
        decay = jnp.exp(jnp.where(causal, diff, -jnp.inf))
        scores = jnp.einsum('bhtd,bhsd,bhtsd->bhts', qb, kb, decay)
        o = jnp.einsum('bhts,bhsv->bhtv', scores, vb) + jnp.einsum('bhtd,bhdv->bhtv', qb * jnp.exp(bcum), s)
        blast = bcum[:, :, -1:, :]
        s_new = jnp.exp(blast[:, :, 0, :])[..., None] * s + jnp.einsum('bhsd,bhsv->bhdv', kb * jnp.exp(blast - bcum), vb)
        return s_new, o

    s_fin, o = lax.scan(step, s0, (qc, kc, gc, vc))
    o = o.transpose(1, 0, 3, 2, 4).reshape(bsz, n_chunks * c_len, q.shape[2], v.shape[3])[:, :t_len]
    return o, s_fin


def token_mixer(h, s0, w_in, w_out, a_ln_w, a_ln_b, a_ws, a_bs, lb, b_norm_w):
    bsz, t_len, _ = h.shape
    proj = jnp.einsum('btd,de->bte', h, w_in)
    u, v, q, f, i, g = jnp.split(proj, SPLITS, axis=-1)
    u = jax.nn.gelu(u, approximate=False)
    vn = layer_norm(jax.nn.gelu(v, approximate=False), a_ln_w, a_ln_b)
    a_out = chunk_spatial_gate(u.reshape(bsz, t_len, A_HEADS, A_HEAD_DIM),
                               vn.reshape(bsz, t_len, A_HEADS, A_HEAD_DIM), a_ws, a_bs).reshape(bsz, t_len, A_WIDTH)
    lbh = lb.reshape(B_HEADS, B_DK)
    fg = lbh + (1.0 - lbh) * jax.nn.sigmoid(f.astype(jnp.float32).reshape(bsz, t_len, B_HEADS, B_DK))
    qq = jax.nn.silu(q.astype(jnp.float32)).reshape(bsz, t_len, B_HEADS, B_DK)
    ii = i.astype(jnp.float32).reshape(bsz, t_len, B_HEADS, B_DV)
    o, s_new = hgrn2_chunked(qq, 1.0 - fg, jnp.log(fg), ii, s0.astype(jnp.float32))
    o = o * lax.rsqrt(jnp.mean(o * o, -1, keepdims=True) + RMS_EPS) * b_norm_w.astype(jnp.float32)
    b_out = o.reshape(bsz, t_len, B_WIDTH) * jax.nn.silu(g.astype(jnp.float32))
    mix = jnp.concatenate([a_out, b_out.astype(h.dtype)], axis=-1)
    y = jnp.einsum('bte,ed->btd', mix, w_out)
    start = CHUNK_A * ((t_len - 1) // CHUNK_A)
    return y, s_new.astype(s0.dtype), vn[:, start:]


def swiglu(h, wg, wu, wd):
    return jnp.einsum('btf,fd->btd', jax.nn.silu(jnp.einsum('btd,df->btf', h, wg)) * jnp.einsum('btd,df->btf', h, wu), wd)


def moe_swiglu(h, w_router, e_gate, e_up, e_down):
    probs = jax.nn.softmax(jnp.einsum('btd,de->bte', h, w_router).astype(jnp.float32), axis=-1)
    top_v, top_i = lax.top_k(probs, TOP_K)
    top_v = top_v / jnp.sum(top_v, -1, keepdims=True)
    combine = jnp.sum(jax.nn.one_hot(top_i, N_EXPERTS, dtype=jnp.float32) * top_v[..., None], axis=-2)
    y = jnp.zeros_like(h)
    for e in range(N_EXPERTS):
        y = y + combine[..., e:e + 1].astype(h.dtype) * swiglu(h, e_gate[e], e_up[e], e_down[e])
    return y


def trunk(x, c, s_all, w_ada, b_ada, w_in, w_out, a_ln_w, a_ln_b, a_ws, a_bs, lb_all, b_norm_w,
          ln1_w, ln1_b, ln2_w, ln2_b, w_ff_gate, w_ff_up, w_ff_down, w_router, e_gate, e_up, e_down):
    c_act = jax.nn.silu(c)
    states, vrows = [], []
    for l in range(DEPTH):
        mod = jnp.einsum('bd,de->be', c_act, w_ada[l]) + b_ada[l]
        sh1, sc1, gt1, sh2, sc2, gt2 = [m[:, None, :] for m in jnp.split(mod, 6, axis=-1)]
        h = x * (1.0 + sc1) + sh1
        y, s_new, vn = token_mixer(h, s_all[l], w_in[l], w_out[l], a_ln_w[l], a_ln_b[l], a_ws[l], a_bs[l],
                                   lb_all[l], b_norm_w[l])
        x = layer_norm(ALPHA * x + gt1 * y, ln1_w[l], ln1_b[l])
        h = x * (1.0 + sc2) + sh2
        if l % 2 == 0:
            f_out = swiglu(h, w_ff_gate[l // 2], w_ff_up[l // 2], w_ff_down[l // 2])
        else:
            f_out = moe_swiglu(h, w_router[l // 2], e_gate[l // 2], e_up[l // 2], e_down[l // 2])
        x = layer_norm(ALPHA * x + gt2 * f_out, ln2_w[l], ln2_b[l])
        states.append(s_new)
        vrows.append(vn)
    return x, jnp.stack(states, axis=0), jnp.stack(vrows, axis=0)


def setup_inputs(seed: int = 0) -> dict:
    key = jax.random.key(seed)
    ks = jax.random.split(key, 32)

    def nrm(k, shape, scale):
        return jax.random.normal(k, shape, jnp.float32) * scale

    d = D_MODEL
    gate_offset = jnp.concatenate([jnp.zeros((2 * d,), jnp.float32), jnp.ones((d,), jnp.float32),
                                   jnp.zeros((2 * d,), jnp.float32), jnp.ones((d,), jnp.float32)])
    return {
        'x_prompt': nrm(ks[0], (BATCH, SEQ, d), 1.0),
        'x_sample': nrm(ks[1], (DEC_BATCH, DEC_SEQ, d), 1.0),
        'state_hgrn': nrm(ks[2], (DEPTH, DEC_BATCH, B_HEADS, B_DK, B_DV), 0.5),
        'c_prompt': nrm(ks[3], (BATCH, d), 1.0),
        'c_sample': nrm(ks[4], (DEC_BATCH, d), 1.0),
        'w_ada': nrm(ks[5], (DEPTH, d, 6 * d), 0.2 * d ** -0.5),
        'b_ada': nrm(ks[6], (DEPTH, 6 * d), 0.02) + gate_offset,
        'w_in': nrm(ks[7], (DEPTH, d, IN_COLS), d ** -0.5),
        'w_out': nrm(ks[8], (DEPTH, MIX_WIDTH, d), BETA * MIX_WIDTH ** -0.5),
        'a_ln_w': 1.0 + nrm(ks[9], (DEPTH, A_WIDTH), 0.02),
        'a_ln_b': nrm(ks[10], (DEPTH, A_WIDTH), 0.02),
        'a_ws': nrm(ks[11], (DEPTH, A_HEADS, CHUNK_A, CHUNK_A), CHUNK_A ** -0.5),
        'a_bs': 1.0 + nrm(ks[12], (DEPTH, A_HEADS, CHUNK_A), 0.02),
        'lb_logits': nrm(ks[13], (DEPTH, B_QK), 0.5),
        'b_norm_w': 1.0 + nrm(ks[14], (DEPTH, B_DV), 0.02),
        'ln1_w': 1.0 + nrm(ks[15], (DEPTH, d), 0.02),
        'ln1_b': nrm(ks[16], (DEPTH, d), 0.02),
        'ln2_w': 1.0 + nrm(ks[17], (DEPTH, d), 0.02),
        'ln2_b': nrm(ks[18], (DEPTH, d), 0.02),
        'w_ff_gate': nrm(ks[19], (N_DENSE, d, D_FF), d ** -0.5),
        'w_ff_up': nrm(ks[20], (N_DENSE, d, D_FF), d ** -0.5),
        'w_ff_down': nrm(ks[21], (N_DENSE, D_FF, d), BETA * D_FF ** -0.5),
        'w_router': nrm(ks[22], (N_MOE, d, N_EXPERTS), d ** -0.5),
        'e_gate': nrm(ks[23], (N_MOE, N_EXPERTS, d, D_FF), d ** -0.5),
        'e_up': nrm(ks[24], (N_MOE, N_EXPERTS, d, D_FF), d ** -0.5),
        'e_down': nrm(ks[25], (N_MOE, N_EXPERTS, D_FF, d), BETA * D_FF ** -0.5),
    }


def reference(x_prompt, x_sample, state_hgrn, c_prompt, c_sample, w_ada, b_ada, w_in, w_out, a_ln_w, a_ln_b,
              a_ws, a_bs, lb_logits, b_norm_w, ln1_w, ln1_b, ln2_w, ln2_b, w_ff_gate, w_ff_up, w_ff_down,
              w_router, e_gate, e_up, e_down):
    p = jax.nn.softmax(lb_logits.astype(jnp.float32), axis=0)
    lb_all = jnp.cumsum(p, axis=0) - p[0:1]
    weights = (w_ada, b_ada, w_in, w_out, a_ln_w, a_ln_b, a_ws, a_bs, lb_all, b_norm_w,
               ln1_w, ln1_b, ln2_w, ln2_b, w_ff_gate, w_ff_up, w_ff_down, w_router, e_gate, e_up, e_down)
    s_prompt0 = jnp.zeros((DEPTH, x_prompt.shape[0], B_HEADS, B_DK, B_DV), state_hgrn.dtype)
    y_prompt, state_hgrn_prompt, chunkv_prompt = trunk(x_prompt, c_prompt, s_prompt0, *weights)
    y_sample, state_hgrn_sample, chunkv_sample = trunk(x_sample, c_sample, state_hgrn, *weights)
    return (y_prompt, y_sample, state_hgrn_prompt, state_hgrn_sample, chunkv_prompt, chunkv_sample)
```

```python
import functools
import math

import numpy as np
import jax
import jax.numpy as jnp
from jax import lax
from jax.experimental import pallas as pl
from jax.experimental.pallas import tpu as pltpu

F32 = jnp.float32
BF16 = jnp.bfloat16

D_MODEL = 1024
DEPTH = 4
HALF = 512
N_HEADS = 4
HEAD = 128
CHUNK = 128
IN_COLS = 6 * HALF
D_FF = 2816
N_EXPERTS = 8
ALPHA = (2.0 * DEPTH) ** 0.25
LN_EPS = 1e-5
RMS_EPS = 1e-6

VMEM_LIMIT_BYTES = 56 * 1024 * 1024
LANE = 128
SUBLANE = 8

PROMPT_TILE = 512
SAMPLE_SEQS = 16
FFN_TILE = 512
EXPERT_TILE = 256
GATHER_CHUNK = 2048
GATHER_WINDOW = 128
FF_SPLITS = ((0, 1024), (1024, 2048), (2048, D_FF))


def _dot(a, b):
    return jnp.dot(a, b, preferred_element_type=F32)


def _dot_nt(a, b):
    return lax.dot_general(a, b, (((1,), (1,)), ((), ())), preferred_element_type=F32)


def _gelu(x):
    return 0.5 * x * (1.0 + lax.erf(x * (1.0 / math.sqrt(2.0))))


def _silu(x):
    return x * jax.nn.sigmoid(x)


def _layer_norm(z, w, b):
    mu = jnp.mean(z, axis=-1, keepdims=True)
    zc = z - mu
    var = jnp.mean(zc * zc, axis=-1, keepdims=True)
    return zc * lax.rsqrt(var + LN_EPS) * w + b


def _level_ids(block):
    t = np.arange(CHUNK)[:, None]
    s = np.arange(CHUNK)[None, :]
    x = t ^ s
    lv = np.where(x == 0, 0, np.floor(np.log2(np.maximum(x, 1))).astype(np.int64) + 1)
    ok = (s <= t) & (x < block)
    return np.where(ok, lv, -1).astype(np.int32)


def _reference_rows(b, m):
    rows, width = b.shape
    two_m = 2 * m
    if two_m >= SUBLANE:
        nb = rows // two_m
        b3 = b.reshape(nb, two_m, width)
        r = jnp.broadcast_to(b3[:, m - 1:m, :], (nb, two_m, width))
        return r.reshape(rows, width)
    t = lax.broadcasted_iota(jnp.int32, (rows, width), 0)
    tm = t & (two_m - 1)
    down1 = pltpu.roll(b, 1, 0)
    if m == 1:
        return jnp.where(tm == 0, b, down1)
    up1 = pltpu.roll(b, rows - 1, 0)
    down2 = pltpu.roll(b, 2, 0)
    return jnp.where(tm == 0, up1, jnp.where(tm == 1, b, jnp.where(tm == 2, down1, down2)))


def _hgrn_intra(qq, kk, gg, vv, lv, block):
    cm = (lv >= 0).astype(BF16)
    g_hi = gg.astype(BF16)
    rem = gg - g_hi.astype(F32)
    g_mid = rem.astype(BF16)
    g_lo = (rem - g_mid.astype(F32)).astype(BF16)
    b = _dot(cm, g_hi) + _dot(cm, g_mid) + _dot(cm, g_lo)

    heads = [slice(h * HEAD, (h + 1) * HEAD) for h in range(N_HEADS)]
    qb = qq.astype(BF16)
    kb = kk.astype(BF16)
    scores = [jnp.where(lv == 0, _dot_nt(qb[:, hs], kb[:, hs]), 0.0) for hs in heads]
    m = block // 2
    while m >= 1:
        level = int(math.log2(m)) + 1
        e = jnp.exp(-jnp.abs(b - _reference_rows(b, m)))
        qe = (qq * e).astype(BF16)
        ke = (kk * e).astype(BF16)
        for h, hs in enumerate(heads):
            scores[h] = jnp.where(lv == level, _dot_nt(qe[:, hs], ke[:, hs]), scores[h])
        m //= 2
    vb = vv.astype(BF16)
    o = jnp.concatenate([_dot(scores[h].astype(BF16), vb[:, hs]) for h, hs in enumerate(heads)], axis=1)
    return o, b


def _forget_bound(lbl_ref, layer):
    z = lbl_ref[...]
    z = z - jnp.max(z, axis=0, keepdims=True)
    ez = jnp.exp(z)
    p = ez / jnp.sum(ez, axis=0, keepdims=True)
    c = p[0:1]
    for r in range(1, layer + 1):
        c = c + p[r:r + 1]
    return c - p[0:1]


def _mixer_chunk_front(proj_scr, rows, alnw_ref, alnb_ref, ws_ref, bs_ref, lb, lv):
    u = proj_scr[rows, 0 * HALF:1 * HALF]
    v = proj_scr[rows, 1 * HALF:2 * HALF]
    q = proj_scr[rows, 2 * HALF:3 * HALF]
    f = proj_scr[rows, 3 * HALF:4 * HALF]
    ug = _gelu(u)
    vn = _layer_norm(_gelu(v), alnw_ref[...], alnb_ref[...])
    vnb = vn.astype(BF16)
    a_parts = []
    for h in range(N_HEADS):
        hs = slice(h * HEAD, (h + 1) * HEAD)
        w = jnp.where(lv >= 0, ws_ref[h], 0.0).astype(BF16)
        mixed = _dot(w, vnb[:, hs]) + bs_ref[:, h:h + 1]
        a_parts.append(ug[:, hs] * mixed)
    a_out = jnp.concatenate(a_parts, axis=1)
    fg = lb + (1.0 - lb) * jax.nn.sigmoid(f)
    return a_out, vn, _silu(q), 1.0 - fg, jnp.log(fg)


def _rms_gate(o, bnw, g):
    parts = []
    for h in range(N_HEADS):
        hs = slice(h * HEAD, (h + 1) * HEAD)
        oh = o[:, hs]
        parts.append(oh * lax.rsqrt(jnp.mean(oh * oh, axis=-1, keepdims=True) + RMS_EPS) * bnw)
    return jnp.concatenate(parts, axis=1) * _silu(g)


def _mixer_prompt_kernel(x_ref, mod_ref, win_ref, wout_ref, alnw_ref, alnb_ref, ws_ref, bs_ref,
                         lbl_ref, bnw_ref, l1w_ref, l1b_ref, lv_ref,
                         x1_ref, st_ref, vn_ref,
                         proj_scr, mix_scr, s_scr, *, layer, tile):
    step = pl.program_id(1)

    @pl.when(step == 0)
    def _():
        s_scr[...] = jnp.zeros_like(s_scr)

    mod = mod_ref[0, 0]
    x = x_ref[0]
    h = (x * (1.0 + mod[1:2]) + mod[0:1]).astype(BF16)
    proj_scr[...] = _dot(h, win_ref[...])
    lv = lv_ref[...]
    lb = _forget_bound(lbl_ref, layer)

    def chunk(c, carry):
        rows = pl.ds(pl.multiple_of(c * CHUNK, CHUNK), CHUNK)
        a_out, vn, qq, kk, gg = _mixer_chunk_front(proj_scr, rows, alnw_ref, alnb_ref, ws_ref, bs_ref, lb, lv)
        vn_ref[0] = vn
        vv = proj_scr[rows, 4 * HALF:5 * HALF]
        o_in, b = _hgrn_intra(qq, kk, gg, vv, lv, CHUNK)
        qh = (qq * jnp.exp(b)).astype(BF16)
        b_last = b[CHUNK - 1:CHUNK, :]
        kdec = kk * jnp.exp(b_last - b)
        e_last = jnp.exp(b_last)
        o_parts = []
        for hh in range(N_HEADS):
            hs = slice(hh * HEAD, (hh + 1) * HEAD)
            s_old = s_scr[hh]
            o_parts.append(o_in[:, hs] + _dot(qh[:, hs], s_old.astype(BF16)))
            dec = jnp.broadcast_to(e_last[:, hs], (HEAD, HEAD)).T
            s_scr[hh] = dec * s_old + _dot(kdec[:, hs].T.astype(BF16), vv[:, hs].astype(BF16))
        g = proj_scr[rows, 5 * HALF:6 * HALF]
        b_out = _rms_gate(jnp.concatenate(o_parts, axis=1), bnw_ref[...], g)
        mix_scr[rows, 0:HALF] = a_out.astype(BF16)
        mix_scr[rows, HALF:2 * HALF] = b_out.astype(BF16)
        return carry

    lax.fori_loop(0, tile // CHUNK, chunk, 0)

    y = _dot(mix_scr[...], wout_ref[...])
    x1_ref[0] = _layer_norm(ALPHA * x + mod[2:3] * y, l1w_ref[...], l1b_ref[...])

    @pl.when(step == pl.num_programs(1) - 1)
    def _():
        st_ref[0] = s_scr[...]


def _mixer_sample_kernel(x_ref, mod_ref, s0_ref, win_ref, wout_ref, alnw_ref, alnb_ref, ws_ref, bs_ref,
                         lbl_ref, bnw_ref, l1w_ref, l1b_ref, lv_ref,
                         x1_ref, st_ref, vn_ref,
                         proj_scr, qh_scr, kt_scr, vb_scr, el_scr, o_scr, *, layer, seq_len):
    nseq = SAMPLE_SEQS
    x3 = x_ref[...]
    mod = mod_ref[0]
    h3 = x3 * (1.0 + mod[1][:, None, :]) + mod[0][:, None, :]
    proj_scr[...] = _dot(h3.reshape(CHUNK, D_MODEL).astype(BF16), win_ref[...])
    lv = lv_ref[...]
    lb = _forget_bound(lbl_ref, layer)
    rows = slice(0, CHUNK)
    a_out, vn, qq, kk, gg = _mixer_chunk_front(proj_scr, rows, alnw_ref, alnb_ref, ws_ref, bs_ref, lb, lv)
    vn_ref[...] = vn.reshape(nseq, seq_len, HALF)
    vv = proj_scr[rows, 4 * HALF:5 * HALF]
    o_in, b = _hgrn_intra(qq, kk, gg, vv, lv, seq_len)
    qh_scr[...] = qq * jnp.exp(b)
    b3 = b.reshape(nseq, seq_len, HALF)
    b_last = jnp.broadcast_to(b3[:, seq_len - 1:seq_len, :], (nseq, seq_len, HALF))
    el_scr[...] = jnp.exp(b_last)
    kdec = kk * jnp.exp(b_last.reshape(CHUNK, HALF) - b)
    for hh in range(N_HEADS):
        hs = slice(hh * HEAD, (hh + 1) * HEAD)
        kt_scr[hh] = kdec[:, hs].T.astype(BF16)
    vb_scr[...] = vv.astype(BF16)
    row_seq = lax.broadcasted_iota(jnp.int32, (HEAD, CHUNK), 1) // seq_len

    def per_seq(j, carry):
        rws = pl.ds(pl.multiple_of(j * seq_len, seq_len), seq_len)
        own = row_seq == j
        el = el_scr[j]
        for hh in range(N_HEADS):
            hs = slice(hh * HEAD, (hh + 1) * HEAD)
            s_old = s0_ref[0, j, hh]
            o_scr[rws, hs] = _dot(qh_scr[rws, hs].astype(BF16), s_old.astype(BF16))
            dec = jnp.broadcast_to(el[0:1, hs], (HEAD, HEAD)).T
            kt = jnp.where(own, kt_scr[hh], jnp.zeros((), BF16))
            st_ref[j, hh] = dec * s_old + _dot(kt, vb_scr[:, hs])
        return carry

    lax.fori_loop(0, nseq, per_seq, 0)

    g = proj_scr[rows, 5 * HALF:6 * HALF]
    b_out = _rms_gate(o_in + o_scr[...], bnw_ref[...], g)
    mix = jnp.concatenate([a_out, b_out], axis=1).astype(BF16)
    y3 = _dot(mix, wout_ref[...]).reshape(nseq, seq_len, D_MODEL)
    x1_ref[...] = _layer_norm(ALPHA * x3 + mod[2][:, None, :] * y3, l1w_ref[...], l1b_ref[...])


def _mod_rows(mod_ref, per_seq, j):
    if per_seq:
        return mod_ref[0, j][:, None, :]
    return mod_ref[0, 0][j:j + 1][None]


def _swiglu(h, wg_ref, wu_ref, wd_ref):
    acc = None
    for f0, f1 in FF_SPLITS:
        act = (_silu(_dot(h, wg_ref[:, f0:f1])) * _dot(h, wu_ref[:, f0:f1])).astype(BF16)
        part = _dot(act, wd_ref[f0:f1, :])
        acc = part if acc is None else acc + part
    return acc


def _ffn_dense_kernel(x_ref, mod_ref, wg_ref, wu_ref, wd_ref, l2w_ref, l2b_ref, o_ref, *, per_seq):
    x3 = x_ref[...]
    groups, rows, _ = x3.shape
    h = (x3 * (1.0 + _mod_rows(mod_ref, per_seq, 4)) + _mod_rows(mod_ref, per_seq, 3))
    h = h.reshape(groups * rows, D_MODEL).astype(BF16)
    f3 = _swiglu(h, wg_ref, wu_ref, wd_ref).reshape(groups, rows, D_MODEL)
    o_ref[...] = _layer_norm(ALPHA * x3 + _mod_rows(mod_ref, per_seq, 5) * f3, l2w_ref[...], l2b_ref[...])


def _router_kernel(x_ref, mod_ref, wr_ref, h_ref, idx_ref, gate_ref, *, per_seq):
    x3 = x_ref[...]
    groups, rows, _ = x3.shape
    n = groups * rows
    h = (x3 * (1.0 + _mod_rows(mod_ref, per_seq, 4)) + _mod_rows(mod_ref, per_seq, 3)).reshape(n, D_MODEL)
    h_ref[...] = h
    logits = jnp.dot(h, wr_ref[...], precision=lax.Precision.HIGHEST, preferred_element_type=F32)
    z = jnp.exp(logits - jnp.max(logits, axis=-1, keepdims=True))
    p = z / jnp.sum(z, axis=-1, keepdims=True)
    lane = lax.broadcasted_iota(jnp.int32, p.shape, 1)
    p1 = jnp.max(p, axis=-1, keepdims=True)
    i1 = jnp.min(jnp.where(p == p1, lane, N_EXPERTS), axis=-1, keepdims=True)
    rest = jnp.where(lane == i1, -1.0, p)
    p2 = jnp.max(rest, axis=-1, keepdims=True)
    i2 = jnp.min(jnp.where(rest == p2, lane, N_EXPERTS), axis=-1, keepdims=True)
    two = lax.broadcasted_iota(jnp.int32, (n, 2), 1)
    idx_ref[...] = jnp.where(two == 0, i1, i2)
    gate_ref[...] = jnp.where(two == 0, p1, p2) / (p1 + p2)


def _gather_rows_kernel(idx_ref, src_ref, out_ref, sem):
    base = pl.program_id(0) * GATHER_CHUNK

    def copy(i):
        return pltpu.make_async_copy(src_ref.at[idx_ref[0, 0, i]], out_ref.at[base + i], sem)

    def start(i, carry):
        copy(i).start()
        return carry

    def wait_then_start(i, carry):
        copy(i).wait()
        copy(i + GATHER_WINDOW).start()
        return carry

    def wait(i, carry):
        copy(i).wait()
        return carry

    lax.fori_loop(0, GATHER_WINDOW, start, 0)
    lax.fori_loop(0, GATHER_CHUNK - GATHER_WINDOW, wait_then_start, 0)
    lax.fori_loop(GATHER_CHUNK - GATHER_WINDOW, GATHER_CHUNK, wait, 0)


def _expert_kernel(te_ref, hg_ref, gate_ref, wg_ref, wu_ref, wd_ref, o_ref, h_scr):
    del te_ref
    for s in range(D_MODEL // LANE):
        h_scr[:, s * LANE:(s + 1) * LANE] = hg_ref[:, s, :].astype(BF16)
    y = gate_ref[...] * _swiglu(h_scr[...], wg_ref.at[0], wu_ref.at[0], wd_ref.at[0])
    for s in range(D_MODEL // LANE):
        o_ref[:, s, :] = y[:, s * LANE:(s + 1) * LANE]


def _combine_kernel(x_ref, mod_ref, ya_ref, yb_ref, l2w_ref, l2b_ref, o_ref, f_scr, *, per_seq):
    x3 = x_ref[...]
    groups, rows, _ = x3.shape
    for s in range(D_MODEL // LANE):
        f_scr[:, s * LANE:(s + 1) * LANE] = ya_ref[:, s, :] + yb_ref[:, s, :]
    f3 = f_scr[...].reshape(groups, rows, D_MODEL)
    o_ref[...] = _layer_norm(ALPHA * x3 + _mod_rows(mod_ref, per_seq, 5) * f3, l2w_ref[...], l2b_ref[...])


def _params(*semantics):
    return pltpu.CompilerParams(dimension_semantics=semantics, vmem_limit_bytes=VMEM_LIMIT_BYTES)


def _const_spec(shape):
    return pl.BlockSpec(shape, lambda *_: (0,) * len(shape), pipeline_mode=pl.Buffered(1))


def _adaln(c_all, w_ada, b_ada):
    nb = c_all.shape[0]
    tn = 1536

    def body(c_ref, w_ref, b_ref, o_ref):
        c = c_ref[...]
        o_ref[0] = _dot(_silu(c).astype(BF16), w_ref[0].astype(BF16)) + b_ref[0]

    return pl.pallas_call(
        body,
        out_shape=jax.ShapeDtypeStruct((DEPTH, nb, 6 * D_MODEL), F32),
        grid=(DEPTH, 6 * D_MODEL // tn),
        in_specs=[pl.BlockSpec((nb, D_MODEL), lambda l, j: (0, 0)),
                  pl.BlockSpec((1, D_MODEL, tn), lambda l, j: (l, 0, j)),
                  pl.BlockSpec((1, 1, tn), lambda l, j: (l, 0, j))],
        out_specs=pl.BlockSpec((1, nb, tn), lambda l, j: (l, 0, j)),
        compiler_params=_params("arbitrary", "arbitrary"),
        name="adaln_modulation",
    )(c_all, w_ada, b_ada.reshape(DEPTH, 1, 6 * D_MODEL))


def _mixer_weight_specs():
    return [_const_spec((D_MODEL, IN_COLS)), _const_spec((D_MODEL, D_MODEL)),
            _const_spec((1, HALF)), _const_spec((1, HALF)),
            _const_spec((N_HEADS, CHUNK, CHUNK)), _const_spec((CHUNK, N_HEADS)),
            _const_spec((DEPTH, HALF)), _const_spec((1, HEAD)),
            _const_spec((1, D_MODEL)), _const_spec((1, D_MODEL)), _const_spec((CHUNK, CHUNK))]


def _mixer_prompt(layer, x, mod_p, wts, lv):
    batch, seq, _ = x.shape
    tile = PROMPT_TILE
    rows_out = seq - CHUNK * ((seq - 1) // CHUNK)
    assert seq % tile == 0 and rows_out == CHUNK
    return pl.pallas_call(
        functools.partial(_mixer_prompt_kernel, layer=layer, tile=tile),
        out_shape=(jax.ShapeDtypeStruct((batch, seq, D_MODEL), F32),
                   jax.ShapeDtypeStruct((batch, N_HEADS, HEAD, HEAD), F32),
                   jax.ShapeDtypeStruct((batch, CHUNK, HALF), F32)),
        grid=(batch, seq // tile),
        in_specs=[pl.BlockSpec((1, tile, D_MODEL), lambda b, s: (b, s, 0)),
                  pl.BlockSpec((1, 1, 6, D_MODEL), lambda b, s: (layer, b, 0, 0))] + _mixer_weight_specs(),
        out_specs=(pl.BlockSpec((1, tile, D_MODEL), lambda b, s: (b, s, 0)),
                   pl.BlockSpec((1, N_HEADS, HEAD, HEAD), lambda b, s: (b, 0, 0, 0)),
                   pl.BlockSpec((1, CHUNK, HALF), lambda b, s: (b, 0, 0))),
        scratch_shapes=[pltpu.VMEM((tile, IN_COLS), F32), pltpu.VMEM((tile, D_MODEL), BF16),
                        pltpu.VMEM((N_HEADS, HEAD, HEAD), F32)],
        compiler_params=_params("arbitrary", "arbitrary"),
        name="token_mixer_prompt",
    )(x, mod_p, *wts, lv)


def _mixer_sample(layer, x, mod_s, state, wts, lv):
    nseq_all, seq_len, _ = x.shape
    nseq = SAMPLE_SEQS
    assert nseq * seq_len == CHUNK and nseq_all % nseq == 0 and seq_len == SUBLANE
    return pl.pallas_call(
        functools.partial(_mixer_sample_kernel, layer=layer, seq_len=seq_len),
        out_shape=(jax.ShapeDtypeStruct((nseq_all, seq_len, D_MODEL), F32),
                   jax.ShapeDtypeStruct((nseq_all, N_HEADS, HEAD, HEAD), F32),
                   jax.ShapeDtypeStruct((nseq_all, seq_len, HALF), F32)),
        grid=(nseq_all // nseq,),
        in_specs=[pl.BlockSpec((nseq, seq_len, D_MODEL), lambda j: (j, 0, 0)),
                  pl.BlockSpec((1, 6, nseq, D_MODEL), lambda j: (layer, 0, j, 0)),
                  pl.BlockSpec((1, nseq, N_HEADS, HEAD, HEAD), lambda j: (layer, j, 0, 0, 0))]
        + _mixer_weight_specs(),
        out_specs=(pl.BlockSpec((nseq, seq_len, D_MODEL), lambda j: (j, 0, 0)),
                   pl.BlockSpec((nseq, N_HEADS, HEAD, HEAD), lambda j: (j, 0, 0, 0)),
                   pl.BlockSpec((nseq, seq_len, HALF), lambda j: (j, 0, 0))),
        scratch_shapes=[pltpu.VMEM((CHUNK, IN_COLS), F32), pltpu.VMEM((CHUNK, HALF), F32),
                        pltpu.VMEM((N_HEADS, HEAD, CHUNK), BF16), pltpu.VMEM((CHUNK, HALF), BF16),
                        pltpu.VMEM((nseq, seq_len, HALF), F32), pltpu.VMEM((CHUNK, HALF), F32)],
        compiler_params=_params("arbitrary"),
        name="token_mixer_sample",
    )(x, mod_s, state, *wts, lv)


def _row_blocking(x, per_seq, tile):
    batch, seq, _ = x.shape
    if per_seq:
        groups = tile // seq
        assert batch % groups == 0
        grid = (batch // groups,)
        x_spec = pl.BlockSpec((groups, seq, D_MODEL), lambda i: (i, 0, 0))
        return grid, x_spec, groups, lambda layer: pl.BlockSpec((1, 6, groups, D_MODEL), lambda i: (layer, 0, i, 0))
    assert seq % tile == 0
    per = seq // tile
    grid = (batch * per,)
    x_spec = pl.BlockSpec((1, tile, D_MODEL), lambda i: (i // per, i % per, 0))
    return grid, x_spec, per, lambda layer: pl.BlockSpec((1, 1, 6, D_MODEL), lambda i: (layer, i // per, 0, 0))


def _ffn_dense(layer, x, mod, per_seq, wg, wu, wd, l2w, l2b):
    tile = FFN_TILE if not per_seq else CHUNK
    grid, x_spec, _, mod_spec = _row_blocking(x, per_seq, tile)
    return pl.pallas_call(
        functools.partial(_ffn_dense_kernel, per_seq=per_seq),
        out_shape=jax.ShapeDtypeStruct(x.shape, F32),
        grid=grid,
        in_specs=[x_spec, mod_spec(layer),
                  _const_spec((D_MODEL, D_FF)), _const_spec((D_MODEL, D_FF)), _const_spec((D_FF, D_MODEL)),
                  _const_spec((1, D_MODEL)), _const_spec((1, D_MODEL))],
        out_specs=x_spec,
        compiler_params=_params("arbitrary"),
        name="ffn_dense",
    )(x, mod, wg, wu, wd, l2w, l2b)


def _router(layer, x, mod, per_seq, w_router):
    batch, seq, _ = x.shape
    tile = FFN_TILE if not per_seq else CHUNK
    grid, x_spec, _, mod_spec = _row_blocking(x, per_seq, tile)
    n = batch * seq
    return pl.pallas_call(
        functools.partial(_router_kernel, per_seq=per_seq),
        out_shape=(jax.ShapeDtypeStruct((n, D_MODEL), F32),
                   jax.ShapeDtypeStruct((n, 2), jnp.int32),
                   jax.ShapeDtypeStruct((n, 2), F32)),
        grid=grid,
        in_specs=[x_spec, mod_spec(layer), _const_spec((D_MODEL, N_EXPERTS))],
        out_specs=(pl.BlockSpec((tile, D_MODEL), lambda i: (i, 0)),
                   pl.BlockSpec((tile, 2), lambda i: (i, 0)),
                   pl.BlockSpec((tile, 2), lambda i: (i, 0))),
        compiler_params=_params("arbitrary"),
        name="moe_router",
    )(x, mod, w_router)


def _gather_rows(src, idx):
    m = idx.shape[0]
    assert m % GATHER_CHUNK == 0
    nchunk = m // GATHER_CHUNK
    return pl.pallas_call(
        _gather_rows_kernel,
        out_shape=jax.ShapeDtypeStruct((m,) + src.shape[1:], src.dtype),
        grid=(nchunk,),
        in_specs=[pl.BlockSpec((1, 1, GATHER_CHUNK), lambda c: (c, 0, 0), memory_space=pltpu.SMEM),
                  pl.BlockSpec(memory_space=pl.ANY)],
        out_specs=pl.BlockSpec(memory_space=pl.ANY),
        scratch_shapes=[pltpu.SemaphoreType.DMA(())],
        compiler_params=_params("arbitrary"),
        name="gather_rows",
    )(idx.reshape(nchunk, 1, GATHER_CHUNK), src)


def _experts(hg, gate_sorted, tile_expert, wg, wu, wd):
    mp = hg.shape[0]
    tile = EXPERT_TILE
    return pl.pallas_call(
        _expert_kernel,
        out_shape=jax.ShapeDtypeStruct(hg.shape, F32),
        grid_spec=pltpu.PrefetchScalarGridSpec(
            num_scalar_prefetch=1,
            grid=(mp // tile,),
            in_specs=[pl.BlockSpec((tile, SUBLANE, LANE), lambda i, te: (i, 0, 0)),
                      pl.BlockSpec((tile, 1), lambda i, te: (i, 0)),
                      pl.BlockSpec((1, D_MODEL, D_FF), lambda i, te: (te[i], 0, 0)),
                      pl.BlockSpec((1, D_MODEL, D_FF), lambda i, te: (te[i], 0, 0)),
                      pl.BlockSpec((1, D_FF, D_MODEL), lambda i, te: (te[i], 0, 0))],
            out_specs=pl.BlockSpec((tile, SUBLANE, LANE), lambda i, te: (i, 0, 0)),
            scratch_shapes=[pltpu.VMEM((tile, D_MODEL), BF16)]),
        compiler_params=_params("arbitrary"),
        name="moe_experts",
    )(tile_expert, hg, gate_sorted, wg, wu, wd)


def _combine(layer, x, mod, per_seq, yg, row0, l2w, l2b):
    batch, seq, _ = x.shape
    tile = FFN_TILE if not per_seq else CHUNK
    grid, x_spec, _, mod_spec = _row_blocking(x, per_seq, tile)
    half = yg.shape[0] // 2
    assert row0 % tile == 0 and half % tile == 0
    a0, b0 = row0 // tile, (half + row0) // tile
    return pl.pallas_call(
        functools.partial(_combine_kernel, per_seq=per_seq),
        out_shape=jax.ShapeDtypeStruct(x.shape, F32),
        grid=grid,
        in_specs=[x_spec, mod_spec(layer),
                  pl.BlockSpec((tile, SUBLANE, LANE), lambda i: (a0 + i, 0, 0)),
                  pl.BlockSpec((tile, SUBLANE, LANE), lambda i: (b0 + i, 0, 0)),
                  _const_spec((1, D_MODEL)), _const_spec((1, D_MODEL))],
        out_specs=x_spec,
        scratch_shapes=[pltpu.VMEM((tile, D_MODEL), F32)],
        compiler_params=_params("arbitrary"),
        name="moe_combine",
    )(x, mod, yg, yg, l2w, l2b)


def _routing_tables(idx, gate, n_pad_rows):
    n = idx.shape[0]
    e_flat = jnp.concatenate([idx[:, 0], idx[:, 1]])
    g_flat = jnp.concatenate([gate[:, 0], gate[:, 1]])
    onehot = (e_flat[:, None] == jnp.arange(N_EXPERTS)[None, :]).astype(jnp.int32)
    csum = jnp.cumsum(onehot, axis=0)
    counts = csum[-1]
    rank = jnp.take_along_axis(csum, e_flat[:, None], axis=1)[:, 0] - 1
    padded = ((counts + EXPERT_TILE - 1) // EXPERT_TILE) * EXPERT_TILE
    pend = jnp.cumsum(padded)
    dest = (pend - padded)[e_flat] + rank
    tok = jnp.concatenate([jnp.arange(n, dtype=jnp.int32)] * 2)
    src_tok = jnp.zeros((n_pad_rows,), jnp.int32).at[dest].set(tok)
    gate_sorted = jnp.zeros((n_pad_rows,), F32).at[dest].set(g_flat)
    tile_start = jnp.arange(n_pad_rows // EXPERT_TILE, dtype=jnp.int32) * EXPERT_TILE
    tile_expert = jnp.minimum(jnp.searchsorted(pend, tile_start, side="right"), N_EXPERTS - 1).astype(jnp.int32)
    return src_tok, gate_sorted.reshape(n_pad_rows, 1), tile_expert, dest.astype(jnp.int32)


def _round_up(a, b):
    return (a + b - 1) // b * b


def kernel(x_prompt, x_sample, state_hgrn, c_prompt, c_sample, w_ada, b_ada, w_in, w_out, a_ln_w, a_ln_b, a_ws, a_bs, lb_logits, b_norm_w, ln1_w, ln1_b, ln2_w, ln2_b, w_ff_gate, w_ff_up, w_ff_down, w_router, e_gate, e_up, e_down):
    batch, seq, _ = x_prompt.shape
    nseq, seq_len, _ = x_sample.shape
    n_prompt, n_sample = batch * seq, nseq * seq_len
    n_tok = n_prompt + n_sample

    mod = _adaln(jnp.concatenate([c_prompt, c_sample], axis=0), w_ada, b_ada)
    mod_p = mod[:, :batch].reshape(DEPTH, batch, 6, D_MODEL)
    mod_s = mod[:, batch:].reshape(DEPTH, nseq, 6, D_MODEL).transpose(0, 2, 1, 3)

    lv_p = jnp.asarray(_level_ids(CHUNK))
    lv_s = jnp.asarray(_level_ids(seq_len))
    reps = CHUNK // seq_len
    n_pad_rows = _round_up(2 * n_tok + N_EXPERTS * EXPERT_TILE, GATHER_CHUNK)
    n_comb_rows = _round_up(2 * n_tok, GATHER_CHUNK)

    xp, xs = x_prompt, x_sample
    st_p, st_s, cv_p, cv_s = [], [], [], []
    for l in range(DEPTH):
        shared = (a_ln_w[l][None], a_ln_b[l][None])
        tail = (lb_logits, b_norm_w[l][None], ln1_w[l][None], ln1_b[l][None])
        w_in_l, w_out_l = w_in[l].astype(BF16), w_out[l].astype(BF16)
        wts_p = (w_in_l, w_out_l) + shared + (a_ws[l], a_bs[l].T) + tail
        ws_s = jnp.tile(a_ws[l][:, :seq_len, :seq_len], (1, reps, reps))
        bs_s = jnp.tile(a_bs[l][:, :seq_len].T, (reps, 1))
        wts_s = (w_in_l, w_out_l) + shared + (ws_s, bs_s) + tail
        xp, sp, vp = _mixer_prompt(l, xp, mod_p, wts_p, lv_p)
        xs, ss, vs = _mixer_sample(l, xs, mod_s, state_hgrn, wts_s, lv_s)
        st_p.append(sp), st_s.append(ss), cv_p.append(vp), cv_s.append(vs)
        l2w, l2b = ln2_w[l][None], ln2_b[l][None]
        if l % 2 == 0:
            wg, wu, wd = (w[l // 2].astype(BF16) for w in (w_ff_gate, w_ff_up, w_ff_down))
            xp = _ffn_dense(l, xp, mod_p, False, wg, wu, wd, l2w, l2b)
            xs = _ffn_dense(l, xs, mod_s, True, wg, wu, wd, l2w, l2b)
        else:
            wr = w_router[l // 2]
            hp, ip, gp = _router(l, xp, mod_p, False, wr)
            hs, is_, gs = _router(l, xs, mod_s, True, wr)
            h_all = jnp.concatenate([hp, hs], axis=0).reshape(n_tok, SUBLANE, LANE)
            src_tok, gate_sorted, tile_expert, dest = _routing_tables(
                jnp.concatenate([ip, is_], axis=0), jnp.concatenate([gp, gs], axis=0), n_pad_rows)
            hg = _gather_rows(h_all, src_tok)
            wg, wu, wd = (w[l // 2].astype(BF16) for w in (e_gate, e_up, e_down))
            yo = _experts(hg, gate_sorted, tile_expert, wg, wu, wd)
            half = n_comb_rows // 2
            pos = jnp.concatenate([dest[:n_tok], jnp.zeros((half - n_tok,), jnp.int32),
                                   dest[n_tok:], jnp.zeros((half - n_tok,), jnp.int32)])
            yg = _gather_rows(yo, pos)
            xp = _combine(l, xp, mod_p, False, yg, 0, l2w, l2b)
            xs = _combine(l, xs, mod_s, True, yg, n_prompt, l2w, l2b)
    return (xp, xs, jnp.stack(st_p), jnp.stack(st_s), jnp.stack(cv_p), jnp.stack(cv_s))
```

```python
import functools
import math

import numpy as np
import jax
import jax.numpy as jnp
from jax import lax
from jax.experimental import pallas as pl
from jax.experimental.pallas import tpu as pltpu

F32 = jnp.float32
BF16 = jnp.bfloat16

D_MODEL = 1024
DEPTH = 4
HALF = 512
N_HEADS = 4
HEAD = 128
CHUNK = 128
IN_COLS = 6 * HALF
D_FF = 2816
N_EXPERTS = 8
ALPHA = (2.0 * DEPTH) ** 0.25
LN_EPS = 1e-5
RMS_EPS = 1e-6

VMEM_LIMIT_BYTES = 56 * 1024 * 1024
LANE = 128
SUBLANE = 8

PROMPT_TILE = 512
SAMPLE_SEQS = 16
FFN_TILE = 512
EXPERT_TILE = 256
FF_SPLITS = ((0, 1024), (1024, 2048), (2048, D_FF))


def _dot(a, b):
    return jnp.dot(a, b, preferred_element_type=F32)


def _dot_nt(a, b):
    return lax.dot_general(a, b, (((1,), (1,)), ((), ())), preferred_element_type=F32)


def _gelu(x):
    return 0.5 * x * (1.0 + lax.erf(x * (1.0 / math.sqrt(2.0))))


def _silu(x):
    return x * jax.nn.sigmoid(x)


def _layer_norm(z, w, b):
    mu = jnp.mean(z, axis=-1, keepdims=True)
    zc = z - mu
    var = jnp.mean(zc * zc, axis=-1, keepdims=True)
    return zc * lax.rsqrt(var + LN_EPS) * w + b


def _level_ids(block):
    t = np.arange(CHUNK)[:, None]
    s = np.arange(CHUNK)[None, :]
    x = t ^ s
    lv = np.where(x == 0, 0, np.floor(np.log2(np.maximum(x, 1))).astype(np.int64) + 1)
    ok = (s <= t) & (x < block)
    return np.where(ok, lv, -1).astype(np.int32)


def _reference_rows(b, m):
    rows, width = b.shape
    two_m = 2 * m
    if two_m >= SUBLANE:
        nb = rows // two_m
        b3 = b.reshape(nb, two_m, width)
        r = jnp.broadcast_to(b3[:, m - 1:m, :], (nb, two_m, width))
        return r.reshape(rows, width)
    t = lax.broadcasted_iota(jnp.int32, (rows, width), 0)
    tm = t & (two_m - 1)
    down1 = pltpu.roll(b, 1, 0)
    if m == 1:
        return jnp.where(tm == 0, b, down1)
    up1 = pltpu.roll(b, rows - 1, 0)
    down2 = pltpu.roll(b, 2, 0)
    return jnp.where(tm == 0, up1, jnp.where(tm == 1, b, jnp.where(tm == 2, down1, down2)))


def _hgrn_intra(qq, kk, gg, vv, lv, block):
    cm = (lv >= 0).astype(BF16)
    g_hi = gg.astype(BF16)
    rem = gg - g_hi.astype(F32)
    g_mid = rem.astype(BF16)
    g_lo = (rem - g_mid.astype(F32)).astype(BF16)
    b = _dot(cm, g_hi) + _dot(cm, g_mid) + _dot(cm, g_lo)

    heads = [slice(h * HEAD, (h + 1) * HEAD) for h in range(N_HEADS)]
    qb = qq.astype(BF16)
    kb = kk.astype(BF16)
    scores = [jnp.where(lv == 0, _dot_nt(qb[:, hs], kb[:, hs]), 0.0) for hs in heads]
    m = block // 2
    while m >= 1:
        level = int(math.log2(m)) + 1
        e = jnp.exp(-jnp.abs(b - _reference_rows(b, m)))
        qe = (qq * e).astype(BF16)
        ke = (kk * e).astype(BF16)
        for h, hs in enumerate(heads):
            scores[h] = jnp.where(lv == level, _dot_nt(qe[:, hs], ke[:, hs]), scores[h])
        m //= 2
    vb = vv.astype(BF16)
    o = jnp.concatenate([_dot(scores[h].astype(BF16), vb[:, hs]) for h, hs in enumerate(heads)], axis=1)
    return o, b


def _forget_bound(lbl_ref, layer):
    z = lbl_ref[...]
    z = z - jnp.max(z, axis=0, keepdims=True)
    ez = jnp.exp(z)
    p = ez / jnp.sum(ez, axis=0, keepdims=True)
    c = p[0:1]
    for r in range(1, layer + 1):
        c = c + p[r:r + 1]
    return c - p[0:1]


def _mixer_chunk_front(proj_scr, rows, alnw_ref, alnb_ref, ws_ref, bs_ref, lb, lv):
    u = proj_scr[rows, 0 * HALF:1 * HALF]
    v = proj_scr[rows, 1 * HALF:2 * HALF]
    q = proj_scr[rows, 2 * HALF:3 * HALF]
    f = proj_scr[rows, 3 * HALF:4 * HALF]
    ug = _gelu(u)
    vn = _layer_norm(_gelu(v), alnw_ref[...], alnb_ref[...])
    vnb = vn.astype(BF16)
    a_parts = []
    for h in range(N_HEADS):
        hs = slice(h * HEAD, (h + 1) * HEAD)
        w = jnp.where(lv >= 0, ws_ref[h], 0.0).astype(BF16)
        mixed = _dot(w, vnb[:, hs]) + bs_ref[:, h:h + 1]
        a_parts.append(ug[:, hs] * mixed)
    a_out = jnp.concatenate(a_parts, axis=1)
    fg = lb + (1.0 - lb) * jax.nn.sigmoid(f)
    return a_out, vn, _silu(q), 1.0 - fg, jnp.log(fg)


def _rms_gate(o, bnw, g):
    parts = []
    for h in range(N_HEADS):
        hs = slice(h * HEAD, (h + 1) * HEAD)
        oh = o[:, hs]
        parts.append(oh * lax.rsqrt(jnp.mean(oh * oh, axis=-1, keepdims=True) + RMS_EPS) * bnw)
    return jnp.concatenate(parts, axis=1) * _silu(g)


def _mixer_prompt_kernel(x_ref, mod_ref, win_ref, wout_ref, alnw_ref, alnb_ref, ws_ref, bs_ref,
                         lbl_ref, bnw_ref, l1w_ref, l1b_ref, lv_ref,
                         x1_ref, st_ref, vn_ref,
                         proj_scr, mix_scr, s_scr, *, layer, tile):
    step = pl.program_id(1)

    @pl.when(step == 0)
    def _():
        s_scr[...] = jnp.zeros_like(s_scr)

    mod = mod_ref[0, 0]
    x = x_ref[0]
    h = (x * (1.0 + mod[1:2]) + mod[0:1]).astype(BF16)
    proj_scr[...] = _dot(h, win_ref[...])
    lv = lv_ref[...]
    lb = _forget_bound(lbl_ref, layer)

    def chunk(c, carry):
        rows = pl.ds(pl.multiple_of(c * CHUNK, CHUNK), CHUNK)
        a_out, vn, qq, kk, gg = _mixer_chunk_front(proj_scr, rows, alnw_ref, alnb_ref, ws_ref, bs_ref, lb, lv)
        vn_ref[0] = vn
        vv = proj_scr[rows, 4 * HALF:5 * HALF]
        o_in, b = _hgrn_intra(qq, kk, gg, vv, lv, CHUNK)
        qh = (qq * jnp.exp(b)).astype(BF16)
        b_last = b[CHUNK - 1:CHUNK, :]
        kdec = kk * jnp.exp(b_last - b)
        e_last = jnp.exp(b_last)
        o_parts = []
        for hh in range(N_HEADS):
            hs = slice(hh * HEAD, (hh + 1) * HEAD)
            s_old = s_scr[hh]
            o_parts.append(o_in[:, hs] + _dot(qh[:, hs], s_old.astype(BF16)))
            dec = jnp.broadcast_to(e_last[:, hs], (HEAD, HEAD)).T
            s_scr[hh] = dec * s_old + _dot(kdec[:, hs].T.astype(BF16), vv[:, hs].astype(BF16))
        g = proj_scr[rows, 5 * HALF:6 * HALF]
        b_out = _rms_gate(jnp.concatenate(o_parts, axis=1), bnw_ref[...], g)
        mix_scr[rows, 0:HALF] = a_out.astype(BF16)
        mix_scr[rows, HALF:2 * HALF] = b_out.astype(BF16)
        return carry

    lax.fori_loop(0, tile // CHUNK, chunk, 0)

    y = _dot(mix_scr[...], wout_ref[...])
    x1_ref[0] = _layer_norm(ALPHA * x + mod[2:3] * y, l1w_ref[...], l1b_ref[...])

    @pl.when(step == pl.num_programs(1) - 1)
    def _():
        st_ref[0] = s_scr[...]


def _mixer_sample_kernel(x_ref, mod_ref, s0_ref, win_ref, wout_ref, alnw_ref, alnb_ref, ws_ref, bs_ref,
                         lbl_ref, bnw_ref, l1w_ref, l1b_ref, lv_ref,
                         x1_ref, st_ref, vn_ref,
                         proj_scr, qh_scr, kt_scr, vb_scr, el_scr, o_scr, *, layer, seq_len):
    nseq = SAMPLE_SEQS
    x3 = x_ref[...]
    mod = mod_ref[0]
    h3 = x3 * (1.0 + mod[1][:, None, :]) + mod[0][:, None, :]
    proj_scr[...] = _dot(h3.reshape(CHUNK, D_MODEL).astype(BF16), win_ref[...])
    lv = lv_ref[...]
    lb = _forget_bound(lbl_ref, layer)
    rows = slice(0, CHUNK)
    a_out, vn, qq, kk, gg = _mixer_chunk_front(proj_scr, rows, alnw_ref, alnb_ref, ws_ref, bs_ref, lb, lv)
    vn_ref[...] = vn.reshape(nseq, seq_len, HALF)
    vv = proj_scr[rows, 4 * HALF:5 * HALF]
    o_in, b = _hgrn_intra(qq, kk, gg, vv, lv, seq_len)
    qh_scr[...] = qq * jnp.exp(b)
    b3 = b.reshape(nseq, seq_len, HALF)
    b_last = jnp.broadcast_to(b3[:, seq_len - 1:seq_len, :], (nseq, seq_len, HALF))
    el_scr[...] = jnp.exp(b_last)
    kdec = kk * jnp.exp(b_last.reshape(CHUNK, HALF) - b)
    for hh in range(N_HEADS):
        hs = slice(hh * HEAD, (hh + 1) * HEAD)
        kt_scr[hh] = kdec[:, hs].T.astype(BF16)
    vb_scr[...] = vv.astype(BF16)
    row_seq = lax.broadcasted_iota(jnp.int32, (HEAD, CHUNK), 1) // seq_len

    def per_seq(j, carry):
        rws = pl.ds(pl.multiple_of(j * seq_len, seq_len), seq_len)
        own = row_seq == j
        el = el_scr[j]
        for hh in range(N_HEADS):
            hs = slice(hh * HEAD, (hh + 1) * HEAD)
            s_old = s0_ref[0, j, hh]
            o_scr[rws, hs] = _dot(qh_scr[rws, hs].astype(BF16), s_old.astype(BF16))
            dec = jnp.broadcast_to(el[0:1, hs], (HEAD, HEAD)).T
            kt = jnp.where(own, kt_scr[hh], jnp.zeros((), BF16))
            st_ref[j, hh] = dec * s_old + _dot(kt, vb_scr[:, hs])
        return carry

    lax.fori_loop(0, nseq, per_seq, 0)

    g = proj_scr[rows, 5 * HALF:6 * HALF]
    b_out = _rms_gate(o_in + o_scr[...], bnw_ref[...], g)
    mix = jnp.concatenate([a_out, b_out], axis=1).astype(BF16)
    y3 = _dot(mix, wout_ref[...]).reshape(nseq, seq_len, D_MODEL)
    x1_ref[...] = _layer_norm(ALPHA * x3 + mod[2][:, None, :] * y3, l1w_ref[...], l1b_ref[...])


def _mod_rows(mod_ref, per_seq, j):
    if per_seq:
        return mod_ref[0, j][:, None, :]
    return mod_ref[0, 0][j:j + 1][None]


def _swiglu(h, wg_ref, wu_ref, wd_ref):
    acc = None
    for f0, f1 in FF_SPLITS:
        act = (_silu(_dot(h, wg_ref[:, f0:f1])) * _dot(h, wu_ref[:, f0:f1])).astype(BF16)
        part = _dot(act, wd_ref[f0:f1, :])
        acc = part if acc is None else acc + part
    return acc


def _ffn_dense_kernel(x_ref, mod_ref, wg_ref, wu_ref, wd_ref, l2w_ref, l2b_ref, o_ref, *, per_seq):
    x3 = x_ref[...]
    groups, rows, _ = x3.shape
    h = (x3 * (1.0 + _mod_rows(mod_ref, per_seq, 4)) + _mod_rows(mod_ref, per_seq, 3))
    h = h.reshape(groups * rows, D_MODEL).astype(BF16)
    f3 = _swiglu(h, wg_ref, wu_ref, wd_ref).reshape(groups, rows, D_MODEL)
    o_ref[...] = _layer_norm(ALPHA * x3 + _mod_rows(mod_ref, per_seq, 5) * f3, l2w_ref[...], l2b_ref[...])


def _moe_input(x_ref, mod_ref, per_seq):
    x3 = x_ref[...]
    groups, rows, _ = x3.shape
    h3 = x3 * (1.0 + _mod_rows(mod_ref, per_seq, 4)) + _mod_rows(mod_ref, per_seq, 3)
    return h3.reshape(groups * rows, D_MODEL)


def _router_kernel(x_ref, mod_ref, wr_ref, idx_ref, gate_ref, *, per_seq):
    h = _moe_input(x_ref, mod_ref, per_seq)
    n = h.shape[0]
    logits = jnp.dot(h, wr_ref[...], precision=lax.Precision.HIGHEST, preferred_element_type=F32)
    z = jnp.exp(logits - jnp.max(logits, axis=-1, keepdims=True))
    p = z / jnp.sum(z, axis=-1, keepdims=True)
    lane = lax.broadcasted_iota(jnp.int32, p.shape, 1)
    p1 = jnp.max(p, axis=-1, keepdims=True)
    i1 = jnp.min(jnp.where(p == p1, lane, N_EXPERTS), axis=-1, keepdims=True)
    rest = jnp.where(lane == i1, -1.0, p)
    p2 = jnp.max(rest, axis=-1, keepdims=True)
    i2 = jnp.min(jnp.where(rest == p2, lane, N_EXPERTS), axis=-1, keepdims=True)
    two = lax.broadcasted_iota(jnp.int32, (n, 2), 1)
    idx_ref[...] = jnp.where(two == 0, i1, i2)
    gate_ref[...] = jnp.where(two == 0, p1, p2) / (p1 + p2)


def _for_each_row(n, fn):
    def body(i, carry):
        fn(i)
        return carry
    lax.fori_loop(0, n, body, 0)


def _dispatch_kernel(x_ref, mod_ref, da_ref, db_ref, hg_in_ref, hg_ref, rows_scr, sem, *, per_seq):
    del hg_in_ref
    h = _moe_input(x_ref, mod_ref, per_seq)
    n = h.shape[0]
    for s in range(D_MODEL // LANE):
        rows_scr[:, s, :] = h[:, s * LANE:(s + 1) * LANE]

    def copy(i, slot_ref):
        return pltpu.make_async_copy(rows_scr.at[i], hg_ref.at[slot_ref[0, 0, i]], sem)

    _for_each_row(n, lambda i: (copy(i, da_ref).start(), copy(i, db_ref).start()))
    _for_each_row(n, lambda i: (copy(i, da_ref).wait(), copy(i, db_ref).wait()))


def _expert_kernel(te_ref, hg_ref, wg_ref, wu_ref, wd_ref, o_ref, h_scr):
    del te_ref
    for s in range(D_MODEL // LANE):
        h_scr[:, s * LANE:(s + 1) * LANE] = hg_ref[:, s, :].astype(BF16)
    y = _swiglu(h_scr[...], wg_ref.at[0], wu_ref.at[0], wd_ref.at[0])
    for s in range(D_MODEL // LANE):
        o_ref[:, s, :] = y[:, s * LANE:(s + 1) * LANE]


def _combine_kernel(x_ref, mod_ref, gate_ref, da_ref, db_ref, yo_ref, l2w_ref, l2b_ref, o_ref,
                    ya_scr, yb_scr, f_scr, sem, *, per_seq):
    x3 = x_ref[...]
    groups, rows, _ = x3.shape
    n = groups * rows

    def copy(i, slot_ref, dst):
        return pltpu.make_async_copy(yo_ref.at[slot_ref[0, 0, i]], dst.at[i], sem)

    _for_each_row(n, lambda i: (copy(i, da_ref, ya_scr).start(), copy(i, db_ref, yb_scr).start()))
    _for_each_row(n, lambda i: (copy(i, da_ref, ya_scr).wait(), copy(i, db_ref, yb_scr).wait()))
    ga, gb = gate_ref[:, 0:1], gate_ref[:, 1:2]
    for s in range(D_MODEL // LANE):
        f_scr[:, s * LANE:(s + 1) * LANE] = ga * ya_scr[:, s, :] + gb * yb_scr[:, s, :]
    f3 = f_scr[...].reshape(groups, rows, D_MODEL)
    o_ref[...] = _layer_norm(ALPHA * x3 + _mod_rows(mod_ref, per_seq, 5) * f3, l2w_ref[...], l2b_ref[...])


def _params(*semantics):
    return pltpu.CompilerParams(dimension_semantics=semantics, vmem_limit_bytes=VMEM_LIMIT_BYTES)


def _const_spec(shape):
    return pl.BlockSpec(shape, lambda *_: (0,) * len(shape), pipeline_mode=pl.Buffered(1))


def _adaln(c_all, w_ada, b_ada):
    nb = c_all.shape[0]
    tn = 1536

    def body(c_ref, w_ref, b_ref, o_ref):
        c = c_ref[...]
        o_ref[0] = _dot(_silu(c).astype(BF16), w_ref[0].astype(BF16)) + b_ref[0]

    return pl.pallas_call(
        body,
        out_shape=jax.ShapeDtypeStruct((DEPTH, nb, 6 * D_MODEL), F32),
        grid=(DEPTH, 6 * D_MODEL // tn),
        in_specs=[pl.BlockSpec((nb, D_MODEL), lambda l, j: (0, 0)),
                  pl.BlockSpec((1, D_MODEL, tn), lambda l, j: (l, 0, j)),
                  pl.BlockSpec((1, 1, tn), lambda l, j: (l, 0, j))],
        out_specs=pl.BlockSpec((1, nb, tn), lambda l, j: (l, 0, j)),
        compiler_params=_params("arbitrary", "arbitrary"),
        name="adaln_modulation",
    )(c_all, w_ada, b_ada.reshape(DEPTH, 1, 6 * D_MODEL))


def _mixer_weight_specs():
    return [_const_spec((D_MODEL, IN_COLS)), _const_spec((D_MODEL, D_MODEL)),
            _const_spec((1, HALF)), _const_spec((1, HALF)),
            _const_spec((N_HEADS, CHUNK, CHUNK)), _const_spec((CHUNK, N_HEADS)),
            _const_spec((DEPTH, HALF)), _const_spec((1, HEAD)),
            _const_spec((1, D_MODEL)), _const_spec((1, D_MODEL)), _const_spec((CHUNK, CHUNK))]


def _mixer_prompt(layer, x, mod_p, wts, lv):
    batch, seq, _ = x.shape
    tile = PROMPT_TILE
    rows_out = seq - CHUNK * ((seq - 1) // CHUNK)
    assert seq % tile == 0 and rows_out == CHUNK
    return pl.pallas_call(
        functools.partial(_mixer_prompt_kernel, layer=layer, tile=tile),
        out_shape=(jax.ShapeDtypeStruct((batch, seq, D_MODEL), F32),
                   jax.ShapeDtypeStruct((batch, N_HEADS, HEAD, HEAD), F32),
                   jax.ShapeDtypeStruct((batch, CHUNK, HALF), F32)),
        grid=(batch, seq // tile),
        in_specs=[pl.BlockSpec((1, tile, D_MODEL), lambda b, s: (b, s, 0)),
                  pl.BlockSpec((1, 1, 6, D_MODEL), lambda b, s: (layer, b, 0, 0))] + _mixer_weight_specs(),
        out_specs=(pl.BlockSpec((1, tile, D_MODEL), lambda b, s: (b, s, 0)),
                   pl.BlockSpec((1, N_HEADS, HEAD, HEAD), lambda b, s: (b, 0, 0, 0)),
                   pl.BlockSpec((1, CHUNK, HALF), lambda b, s: (b, 0, 0))),
        scratch_shapes=[pltpu.VMEM((tile, IN_COLS), F32), pltpu.VMEM((tile, D_MODEL), BF16),
                        pltpu.VMEM((N_HEADS, HEAD, HEAD), F32)],
        compiler_params=_params("arbitrary", "arbitrary"),
        name="token_mixer_prompt",
    )(x, mod_p, *wts, lv)


def _mixer_sample(layer, x, mod_s, state, wts, lv):
    nseq_all, seq_len, _ = x.shape
    nseq = SAMPLE_SEQS
    assert nseq * seq_len == CHUNK and nseq_all % nseq == 0 and seq_len == SUBLANE
    return pl.pallas_call(
        functools.partial(_mixer_sample_kernel, layer=layer, seq_len=seq_len),
        out_shape=(jax.ShapeDtypeStruct((nseq_all, seq_len, D_MODEL), F32),
                   jax.ShapeDtypeStruct((nseq_all, N_HEADS, HEAD, HEAD), F32),
                   jax.ShapeDtypeStruct((nseq_all, seq_len, HALF), F32)),
        grid=(nseq_all // nseq,),
        in_specs=[pl.BlockSpec((nseq, seq_len, D_MODEL), lambda j: (j, 0, 0)),
                  pl.BlockSpec((1, 6, nseq, D_MODEL), lambda j: (layer, 0, j, 0)),
                  pl.BlockSpec((1, nseq, N_HEADS, HEAD, HEAD), lambda j: (layer, j, 0, 0, 0))]
        + _mixer_weight_specs(),
        out_specs=(pl.BlockSpec((nseq, seq_len, D_MODEL), lambda j: (j, 0, 0)),
                   pl.BlockSpec((nseq, N_HEADS, HEAD, HEAD), lambda j: (j, 0, 0, 0)),
                   pl.BlockSpec((nseq, seq_len, HALF), lambda j: (j, 0, 0))),
        scratch_shapes=[pltpu.VMEM((CHUNK, IN_COLS), F32), pltpu.VMEM((CHUNK, HALF), F32),
                        pltpu.VMEM((N_HEADS, HEAD, CHUNK), BF16), pltpu.VMEM((CHUNK, HALF), BF16),
                        pltpu.VMEM((nseq, seq_len, HALF), F32), pltpu.VMEM((CHUNK, HALF), F32)],
        compiler_params=_params("arbitrary"),
        name="token_mixer_sample",
    )(x, mod_s, state, *wts, lv)


def _row_blocking(x, per_seq, tile):
    batch, seq, _ = x.shape
    if per_seq:
        groups = tile // seq
        assert batch % groups == 0
        grid = (batch // groups,)
        x_spec = pl.BlockSpec((groups, seq, D_MODEL), lambda i: (i, 0, 0))
        return grid, x_spec, groups, lambda layer: pl.BlockSpec((1, 6, groups, D_MODEL), lambda i: (layer, 0, i, 0))
    assert seq % tile == 0
    per = seq // tile
    grid = (batch * per,)
    x_spec = pl.BlockSpec((1, tile, D_MODEL), lambda i: (i // per, i % per, 0))
    return grid, x_spec, per, lambda layer: pl.BlockSpec((1, 1, 6, D_MODEL), lambda i: (layer, i // per, 0, 0))


def _ffn_dense(layer, x, mod, per_seq, wg, wu, wd, l2w, l2b):
    tile = FFN_TILE if not per_seq else CHUNK
    grid, x_spec, _, mod_spec = _row_blocking(x, per_seq, tile)
    return pl.pallas_call(
        functools.partial(_ffn_dense_kernel, per_seq=per_seq),
        out_shape=jax.ShapeDtypeStruct(x.shape, F32),
        grid=grid,
        in_specs=[x_spec, mod_spec(layer),
                  _const_spec((D_MODEL, D_FF)), _const_spec((D_MODEL, D_FF)), _const_spec((D_FF, D_MODEL)),
                  _const_spec((1, D_MODEL)), _const_spec((1, D_MODEL))],
        out_specs=x_spec,
        compiler_params=_params("arbitrary"),
        name="ffn_dense",
    )(x, mod, wg, wu, wd, l2w, l2b)


def _router(layer, x, mod, per_seq, w_router):
    batch, seq, _ = x.shape
    tile = FFN_TILE if not per_seq else CHUNK
    grid, x_spec, _, mod_spec = _row_blocking(x, per_seq, tile)
    n = batch * seq
    return pl.pallas_call(
        functools.partial(_router_kernel, per_seq=per_seq),
        out_shape=(jax.ShapeDtypeStruct((n, 2), jnp.int32), jax.ShapeDtypeStruct((n, 2), F32)),
        grid=grid,
        in_specs=[x_spec, mod_spec(layer), _const_spec((D_MODEL, N_EXPERTS))],
        out_specs=(pl.BlockSpec((tile, 2), lambda i: (i, 0)), pl.BlockSpec((tile, 2), lambda i: (i, 0))),
        compiler_params=_params("arbitrary"),
        name="moe_router",
    )(x, mod, w_router)


def _slot_spec(tile):
    return pl.BlockSpec((1, 1, tile), lambda i: (i, 0, 0), memory_space=pltpu.SMEM)


def _dispatch(layer, x, mod, per_seq, slot_a, slot_b, hg):
    tile = FFN_TILE if not per_seq else CHUNK
    grid, x_spec, _, mod_spec = _row_blocking(x, per_seq, tile)
    return pl.pallas_call(
        functools.partial(_dispatch_kernel, per_seq=per_seq),
        out_shape=jax.ShapeDtypeStruct(hg.shape, hg.dtype),
        grid=grid,
        in_specs=[x_spec, mod_spec(layer), _slot_spec(tile), _slot_spec(tile), pl.BlockSpec(memory_space=pl.ANY)],
        out_specs=pl.BlockSpec(memory_space=pl.ANY),
        scratch_shapes=[pltpu.VMEM((tile, SUBLANE, LANE), F32), pltpu.SemaphoreType.DMA(())],
        input_output_aliases={4: 0},
        compiler_params=_params("arbitrary"),
        name="moe_dispatch",
    )(x, mod, slot_a.reshape(-1, 1, tile), slot_b.reshape(-1, 1, tile), hg)


def _experts(hg, tile_expert, wg, wu, wd):
    mp = hg.shape[0]
    tile = EXPERT_TILE
    return pl.pallas_call(
        _expert_kernel,
        out_shape=jax.ShapeDtypeStruct(hg.shape, F32),
        grid_spec=pltpu.PrefetchScalarGridSpec(
            num_scalar_prefetch=1,
            grid=(mp // tile,),
            in_specs=[pl.BlockSpec((tile, SUBLANE, LANE), lambda i, te: (i, 0, 0)),
                      pl.BlockSpec((1, D_MODEL, D_FF), lambda i, te: (te[i], 0, 0)),
                      pl.BlockSpec((1, D_MODEL, D_FF), lambda i, te: (te[i], 0, 0)),
                      pl.BlockSpec((1, D_FF, D_MODEL), lambda i, te: (te[i], 0, 0))],
            out_specs=pl.BlockSpec((tile, SUBLANE, LANE), lambda i, te: (i, 0, 0)),
            scratch_shapes=[pltpu.VMEM((tile, D_MODEL), BF16)]),
        compiler_params=_params("arbitrary"),
        name="moe_experts",
    )(tile_expert, hg, wg, wu, wd)


def _combine(layer, x, mod, per_seq, gate, slot_a, slot_b, yo, l2w, l2b):
    tile = FFN_TILE if not per_seq else CHUNK
    grid, x_spec, _, mod_spec = _row_blocking(x, per_seq, tile)
    return pl.pallas_call(
        functools.partial(_combine_kernel, per_seq=per_seq),
        out_shape=jax.ShapeDtypeStruct(x.shape, F32),
        grid=grid,
        in_specs=[x_spec, mod_spec(layer), pl.BlockSpec((tile, 2), lambda i: (i, 0)),
                  _slot_spec(tile), _slot_spec(tile), pl.BlockSpec(memory_space=pl.ANY),
                  _const_spec((1, D_MODEL)), _const_spec((1, D_MODEL))],
        out_specs=x_spec,
        scratch_shapes=[pltpu.VMEM((tile, SUBLANE, LANE), F32), pltpu.VMEM((tile, SUBLANE, LANE), F32),
                        pltpu.VMEM((tile, D_MODEL), F32), pltpu.SemaphoreType.DMA(())],
        compiler_params=_params("arbitrary"),
        name="moe_combine",
    )(x, mod, gate, slot_a.reshape(-1, 1, tile), slot_b.reshape(-1, 1, tile), yo, l2w, l2b)


def _routing_tables(idx, n_pad_rows):
    e_flat = jnp.concatenate([idx[:, 0], idx[:, 1]])
    onehot = (e_flat[:, None] == jnp.arange(N_EXPERTS, dtype=jnp.int32)[None, :]).astype(jnp.int32)
    csum = jnp.cumsum(onehot, axis=0)
    padded = ((csum[-1] + EXPERT_TILE - 1) // EXPERT_TILE) * EXPERT_TILE
    pend = jnp.cumsum(padded)
    slot = jnp.sum(onehot * (csum - 1 + (pend - padded)[None, :]), axis=1).astype(jnp.int32)
    tile_start = jnp.arange(n_pad_rows // EXPERT_TILE, dtype=jnp.int32) * EXPERT_TILE
    tile_expert = jnp.sum((tile_start[:, None] >= pend[None, :]).astype(jnp.int32), axis=1)
    return slot, jnp.minimum(tile_expert, N_EXPERTS - 1).astype(jnp.int32)


def _round_up(a, b):
    return (a + b - 1) // b * b


def kernel(x_prompt, x_sample, state_hgrn, c_prompt, c_sample, w_ada, b_ada, w_in, w_out, a_ln_w, a_ln_b, a_ws, a_bs, lb_logits, b_norm_w, ln1_w, ln1_b, ln2_w, ln2_b, w_ff_gate, w_ff_up, w_ff_down, w_router, e_gate, e_up, e_down):
    batch, seq, _ = x_prompt.shape
    nseq, seq_len, _ = x_sample.shape
    n_prompt, n_sample = batch * seq, nseq * seq_len
    n_tok = n_prompt + n_sample

    mod = _adaln(jnp.concatenate([c_prompt, c_sample], axis=0), w_ada, b_ada)
    mod_p = mod[:, :batch].reshape(DEPTH, batch, 6, D_MODEL)
    mod_s = mod[:, batch:].reshape(DEPTH, nseq, 6, D_MODEL).transpose(0, 2, 1, 3)

    lv_p = jnp.asarray(_level_ids(CHUNK))
    lv_s = jnp.asarray(_level_ids(seq_len))
    reps = CHUNK // seq_len
    n_pad_rows = _round_up(2 * n_tok, EXPERT_TILE) + N_EXPERTS * EXPERT_TILE

    xp, xs = x_prompt, x_sample
    st_p, st_s, cv_p, cv_s = [], [], [], []
    for l in range(DEPTH):
        shared = (a_ln_w[l][None], a_ln_b[l][None])
        tail = (lb_logits, b_norm_w[l][None], ln1_w[l][None], ln1_b[l][None])
        w_in_l, w_out_l = w_in[l].astype(BF16), w_out[l].astype(BF16)
        wts_p = (w_in_l, w_out_l) + shared + (a_ws[l], a_bs[l].T) + tail
        ws_s = jnp.tile(a_ws[l][:, :seq_len, :seq_len], (1, reps, reps))
        bs_s = jnp.tile(a_bs[l][:, :seq_len].T, (reps, 1))
        wts_s = (w_in_l, w_out_l) + shared + (ws_s, bs_s) + tail
        xp, sp, vp = _mixer_prompt(l, xp, mod_p, wts_p, lv_p)
        xs, ss, vs = _mixer_sample(l, xs, mod_s, state_hgrn, wts_s, lv_s)
        st_p.append(sp), st_s.append(ss), cv_p.append(vp), cv_s.append(vs)
        l2w, l2b = ln2_w[l][None], ln2_b[l][None]
        if l % 2 == 0:
            wg, wu, wd = (w[l // 2].astype(BF16) for w in (w_ff_gate, w_ff_up, w_ff_down))
            xp = _ffn_dense(l, xp, mod_p, False, wg, wu, wd, l2w, l2b)
            xs = _ffn_dense(l, xs, mod_s, True, wg, wu, wd, l2w, l2b)
        else:
            wr = w_router[l // 2]
            ip, gp = _router(l, xp, mod_p, False, wr)
            is_, gs = _router(l, xs, mod_s, True, wr)
            slot, tile_expert = _routing_tables(jnp.concatenate([ip, is_], axis=0), n_pad_rows)
            sa_p, sa_s = slot[:n_prompt], slot[n_prompt:n_tok]
            sb_p, sb_s = slot[n_tok:n_tok + n_prompt], slot[n_tok + n_prompt:]
            hg = jnp.zeros((n_pad_rows, SUBLANE, LANE), F32)
            hg = _dispatch(l, xp, mod_p, False, sa_p, sb_p, hg)
            hg = _dispatch(l, xs, mod_s, True, sa_s, sb_s, hg)
            wg, wu, wd = (w[l // 2].astype(BF16) for w in (e_gate, e_up, e_down))
            yo = _experts(hg, tile_expert, wg, wu, wd)
            xp = _combine(l, xp, mod_p, False, gp, sa_p, sb_p, yo, l2w, l2b)
            xs = _combine(l, xs, mod_s, True, gs, sa_s, sb_s, yo, l2w, l2b)
    return (xp, xs, jnp.stack(st_p), jnp.stack(st_s), jnp.stack(cv_p), jnp.stack(cv_s))
```

```python
import functools
import math

import numpy as np
import jax
import jax.numpy as jnp
from jax import lax
from jax.experimental import pallas as pl
from jax.experimental.pallas import tpu as pltpu

F32 = jnp.float32
BF16 = jnp.bfloat16

D_MODEL = 1024
DEPTH = 4
HALF = 512
N_HEADS = 4
HEAD = 128
CHUNK = 128
IN_COLS = 6 * HALF
D_FF = 2816
N_EXPERTS = 8
ALPHA = (2.0 * DEPTH) ** 0.25
LN_EPS = 1e-5
RMS_EPS = 1e-6
LOG2_E = math.log2(math.e)

VMEM_LIMIT_BYTES = 56 * 1024 * 1024
LANE = 128
SUBLANE = 8
ROW_TILE = D_MODEL // LANE
assert ROW_TILE == SUBLANE

PROMPT_TILE = 512
SAMPLE_SEQS = 16
FFN_TILE = 512
EXPERT_TILE = 256
FF_SPLITS = ((0, 1024), (1024, 2048), (2048, D_FF))


def _dot(a, b):
    return jnp.dot(a, b, preferred_element_type=F32)


def _dot_nt(a, b):
    return lax.dot_general(a, b, (((1,), (1,)), ((), ())), preferred_element_type=F32)


def _gelu(x):
    return 0.5 * x * (1.0 + lax.erf(x * (1.0 / math.sqrt(2.0))))


def _silu(x):
    return x * jax.nn.sigmoid(x)


def _layer_norm(z, w, b):
    mu = jnp.mean(z, axis=-1, keepdims=True)
    zc = z - mu
    var = jnp.mean(zc * zc, axis=-1, keepdims=True)
    return zc * lax.rsqrt(var + LN_EPS) * w + b


def _level_ids(block):
    t = np.arange(CHUNK)[:, None]
    s = np.arange(CHUNK)[None, :]
    x = t ^ s
    lv = np.where(x == 0, 0, np.floor(np.log2(np.maximum(x, 1))).astype(np.int64) + 1)
    ok = (s <= t) & (x < block)
    return np.where(ok, lv, -1).astype(np.int32)


def _reference_rows(b, m):
    rows, width = b.shape
    two_m = 2 * m
    if two_m >= SUBLANE:
        nb = rows // two_m
        b3 = b.reshape(nb, two_m, width)
        r = jnp.broadcast_to(b3[:, m - 1:m, :], (nb, two_m, width))
        return r.reshape(rows, width)
    t = lax.broadcasted_iota(jnp.int32, (rows, width), 0)
    tm = t & (two_m - 1)
    down1 = pltpu.roll(b, 1, 0)
    if m == 1:
        return jnp.where(tm == 0, b, down1)
    up1 = pltpu.roll(b, rows - 1, 0)
    down2 = pltpu.roll(b, 2, 0)
    return jnp.where(tm == 0, up1, jnp.where(tm == 1, b, jnp.where(tm == 2, down1, down2)))


def _hgrn_intra(qq, kk, gg, vv, lv, block):
    cm = (lv >= 0).astype(BF16)
    g_hi = gg.astype(BF16)
    rem = gg - g_hi.astype(F32)
    g_mid = rem.astype(BF16)
    g_lo = (rem - g_mid.astype(F32)).astype(BF16)
    b = _dot(cm, g_hi) + _dot(cm, g_mid) + _dot(cm, g_lo)

    heads = [slice(h * HEAD, (h + 1) * HEAD) for h in range(N_HEADS)]
    qb = qq.astype(BF16)
    kb = kk.astype(BF16)
    scores = [jnp.where(lv == 0, _dot_nt(qb[:, hs], kb[:, hs]), 0.0) for hs in heads]
    m = block // 2
    while m >= 1:
        level = int(math.log2(m)) + 1
        e = jnp.exp2(-jnp.abs(b - _reference_rows(b, m)))
        qe = (qq * e).astype(BF16)
        ke = (kk * e).astype(BF16)
        for h, hs in enumerate(heads):
            scores[h] = jnp.where(lv == level, _dot_nt(qe[:, hs], ke[:, hs]), scores[h])
        m //= 2
    vb = vv.astype(BF16)
    o = jnp.concatenate([_dot(scores[h].astype(BF16), vb[:, hs]) for h, hs in enumerate(heads)], axis=1)
    return o, b


def _forget_bound(lbl_ref, layer):
    z = lbl_ref[...]
    z = z - jnp.max(z, axis=0, keepdims=True)
    ez = jnp.exp(z)
    p = ez / jnp.sum(ez, axis=0, keepdims=True)
    c = p[0:1]
    for r in range(1, layer + 1):
        c = c + p[r:r + 1]
    return c - p[0:1]


def _mixer_chunk_front(proj_scr, rows, alnw_ref, alnb_ref, ws_ref, bs_ref, lb, lv):
    u = proj_scr[rows, 0 * HALF:1 * HALF]
    v = proj_scr[rows, 1 * HALF:2 * HALF]
    q = proj_scr[rows, 2 * HALF:3 * HALF]
    f = proj_scr[rows, 3 * HALF:4 * HALF]
    ug = _gelu(u)
    vn = _layer_norm(_gelu(v), alnw_ref[...], alnb_ref[...])
    vnb = vn.astype(BF16)
    a_parts = []
    for h in range(N_HEADS):
        hs = slice(h * HEAD, (h + 1) * HEAD)
        w = jnp.where(lv >= 0, ws_ref[h], 0.0).astype(BF16)
        mixed = _dot(w, vnb[:, hs]) + bs_ref[:, h:h + 1]
        a_parts.append(ug[:, hs] * mixed)
    a_out = jnp.concatenate(a_parts, axis=1)
    fg = lb + (1.0 - lb) * jax.nn.sigmoid(f)
    return a_out, vn, _silu(q), 1.0 - fg, jnp.log(fg) * LOG2_E


def _rms_gate(o, bnw, g):
    parts = []
    for h in range(N_HEADS):
        hs = slice(h * HEAD, (h + 1) * HEAD)
        oh = o[:, hs]
        parts.append(oh * lax.rsqrt(jnp.mean(oh * oh, axis=-1, keepdims=True) + RMS_EPS) * bnw)
    return jnp.concatenate(parts, axis=1) * _silu(g)


def _mixer_prompt_kernel(x_ref, mod_ref, xn_ref, modn_ref, win_ref, wout_ref, alnw_ref, alnb_ref, ws_ref, bs_ref,
                         lbl_ref, bnw_ref, l1w_ref, l1b_ref, lv_ref,
                         x1_ref, st_ref, vn_ref,
                         proj_a, proj_b, hn_scr, mix_scr, s_scr, *, layer, tile):
    step = pl.program_id(1)
    lin = pl.program_id(0) * pl.num_programs(1) + step
    n_chunks = tile // CHUNK
    col_splits = [(IN_COLS * c // n_chunks, IN_COLS * (c + 1) // n_chunks) for c in range(n_chunks)]

    @pl.when(step == 0)
    def _():
        s_scr[...] = jnp.zeros_like(s_scr)

    mod = mod_ref[0, 0]
    x = x_ref[0]

    @pl.when(lin == 0)
    def _():
        proj_a[...] = _dot((x * (1.0 + mod[1:2]) + mod[0:1]).astype(BF16), win_ref[...])

    modn = modn_ref[0, 0]
    hn_scr[...] = (xn_ref[0] * (1.0 + modn[1:2]) + modn[0:1]).astype(BF16)
    lv = lv_ref[...]
    lb = _forget_bound(lbl_ref, layer)

    def chunk(c, proj_scr):
        rows = slice(c * CHUNK, (c + 1) * CHUNK)
        a_out, vn, qq, kk, gg = _mixer_chunk_front(proj_scr, rows, alnw_ref, alnb_ref, ws_ref, bs_ref, lb, lv)
        vn_ref[0] = vn
        vv = proj_scr[rows, 4 * HALF:5 * HALF]
        o_in, b = _hgrn_intra(qq, kk, gg, vv, lv, CHUNK)
        qh = (qq * jnp.exp2(b)).astype(BF16)
        b_last = b[CHUNK - 1:CHUNK, :]
        kdec = kk * jnp.exp2(b_last - b)
        e_last = jnp.exp2(b_last)
        o_parts = []
        for hh in range(N_HEADS):
            hs = slice(hh * HEAD, (hh + 1) * HEAD)
            s_old = s_scr[hh]
            o_parts.append(o_in[:, hs] + _dot(qh[:, hs], s_old.astype(BF16)))
            dec = jnp.broadcast_to(e_last[:, hs], (HEAD, HEAD)).T
            s_scr[hh] = dec * s_old + _dot(kdec[:, hs].T.astype(BF16), vv[:, hs].astype(BF16))
        g = proj_scr[rows, 5 * HALF:6 * HALF]
        b_out = _rms_gate(jnp.concatenate(o_parts, axis=1), bnw_ref[...], g)
        mix_scr[rows, 0:HALF] = a_out.astype(BF16)
        mix_scr[rows, HALF:2 * HALF] = b_out.astype(BF16)

    def run(proj_cur, proj_nxt):
        for c, (c0, c1) in enumerate(col_splits):
            proj_nxt[:, c0:c1] = _dot(hn_scr[...], win_ref[:, c0:c1])
            chunk(c, proj_cur)

    @pl.when(lin % 2 == 0)
    def _():
        run(proj_a, proj_b)

    @pl.when(lin % 2 == 1)
    def _():
        run(proj_b, proj_a)

    y = _dot(mix_scr[...], wout_ref[...])
    x1_ref[0] = _layer_norm(ALPHA * x + mod[2:3] * y, l1w_ref[...], l1b_ref[...])

    @pl.when(step == pl.num_programs(1) - 1)
    def _():
        st_ref[0] = s_scr[...]


def _mixer_sample_kernel(x_ref, mod_ref, s0_ref, win_ref, wout_ref, alnw_ref, alnb_ref, ws_ref, bs_ref,
                         lbl_ref, bnw_ref, l1w_ref, l1b_ref, lv_ref,
                         x1_ref, st_ref, vn_ref,
                         proj_scr, qh_scr, kt_scr, vb_scr, el_scr, o_scr, *, layer, seq_len):
    nseq = SAMPLE_SEQS
    x3 = x_ref[...]
    mod = mod_ref[0]
    h3 = x3 * (1.0 + mod[1][:, None, :]) + mod[0][:, None, :]
    proj_scr[...] = _dot(h3.reshape(CHUNK, D_MODEL).astype(BF16), win_ref[...])
    lv = lv_ref[...]
    lb = _forget_bound(lbl_ref, layer)
    rows = slice(0, CHUNK)
    a_out, vn, qq, kk, gg = _mixer_chunk_front(proj_scr, rows, alnw_ref, alnb_ref, ws_ref, bs_ref, lb, lv)
    vn_ref[...] = vn.reshape(nseq, seq_len, HALF)
    vv = proj_scr[rows, 4 * HALF:5 * HALF]
    o_in, b = _hgrn_intra(qq, kk, gg, vv, lv, seq_len)
    qh_scr[...] = qq * jnp.exp2(b)
    b3 = b.reshape(nseq, seq_len, HALF)
    b_last = jnp.broadcast_to(b3[:, seq_len - 1:seq_len, :], (nseq, seq_len, HALF))
    el_scr[...] = jnp.exp2(b_last)
    kdec = kk * jnp.exp2(b_last.reshape(CHUNK, HALF) - b)
    for hh in range(N_HEADS):
        hs = slice(hh * HEAD, (hh + 1) * HEAD)
        kt_scr[hh] = kdec[:, hs].T.astype(BF16)
    vb_scr[...] = vv.astype(BF16)
    row_seq = lax.broadcasted_iota(jnp.int32, (HEAD, CHUNK), 1) // seq_len

    def per_seq(j, carry):
        rws = pl.ds(pl.multiple_of(j * seq_len, seq_len), seq_len)
        own = row_seq == j
        el = el_scr[j]
        for hh in range(N_HEADS):
            hs = slice(hh * HEAD, (hh + 1) * HEAD)
            s_old = s0_ref[0, j, hh]
            o_scr[rws, hs] = _dot(qh_scr[rws, hs].astype(BF16), s_old.astype(BF16))
            dec = jnp.broadcast_to(el[0:1, hs], (HEAD, HEAD)).T
            kt = jnp.where(own, kt_scr[hh], jnp.zeros((), BF16))
            st_ref[j, hh] = dec * s_old + _dot(kt, vb_scr[:, hs])
        return carry

    lax.fori_loop(0, nseq, per_seq, 0)

    g = proj_scr[rows, 5 * HALF:6 * HALF]
    b_out = _rms_gate(o_in + o_scr[...], bnw_ref[...], g)
    mix = jnp.concatenate([a_out, b_out], axis=1).astype(BF16)
    y3 = _dot(mix, wout_ref[...]).reshape(nseq, seq_len, D_MODEL)
    x1_ref[...] = _layer_norm(ALPHA * x3 + mod[2][:, None, :] * y3, l1w_ref[...], l1b_ref[...])


def _mod_rows(mod_ref, per_seq, j):
    if per_seq:
        return mod_ref[0, j][:, None, :]
    return mod_ref[0, 0][j:j + 1][None]


def _swiglu(h, wg_ref, wu_ref, wd_ref):
    acc = None
    for f0, f1 in FF_SPLITS:
        act = (_silu(_dot(h, wg_ref[:, f0:f1])) * _dot(h, wu_ref[:, f0:f1])).astype(BF16)
        part = _dot(act, wd_ref[f0:f1, :])
        acc = part if acc is None else acc + part
    return acc


def _ffn_dense_kernel(x_ref, mod_ref, wg_ref, wu_ref, wd_ref, l2w_ref, l2b_ref, o_ref, *, per_seq):
    x3 = x_ref[...]
    groups, rows, _ = x3.shape
    h = (x3 * (1.0 + _mod_rows(mod_ref, per_seq, 4)) + _mod_rows(mod_ref, per_seq, 3))
    h = h.reshape(groups * rows, D_MODEL).astype(BF16)
    f3 = _swiglu(h, wg_ref, wu_ref, wd_ref).reshape(groups, rows, D_MODEL)
    o_ref[...] = _layer_norm(ALPHA * x3 + _mod_rows(mod_ref, per_seq, 5) * f3, l2w_ref[...], l2b_ref[...])


def _moe_input(x_ref, mod_ref, per_seq):
    x3 = x_ref[...]
    groups, rows, _ = x3.shape
    h3 = x3 * (1.0 + _mod_rows(mod_ref, per_seq, 4)) + _mod_rows(mod_ref, per_seq, 3)
    return h3.reshape(groups * rows, D_MODEL)


def _router_kernel(x_ref, mod_ref, wr_ref, idx_ref, gate_ref, *, per_seq):
    h = _moe_input(x_ref, mod_ref, per_seq)
    n = h.shape[0]
    logits = jnp.dot(h, wr_ref[...], precision=lax.Precision.HIGHEST, preferred_element_type=F32)
    z = jnp.exp(logits - jnp.max(logits, axis=-1, keepdims=True))
    p = z / jnp.sum(z, axis=-1, keepdims=True)
    lane = lax.broadcasted_iota(jnp.int32, p.shape, 1)
    p1 = jnp.max(p, axis=-1, keepdims=True)
    i1 = jnp.min(jnp.where(p == p1, lane, N_EXPERTS), axis=-1, keepdims=True)
    rest = jnp.where(lane == i1, -1.0, p)
    p2 = jnp.max(rest, axis=-1, keepdims=True)
    i2 = jnp.min(jnp.where(rest == p2, lane, N_EXPERTS), axis=-1, keepdims=True)
    two = lax.broadcasted_iota(jnp.int32, (n, 2), 1)
    idx_ref[...] = jnp.where(two == 0, i1, i2)
    gate_ref[...] = jnp.where(two == 0, p1, p2) / (p1 + p2)


def _for_each_row(n, fn):
    def body(i, carry):
        fn(i)
        return carry
    lax.fori_loop(0, n, body, 0)


def _lane_block(n, s):
    return pl.ds(s, n, stride=ROW_TILE)


def _tile_of_row(r):
    return pl.ds(pl.multiple_of(r * ROW_TILE, ROW_TILE), ROW_TILE)


def _dispatch_kernel(x_ref, mod_ref, da_ref, db_ref, hg_in_ref, hg_ref, rows_scr, sem, *, per_seq):
    del hg_in_ref
    h = _moe_input(x_ref, mod_ref, per_seq)
    n = h.shape[0]
    for s in range(ROW_TILE):
        rows_scr[_lane_block(n, s), :] = h[:, s * LANE:(s + 1) * LANE]

    def copy(i, slot_ref):
        return pltpu.make_async_copy(rows_scr.at[_tile_of_row(i)], hg_ref.at[_tile_of_row(slot_ref[0, 0, i])], sem)

    _for_each_row(n, lambda i: (copy(i, da_ref).start(), copy(i, db_ref).start()))
    _for_each_row(n, lambda i: (copy(i, da_ref).wait(), copy(i, db_ref).wait()))


def _expert_kernel(te_ref, hg_ref, wg_ref, wu_ref, wd_ref, o_ref, raw_scr, h_scr, sem):
    del te_ref
    i = pl.program_id(0)
    n = pl.num_programs(0)
    tile = h_scr.shape[1]
    cur, nxt = i % 2, (i + 1) % 2

    def fetch(t, slot):
        rows = pl.ds(pl.multiple_of(t * tile * ROW_TILE, tile * ROW_TILE), tile * ROW_TILE)
        return pltpu.make_async_copy(hg_ref.at[rows], raw_scr.at[slot], sem.at[slot])

    def rearrange(slot):
        for s in range(ROW_TILE):
            h_scr[slot, :, s * LANE:(s + 1) * LANE] = raw_scr[slot, _lane_block(tile, s), :].astype(BF16)

    @pl.when(i == 0)
    def _():
        fetch(0, 0).start()
        fetch(0, 0).wait()
        rearrange(0)
        fetch(1, 1).start()

    @pl.when(i + 1 < n)
    def _():
        fetch(i + 1, nxt).wait()

    @pl.when(i + 2 < n)
    def _():
        fetch(i + 2, cur).start()

    rearrange(nxt)
    y = _swiglu(h_scr[cur], wg_ref.at[0], wu_ref.at[0], wd_ref.at[0])
    for s in range(ROW_TILE):
        o_ref[_lane_block(tile, s), :] = y[:, s * LANE:(s + 1) * LANE]


def _combine_kernel(x_ref, mod_ref, gate_ref, da_ref, db_ref, yo_ref, l2w_ref, l2b_ref, o_ref,
                    ya_scr, yb_scr, f_scr, sem, *, per_seq):
    x3 = x_ref[...]
    groups, rows, _ = x3.shape
    n = groups * rows

    def copy(i, slot_ref, dst):
        return pltpu.make_async_copy(yo_ref.at[_tile_of_row(slot_ref[0, 0, i])], dst.at[_tile_of_row(i)], sem)

    _for_each_row(n, lambda i: (copy(i, da_ref, ya_scr).start(), copy(i, db_ref, yb_scr).start()))
    _for_each_row(n, lambda i: (copy(i, da_ref, ya_scr).wait(), copy(i, db_ref, yb_scr).wait()))
    ga, gb = gate_ref[:, 0:1], gate_ref[:, 1:2]
    for s in range(ROW_TILE):
        f_scr[:, s * LANE:(s + 1) * LANE] = ga * ya_scr[_lane_block(n, s), :] + gb * yb_scr[_lane_block(n, s), :]
    f3 = f_scr[...].reshape(groups, rows, D_MODEL)
    o_ref[...] = _layer_norm(ALPHA * x3 + _mod_rows(mod_ref, per_seq, 5) * f3, l2w_ref[...], l2b_ref[...])


def _params(*semantics):
    return pltpu.CompilerParams(dimension_semantics=semantics, vmem_limit_bytes=VMEM_LIMIT_BYTES)


def _const_spec(shape):
    return pl.BlockSpec(shape, lambda *_: (0,) * len(shape), pipeline_mode=pl.Buffered(1))


def _adaln(c_all, w_ada, b_ada):
    nb = c_all.shape[0]
    tn = 1536

    def body(c_ref, w_ref, b_ref, o_ref):
        c = c_ref[...]
        o_ref[0] = _dot(_silu(c).astype(BF16), w_ref[0].astype(BF16)) + b_ref[0]

    return pl.pallas_call(
        body,
        out_shape=jax.ShapeDtypeStruct((DEPTH, nb, 6 * D_MODEL), F32),
        grid=(DEPTH, 6 * D_MODEL // tn),
        in_specs=[pl.BlockSpec((nb, D_MODEL), lambda l, j: (0, 0)),
                  pl.BlockSpec((1, D_MODEL, tn), lambda l, j: (l, 0, j)),
                  pl.BlockSpec((1, 1, tn), lambda l, j: (l, 0, j))],
        out_specs=pl.BlockSpec((1, nb, tn), lambda l, j: (l, 0, j)),
        compiler_params=_params("arbitrary", "arbitrary"),
        name="adaln_modulation",
    )(c_all, w_ada, b_ada.reshape(DEPTH, 1, 6 * D_MODEL))


def _mixer_weight_specs():
    return [_const_spec((D_MODEL, IN_COLS)), _const_spec((D_MODEL, D_MODEL)),
            _const_spec((1, HALF)), _const_spec((1, HALF)),
            _const_spec((N_HEADS, CHUNK, CHUNK)), _const_spec((CHUNK, N_HEADS)),
            _const_spec((DEPTH, HALF)), _const_spec((1, HEAD)),
            _const_spec((1, D_MODEL)), _const_spec((1, D_MODEL)), _const_spec((CHUNK, CHUNK))]


def _mixer_prompt(layer, x, mod_p, wts, lv):
    batch, seq, _ = x.shape
    tile = PROMPT_TILE
    rows_out = seq - CHUNK * ((seq - 1) // CHUNK)
    assert seq % tile == 0 and rows_out == CHUNK
    per = seq // tile

    def nxt(b, s):
        return jnp.minimum(b * per + s + 1, batch * per - 1)

    return pl.pallas_call(
        functools.partial(_mixer_prompt_kernel, layer=layer, tile=tile),
        out_shape=(jax.ShapeDtypeStruct((batch, seq, D_MODEL), F32),
                   jax.ShapeDtypeStruct((batch, N_HEADS, HEAD, HEAD), F32),
                   jax.ShapeDtypeStruct((batch, CHUNK, HALF), F32)),
        grid=(batch, per),
        in_specs=[pl.BlockSpec((1, tile, D_MODEL), lambda b, s: (b, s, 0)),
                  pl.BlockSpec((1, 1, 6, D_MODEL), lambda b, s: (layer, b, 0, 0)),
                  pl.BlockSpec((1, tile, D_MODEL), lambda b, s: (nxt(b, s) // per, nxt(b, s) % per, 0)),
                  pl.BlockSpec((1, 1, 6, D_MODEL), lambda b, s: (layer, nxt(b, s) // per, 0, 0))]
        + _mixer_weight_specs(),
        out_specs=(pl.BlockSpec((1, tile, D_MODEL), lambda b, s: (b, s, 0)),
                   pl.BlockSpec((1, N_HEADS, HEAD, HEAD), lambda b, s: (b, 0, 0, 0)),
                   pl.BlockSpec((1, CHUNK, HALF), lambda b, s: (b, 0, 0))),
        scratch_shapes=[pltpu.VMEM((tile, IN_COLS), F32), pltpu.VMEM((tile, IN_COLS), F32),
                        pltpu.VMEM((tile, D_MODEL), BF16), pltpu.VMEM((tile, D_MODEL), BF16),
                        pltpu.VMEM((N_HEADS, HEAD, HEAD), F32)],
        compiler_params=_params("arbitrary", "arbitrary"),
        name="token_mixer_prompt",
    )(x, mod_p, x, mod_p, *wts, lv)


def _mixer_sample(layer, x, mod_s, state, wts, lv):
    nseq_all, seq_len, _ = x.shape
    nseq = SAMPLE_SEQS
    assert nseq * seq_len == CHUNK and nseq_all % nseq == 0 and seq_len == SUBLANE
    return pl.pallas_call(
        functools.partial(_mixer_sample_kernel, layer=layer, seq_len=seq_len),
        out_shape=(jax.ShapeDtypeStruct((nseq_all, seq_len, D_MODEL), F32),
                   jax.ShapeDtypeStruct((nseq_all, N_HEADS, HEAD, HEAD), F32),
                   jax.ShapeDtypeStruct((nseq_all, seq_len, HALF), F32)),
        grid=(nseq_all // nseq,),
        in_specs=[pl.BlockSpec((nseq, seq_len, D_MODEL), lambda j: (j, 0, 0)),
                  pl.BlockSpec((1, 6, nseq, D_MODEL), lambda j: (layer, 0, j, 0)),
                  pl.BlockSpec((1, nseq, N_HEADS, HEAD, HEAD), lambda j: (layer, j, 0, 0, 0))]
        + _mixer_weight_specs(),
        out_specs=(pl.BlockSpec((nseq, seq_len, D_MODEL), lambda j: (j, 0, 0)),
                   pl.BlockSpec((nseq, N_HEADS, HEAD, HEAD), lambda j: (j, 0, 0, 0)),
                   pl.BlockSpec((nseq, seq_len, HALF), lambda j: (j, 0, 0))),
        scratch_shapes=[pltpu.VMEM((CHUNK, IN_COLS), F32), pltpu.VMEM((CHUNK, HALF), F32),
                        pltpu.VMEM((N_HEADS, HEAD, CHUNK), BF16), pltpu.VMEM((CHUNK, HALF), BF16),
                        pltpu.VMEM((nseq, seq_len, HALF), F32), pltpu.VMEM((CHUNK, HALF), F32)],
        compiler_params=_params("arbitrary"),
        name="token_mixer_sample",
    )(x, mod_s, state, *wts, lv)


def _row_blocking(x, per_seq, tile):
    batch, seq, _ = x.shape
    if per_seq:
        groups = tile // seq
        assert batch % groups == 0
        grid = (batch // groups,)
        x_spec = pl.BlockSpec((groups, seq, D_MODEL), lambda i: (i, 0, 0))
        return grid, x_spec, groups, lambda layer: pl.BlockSpec((1, 6, groups, D_MODEL), lambda i: (layer, 0, i, 0))
    assert seq % tile == 0
    per = seq // tile
    grid = (batch * per,)
    x_spec = pl.BlockSpec((1, tile, D_MODEL), lambda i: (i // per, i % per, 0))
    return grid, x_spec, per, lambda layer: pl.BlockSpec((1, 1, 6, D_MODEL), lambda i: (layer, i // per, 0, 0))


def _ffn_dense(layer, x, mod, per_seq, wg, wu, wd, l2w, l2b):
    tile = FFN_TILE if not per_seq else CHUNK
    grid, x_spec, _, mod_spec = _row_blocking(x, per_seq, tile)
    return pl.pallas_call(
        functools.partial(_ffn_dense_kernel, per_seq=per_seq),
        out_shape=jax.ShapeDtypeStruct(x.shape, F32),
        grid=grid,
        in_specs=[x_spec, mod_spec(layer),
                  _const_spec((D_MODEL, D_FF)), _const_spec((D_MODEL, D_FF)), _const_spec((D_FF, D_MODEL)),
                  _const_spec((1, D_MODEL)), _const_spec((1, D_MODEL))],
        out_specs=x_spec,
        compiler_params=_params("arbitrary"),
        name="ffn_dense",
    )(x, mod, wg, wu, wd, l2w, l2b)


def _router(layer, x, mod, per_seq, w_router):
    batch, seq, _ = x.shape
    tile = FFN_TILE if not per_seq else CHUNK
    grid, x_spec, _, mod_spec = _row_blocking(x, per_seq, tile)
    n = batch * seq
    return pl.pallas_call(
        functools.partial(_router_kernel, per_seq=per_seq),
        out_shape=(jax.ShapeDtypeStruct((n, 2), jnp.int32), jax.ShapeDtypeStruct((n, 2), F32)),
        grid=grid,
        in_specs=[x_spec, mod_spec(layer), _const_spec((D_MODEL, N_EXPERTS))],
        out_specs=(pl.BlockSpec((tile, 2), lambda i: (i, 0)), pl.BlockSpec((tile, 2), lambda i: (i, 0))),
        compiler_params=_params("arbitrary"),
        name="moe_router",
    )(x, mod, w_router)


def _slot_spec(tile):
    return pl.BlockSpec((1, 1, tile), lambda i: (i, 0, 0), memory_space=pltpu.SMEM)


def _dispatch(layer, x, mod, per_seq, slot_a, slot_b, hg):
    tile = FFN_TILE if not per_seq else CHUNK
    grid, x_spec, _, mod_spec = _row_blocking(x, per_seq, tile)
    return pl.pallas_call(
        functools.partial(_dispatch_kernel, per_seq=per_seq),
        out_shape=jax.ShapeDtypeStruct(hg.shape, hg.dtype),
        grid=grid,
        in_specs=[x_spec, mod_spec(layer), _slot_spec(tile), _slot_spec(tile), pl.BlockSpec(memory_space=pl.ANY)],
        out_specs=pl.BlockSpec(memory_space=pl.ANY),
        scratch_shapes=[pltpu.VMEM((tile * ROW_TILE, LANE), F32), pltpu.SemaphoreType.DMA(())],
        input_output_aliases={4: 0},
        compiler_params=_params("arbitrary"),
        name="moe_dispatch",
    )(x, mod, slot_a.reshape(-1, 1, tile), slot_b.reshape(-1, 1, tile), hg)


def _experts(hg, tile_expert, wg, wu, wd):
    mp = hg.shape[0] // ROW_TILE
    tile = EXPERT_TILE
    assert mp % tile == 0 and mp // tile >= 2
    return pl.pallas_call(
        _expert_kernel,
        out_shape=jax.ShapeDtypeStruct(hg.shape, F32),
        grid_spec=pltpu.PrefetchScalarGridSpec(
            num_scalar_prefetch=1,
            grid=(mp // tile,),
            in_specs=[pl.BlockSpec(memory_space=pl.ANY),
                      pl.BlockSpec((1, D_MODEL, D_FF), lambda i, te: (te[i], 0, 0)),
                      pl.BlockSpec((1, D_MODEL, D_FF), lambda i, te: (te[i], 0, 0)),
                      pl.BlockSpec((1, D_FF, D_MODEL), lambda i, te: (te[i], 0, 0))],
            out_specs=pl.BlockSpec((tile * ROW_TILE, LANE), lambda i, te: (i, 0)),
            scratch_shapes=[pltpu.VMEM((2, tile * ROW_TILE, LANE), F32), pltpu.VMEM((2, tile, D_MODEL), BF16),
                            pltpu.SemaphoreType.DMA((2,))]),
        compiler_params=_params("arbitrary"),
        name="moe_experts",
    )(tile_expert, hg, wg, wu, wd)


def _combine(layer, x, mod, per_seq, gate, slot_a, slot_b, yo, l2w, l2b):
    tile = FFN_TILE if not per_seq else CHUNK
    grid, x_spec, _, mod_spec = _row_blocking(x, per_seq, tile)
    return pl.pallas_call(
        functools.partial(_combine_kernel, per_seq=per_seq),
        out_shape=jax.ShapeDtypeStruct(x.shape, F32),
        grid=grid,
        in_specs=[x_spec, mod_spec(layer), pl.BlockSpec((tile, 2), lambda i: (i, 0)),
                  _slot_spec(tile), _slot_spec(tile), pl.BlockSpec(memory_space=pl.ANY),
                  _const_spec((1, D_MODEL)), _const_spec((1, D_MODEL))],
        out_specs=x_spec,
        scratch_shapes=[pltpu.VMEM((tile * ROW_TILE, LANE), F32), pltpu.VMEM((tile * ROW_TILE, LANE), F32),
                        pltpu.VMEM((tile, D_MODEL), F32), pltpu.SemaphoreType.DMA(())],
        compiler_params=_params("arbitrary"),
        name="moe_combine",
    )(x, mod, gate, slot_a.reshape(-1, 1, tile), slot_b.reshape(-1, 1, tile), yo, l2w, l2b)


def _routing_tables(idx, n_pad_rows):
    e_flat = jnp.concatenate([idx[:, 0], idx[:, 1]])
    onehot = (e_flat[:, None] == jnp.arange(N_EXPERTS, dtype=jnp.int32)[None, :]).astype(jnp.int32)
    csum = jnp.cumsum(onehot, axis=0)
    padded = ((csum[-1] + EXPERT_TILE - 1) // EXPERT_TILE) * EXPERT_TILE
    pend = jnp.cumsum(padded)
    slot = jnp.sum(onehot * (csum - 1 + (pend - padded)[None, :]), axis=1).astype(jnp.int32)
    tile_start = jnp.arange(n_pad_rows // EXPERT_TILE, dtype=jnp.int32) * EXPERT_TILE
    tile_expert = jnp.sum((tile_start[:, None] >= pend[None, :]).astype(jnp.int32), axis=1)
    return slot, jnp.minimum(tile_expert, N_EXPERTS - 1).astype(jnp.int32)


def _round_up(a, b):
    return (a + b - 1) // b * b


def kernel(x_prompt, x_sample, state_hgrn, c_prompt, c_sample, w_ada, b_ada, w_in, w_out, a_ln_w, a_ln_b, a_ws, a_bs, lb_logits, b_norm_w, ln1_w, ln1_b, ln2_w, ln2_b, w_ff_gate, w_ff_up, w_ff_down, w_router, e_gate, e_up, e_down):
    batch, seq, _ = x_prompt.shape
    nseq, seq_len, _ = x_sample.shape
    n_prompt, n_sample = batch * seq, nseq * seq_len
    n_tok = n_prompt + n_sample

    mod = _adaln(jnp.concatenate([c_prompt, c_sample], axis=0), w_ada, b_ada)
    mod_p = mod[:, :batch].reshape(DEPTH, batch, 6, D_MODEL)
    mod_s = mod[:, batch:].reshape(DEPTH, nseq, 6, D_MODEL).transpose(0, 2, 1, 3)

    lv_p = jnp.asarray(_level_ids(CHUNK))
    lv_s = jnp.asarray(_level_ids(seq_len))
    reps = CHUNK // seq_len
    n_pad_rows = _round_up(2 * n_tok, EXPERT_TILE) + N_EXPERTS * EXPERT_TILE

    xp, xs = x_prompt, x_sample
    st_p, st_s, cv_p, cv_s = [], [], [], []
    for l in range(DEPTH):
        shared = (a_ln_w[l][None], a_ln_b[l][None])
        tail = (lb_logits, b_norm_w[l][None], ln1_w[l][None], ln1_b[l][None])
        w_in_l, w_out_l = w_in[l].astype(BF16), w_out[l].astype(BF16)
        wts_p = (w_in_l, w_out_l) + shared + (a_ws[l], a_bs[l].T) + tail
        ws_s = jnp.tile(a_ws[l][:, :seq_len, :seq_len], (1, reps, reps))
        bs_s = jnp.tile(a_bs[l][:, :seq_len].T, (reps, 1))
        wts_s = (w_in_l, w_out_l) + shared + (ws_s, bs_s) + tail
        xp, sp, vp = _mixer_prompt(l, xp, mod_p, wts_p, lv_p)
        xs, ss, vs = _mixer_sample(l, xs, mod_s, state_hgrn, wts_s, lv_s)
        st_p.append(sp), st_s.append(ss), cv_p.append(vp), cv_s.append(vs)
        l2w, l2b = ln2_w[l][None], ln2_b[l][None]
        if l % 2 == 0:
            wg, wu, wd = (w[l // 2].astype(BF16) for w in (w_ff_gate, w_ff_up, w_ff_down))
            xp = _ffn_dense(l, xp, mod_p, False, wg, wu, wd, l2w, l2b)
            xs = _ffn_dense(l, xs, mod_s, True, wg, wu, wd, l2w, l2b)
        else:
            wr = w_router[l // 2]
            ip, gp = _router(l, xp, mod_p, False, wr)
            is_, gs = _router(l, xs, mod_s, True, wr)
            slot, tile_expert = _routing_tables(jnp.concatenate([ip, is_], axis=0), n_pad_rows)
            sa_p, sa_s = slot[:n_prompt], slot[n_prompt:n_tok]
            sb_p, sb_s = slot[n_tok:n_tok + n_prompt], slot[n_tok + n_prompt:]
            hg = jnp.zeros((n_pad_rows * ROW_TILE, LANE), F32)
            hg = _dispatch(l, xp, mod_p, False, sa_p, sb_p, hg)
            hg = _dispatch(l, xs, mod_s, True, sa_s, sb_s, hg)
            wg, wu, wd = (w[l // 2].astype(BF16) for w in (e_gate, e_up, e_down))
            yo = _experts(hg, tile_expert, wg, wu, wd)
            xp = _combine(l, xp, mod_p, False, gp, sa_p, sb_p, yo, l2w, l2b)
            xs = _combine(l, xs, mod_s, True, gs, sa_s, sb_s, yo, l2w, l2b)
    return (xp, xs, jnp.stack(st_p), jnp.stack(st_s), jnp.stack(cv_p), jnp.stack(cv_s))
```

```python
import functools
import math

import numpy as np
import jax
import jax.numpy as jnp
from jax import lax
from jax.experimental import pallas as pl
from jax.experimental.pallas import tpu as pltpu

F32 = jnp.float32
BF16 = jnp.bfloat16

D_MODEL = 1024
DEPTH = 4
HALF = 512
N_HEADS = 4
HEAD = 128
CHUNK = 128
IN_COLS = 6 * HALF
D_FF = 2816
N_EXPERTS = 8
ALPHA = (2.0 * DEPTH) ** 0.25
LN_EPS = 1e-5
RMS_EPS = 1e-6
LOG2_E = math.log2(math.e)

VMEM_LIMIT_BYTES = 56 * 1024 * 1024
LANE = 128
SUBLANE = 8
ROW_TILE = D_MODEL // LANE
assert ROW_TILE == SUBLANE

PROMPT_TILE = 512
SAMPLE_SEQS = 16
FFN_TILE = 512
EXPERT_TILE = 256
FF_SPLITS = ((0, 1024), (1024, 2048), (2048, D_FF))


def _dot(a, b):
    return jnp.dot(a, b, preferred_element_type=F32)


def _dot_nt(a, b):
    return lax.dot_general(a, b, (((1,), (1,)), ((), ())), preferred_element_type=F32)


def _gelu(x):
    return 0.5 * x * (1.0 + lax.erf(x * (1.0 / math.sqrt(2.0))))


def _silu(x):
    return x * jax.nn.sigmoid(x)


def _layer_norm(z, w, b):
    mu = jnp.mean(z, axis=-1, keepdims=True)
    zc = z - mu
    var = jnp.mean(zc * zc, axis=-1, keepdims=True)
    return zc * lax.rsqrt(var + LN_EPS) * w + b


def _level_ids(block):
    t = np.arange(CHUNK)[:, None]
    s = np.arange(CHUNK)[None, :]
    x = t ^ s
    lv = np.where(x == 0, 0, np.floor(np.log2(np.maximum(x, 1))).astype(np.int64) + 1)
    ok = (s <= t) & (x < block)
    return np.where(ok, lv, -1).astype(np.int32)


def _reference_rows(b, m):
    rows, width = b.shape
    two_m = 2 * m
    if two_m >= SUBLANE:
        nb = rows // two_m
        b3 = b.reshape(nb, two_m, width)
        r = jnp.broadcast_to(b3[:, m - 1:m, :], (nb, two_m, width))
        return r.reshape(rows, width)
    t = lax.broadcasted_iota(jnp.int32, (rows, width), 0)
    tm = t & (two_m - 1)
    down1 = pltpu.roll(b, 1, 0)
    if m == 1:
        return jnp.where(tm == 0, b, down1)
    up1 = pltpu.roll(b, rows - 1, 0)
    down2 = pltpu.roll(b, 2, 0)
    return jnp.where(tm == 0, up1, jnp.where(tm == 1, b, jnp.where(tm == 2, down1, down2)))


def _hgrn_intra(qq, kk, gg, vv, lv, block):
    cm = (lv >= 0).astype(BF16)
    g_hi = gg.astype(BF16)
    rem = gg - g_hi.astype(F32)
    g_mid = rem.astype(BF16)
    g_lo = (rem - g_mid.astype(F32)).astype(BF16)
    b = _dot(cm, g_hi) + _dot(cm, g_mid) + _dot(cm, g_lo)

    heads = [slice(h * HEAD, (h + 1) * HEAD) for h in range(N_HEADS)]
    qb = qq.astype(BF16)
    kb = kk.astype(BF16)
    scores = [jnp.where(lv == 0, _dot_nt(qb[:, hs], kb[:, hs]), 0.0) for hs in heads]
    m = block // 2
    while m >= 1:
        level = int(math.log2(m)) + 1
        e = jnp.exp2(-jnp.abs(b - _reference_rows(b, m)))
        qe = (qq * e).astype(BF16)
        ke = (kk * e).astype(BF16)
        for h, hs in enumerate(heads):
            scores[h] = jnp.where(lv == level, _dot_nt(qe[:, hs], ke[:, hs]), scores[h])
        m //= 2
    vb = vv.astype(BF16)
    o = jnp.concatenate([_dot(scores[h].astype(BF16), vb[:, hs]) for h, hs in enumerate(heads)], axis=1)
    return o, b


def _forget_bound(lbl_ref, layer):
    z = lbl_ref[...]
    z = z - jnp.max(z, axis=0, keepdims=True)
    ez = jnp.exp(z)
    p = ez / jnp.sum(ez, axis=0, keepdims=True)
    c = p[0:1]
    for r in range(1, layer + 1):
        c = c + p[r:r + 1]
    return c - p[0:1]


def _mixer_chunk_front(proj_scr, rows, alnw_ref, alnb_ref, ws_ref, bs_ref, lb, lv):
    u = proj_scr[rows, 0 * HALF:1 * HALF]
    v = proj_scr[rows, 1 * HALF:2 * HALF]
    q = proj_scr[rows, 2 * HALF:3 * HALF]
    f = proj_scr[rows, 3 * HALF:4 * HALF]
    ug = _gelu(u)
    vn = _layer_norm(_gelu(v), alnw_ref[...], alnb_ref[...])
    vnb = vn.astype(BF16)
    a_parts = []
    for h in range(N_HEADS):
        hs = slice(h * HEAD, (h + 1) * HEAD)
        w = jnp.where(lv >= 0, ws_ref[h], 0.0).astype(BF16)
        mixed = _dot(w, vnb[:, hs]) + bs_ref[:, h:h + 1]
        a_parts.append(ug[:, hs] * mixed)
    a_out = jnp.concatenate(a_parts, axis=1)
    fg = lb + (1.0 - lb) * jax.nn.sigmoid(f)
    return a_out, vn, _silu(q), 1.0 - fg, jnp.log(fg) * LOG2_E


def _rms_gate(o, bnw, g):
    parts = []
    for h in range(N_HEADS):
        hs = slice(h * HEAD, (h + 1) * HEAD)
        oh = o[:, hs]
        parts.append(oh * lax.rsqrt(jnp.mean(oh * oh, axis=-1, keepdims=True) + RMS_EPS) * bnw)
    return jnp.concatenate(parts, axis=1) * _silu(g)


def _mixer_prompt_kernel(x_ref, mod_ref, xn_ref, modn_ref, win_ref, wout_ref, alnw_ref, alnb_ref, ws_ref, bs_ref,
                         lbl_ref, bnw_ref, l1w_ref, l1b_ref, lv_ref,
                         x1_ref, st_ref, vn_ref,
                         proj_a, proj_b, hn_scr, mix_scr, s_scr, *, layer, tile):
    step = pl.program_id(1)
    lin = pl.program_id(0) * pl.num_programs(1) + step
    n_chunks = tile // CHUNK
    col_splits = [(IN_COLS * c // n_chunks, IN_COLS * (c + 1) // n_chunks) for c in range(n_chunks)]

    @pl.when(step == 0)
    def _():
        s_scr[...] = jnp.zeros_like(s_scr)

    mod = mod_ref[0, 0]
    x = x_ref[0]

    @pl.when(lin == 0)
    def _():
        proj_a[...] = _dot((x * (1.0 + mod[1:2]) + mod[0:1]).astype(BF16), win_ref[0])

    modn = modn_ref[0, 0]
    hn_scr[...] = (xn_ref[0] * (1.0 + modn[1:2]) + modn[0:1]).astype(BF16)
    lv = lv_ref[...]
    lb = _forget_bound(lbl_ref, layer)

    def chunk(c, proj_scr):
        rows = slice(c * CHUNK, (c + 1) * CHUNK)
        a_out, vn, qq, kk, gg = _mixer_chunk_front(proj_scr, rows, alnw_ref, alnb_ref, ws_ref, bs_ref, lb, lv)
        vn_ref[0] = vn
        vv = proj_scr[rows, 4 * HALF:5 * HALF]
        o_in, b = _hgrn_intra(qq, kk, gg, vv, lv, CHUNK)
        qh = (qq * jnp.exp2(b)).astype(BF16)
        b_last = b[CHUNK - 1:CHUNK, :]
        kdec = kk * jnp.exp2(b_last - b)
        e_last = jnp.exp2(b_last)
        o_parts = []
        for hh in range(N_HEADS):
            hs = slice(hh * HEAD, (hh + 1) * HEAD)
            s_old = s_scr[hh]
            o_parts.append(o_in[:, hs] + _dot(qh[:, hs], s_old.astype(BF16)))
            dec = jnp.broadcast_to(e_last[:, hs], (HEAD, HEAD)).T
            s_scr[hh] = dec * s_old + _dot(kdec[:, hs].T.astype(BF16), vv[:, hs].astype(BF16))
        g = proj_scr[rows, 5 * HALF:6 * HALF]
        b_out = _rms_gate(jnp.concatenate(o_parts, axis=1), bnw_ref[...], g)
        mix_scr[rows, 0:HALF] = a_out.astype(BF16)
        mix_scr[rows, HALF:2 * HALF] = b_out.astype(BF16)

    def run(proj_cur, proj_nxt):
        for c, (c0, c1) in enumerate(col_splits):
            proj_nxt[:, c0:c1] = _dot(hn_scr[...], win_ref[0, :, c0:c1])
            chunk(c, proj_cur)

    @pl.when(lin % 2 == 0)
    def _():
        run(proj_a, proj_b)

    @pl.when(lin % 2 == 1)
    def _():
        run(proj_b, proj_a)

    y = _dot(mix_scr[...], wout_ref[0])
    x1_ref[0] = _layer_norm(ALPHA * x + mod[2:3] * y, l1w_ref[...], l1b_ref[...])

    @pl.when(step == pl.num_programs(1) - 1)
    def _():
        st_ref[0] = s_scr[...]


def _mixer_sample_kernel(x_ref, mod_ref, s0_ref, win_ref, wout_ref, alnw_ref, alnb_ref, ws_ref, bs_ref,
                         lbl_ref, bnw_ref, l1w_ref, l1b_ref, lv_ref, st_in_ref,
                         x1_ref, st_ref, vn_ref,
                         proj_scr, qh_scr, kt_scr, vb_scr, el_scr, o_scr, *, layer, seq_len):
    del st_in_ref
    nseq = SAMPLE_SEQS
    x3 = x_ref[...]
    mod = mod_ref[0]
    h3 = x3 * (1.0 + mod[1][:, None, :]) + mod[0][:, None, :]
    proj_scr[...] = _dot(h3.reshape(CHUNK, D_MODEL).astype(BF16), win_ref[0])
    lv = lv_ref[...]
    lb = _forget_bound(lbl_ref, layer)
    rows = slice(0, CHUNK)
    a_out, vn, qq, kk, gg = _mixer_chunk_front(proj_scr, rows, alnw_ref, alnb_ref, ws_ref, bs_ref, lb, lv)
    vn_ref[...] = vn.reshape(nseq, seq_len, HALF)
    vv = proj_scr[rows, 4 * HALF:5 * HALF]
    o_in, b = _hgrn_intra(qq, kk, gg, vv, lv, seq_len)
    qh_scr[...] = qq * jnp.exp2(b)
    b3 = b.reshape(nseq, seq_len, HALF)
    b_last = jnp.broadcast_to(b3[:, seq_len - 1:seq_len, :], (nseq, seq_len, HALF))
    el_scr[...] = jnp.exp2(b_last)
    kdec = kk * jnp.exp2(b_last.reshape(CHUNK, HALF) - b)
    for hh in range(N_HEADS):
        hs = slice(hh * HEAD, (hh + 1) * HEAD)
        kt_scr[hh] = kdec[:, hs].T.astype(BF16)
    vb_scr[...] = vv.astype(BF16)
    row_seq = lax.broadcasted_iota(jnp.int32, (HEAD, CHUNK), 1) // seq_len

    def per_seq(j, carry):
        rws = pl.ds(pl.multiple_of(j * seq_len, seq_len), seq_len)
        own = row_seq == j
        el = el_scr[j]
        for hh in range(N_HEADS):
            hs = slice(hh * HEAD, (hh + 1) * HEAD)
            s_old = s0_ref[0, j, hh]
            o_scr[rws, hs] = _dot(qh_scr[rws, hs].astype(BF16), s_old.astype(BF16))
            dec = jnp.broadcast_to(el[0:1, hs], (HEAD, HEAD)).T
            kt = jnp.where(own, kt_scr[hh], jnp.zeros((), BF16))
            st_ref[0, j, hh] = dec * s_old + _dot(kt, vb_scr[:, hs])
        return carry

    lax.fori_loop(0, nseq, per_seq, 0)

    g = proj_scr[rows, 5 * HALF:6 * HALF]
    b_out = _rms_gate(o_in + o_scr[...], bnw_ref[...], g)
    mix = jnp.concatenate([a_out, b_out], axis=1).astype(BF16)
    y3 = _dot(mix, wout_ref[0]).reshape(nseq, seq_len, D_MODEL)
    x1_ref[...] = _layer_norm(ALPHA * x3 + mod[2][:, None, :] * y3, l1w_ref[...], l1b_ref[...])


def _mod_rows(mod_ref, per_seq, j):
    if per_seq:
        return mod_ref[0, j][:, None, :]
    return mod_ref[0, 0][j:j + 1][None]


def _swiglu(h, wg_ref, wu_ref, wd_ref):
    acc = None
    for f0, f1 in FF_SPLITS:
        act = (_silu(_dot(h, wg_ref[:, f0:f1])) * _dot(h, wu_ref[:, f0:f1])).astype(BF16)
        part = _dot(act, wd_ref[f0:f1, :])
        acc = part if acc is None else acc + part
    return acc


def _ffn_dense_kernel(x_ref, mod_ref, wg_ref, wu_ref, wd_ref, l2w_ref, l2b_ref, o_ref, *, per_seq):
    x3 = x_ref[...]
    groups, rows, _ = x3.shape
    h = (x3 * (1.0 + _mod_rows(mod_ref, per_seq, 4)) + _mod_rows(mod_ref, per_seq, 3))
    h = h.reshape(groups * rows, D_MODEL).astype(BF16)
    f3 = _swiglu(h, wg_ref.at[0], wu_ref.at[0], wd_ref.at[0]).reshape(groups, rows, D_MODEL)
    o_ref[...] = _layer_norm(ALPHA * x3 + _mod_rows(mod_ref, per_seq, 5) * f3, l2w_ref[...], l2b_ref[...])


def _moe_input(x_ref, mod_ref, per_seq):
    x3 = x_ref[...]
    groups, rows, _ = x3.shape
    h3 = x3 * (1.0 + _mod_rows(mod_ref, per_seq, 4)) + _mod_rows(mod_ref, per_seq, 3)
    return h3.reshape(groups * rows, D_MODEL)


def _router_kernel(x_ref, mod_ref, wr_ref, idx_ref, gate_ref, *, per_seq):
    h = _moe_input(x_ref, mod_ref, per_seq)
    n = h.shape[0]
    logits = jnp.dot(h, wr_ref[...], precision=lax.Precision.HIGHEST, preferred_element_type=F32)
    z = jnp.exp(logits - jnp.max(logits, axis=-1, keepdims=True))
    p = z / jnp.sum(z, axis=-1, keepdims=True)
    lane = lax.broadcasted_iota(jnp.int32, p.shape, 1)
    p1 = jnp.max(p, axis=-1, keepdims=True)
    i1 = jnp.min(jnp.where(p == p1, lane, N_EXPERTS), axis=-1, keepdims=True)
    rest = jnp.where(lane == i1, -1.0, p)
    p2 = jnp.max(rest, axis=-1, keepdims=True)
    i2 = jnp.min(jnp.where(rest == p2, lane, N_EXPERTS), axis=-1, keepdims=True)
    two = lax.broadcasted_iota(jnp.int32, (n, 2), 1)
    idx_ref[...] = jnp.where(two == 0, i1, i2)
    gate_ref[...] = jnp.where(two == 0, p1, p2) / (p1 + p2)


def _for_each_row(n, fn):
    def body(i, carry):
        fn(i)
        return carry
    lax.fori_loop(0, n, body, 0)


def _lane_block(n, s):
    return pl.ds(s, n, stride=ROW_TILE)


def _tile_of_row(r):
    return pl.ds(pl.multiple_of(r * ROW_TILE, ROW_TILE), ROW_TILE)


def _dispatch_kernel(x_ref, mod_ref, da_ref, db_ref, hg_in_ref, hg_ref, rows_scr, sem, *, per_seq):
    del hg_in_ref
    h = _moe_input(x_ref, mod_ref, per_seq)
    n = h.shape[0]
    for s in range(ROW_TILE):
        rows_scr[_lane_block(n, s), :] = h[:, s * LANE:(s + 1) * LANE]

    def copy(i, slot_ref):
        return pltpu.make_async_copy(rows_scr.at[_tile_of_row(i)], hg_ref.at[_tile_of_row(slot_ref[0, 0, i])], sem)

    _for_each_row(n, lambda i: (copy(i, da_ref).start(priority=0), copy(i, db_ref).start(priority=1)))
    _for_each_row(n, lambda i: (copy(i, da_ref).wait(), copy(i, db_ref).wait()))


def _expert_kernel(te_ref, used_ref, hg_ref, wg_ref, wu_ref, wd_ref, o_ref, raw_scr, h_scr, sem):
    del te_ref
    i = pl.program_id(0)
    n = pl.num_programs(0)
    tile = h_scr.shape[1]
    cur, nxt = i % 2, (i + 1) % 2

    def fetch(t, slot):
        rows = pl.ds(pl.multiple_of(t * tile * ROW_TILE, tile * ROW_TILE), tile * ROW_TILE)
        return pltpu.make_async_copy(hg_ref.at[rows], raw_scr.at[slot], sem.at[slot])

    def rearrange(slot):
        for s in range(ROW_TILE):
            h_scr[slot, :, s * LANE:(s + 1) * LANE] = raw_scr[slot, _lane_block(tile, s), :].astype(BF16)

    @pl.when(i == 0)
    def _():
        fetch(0, 0).start()
        fetch(0, 0).wait()
        rearrange(0)
        fetch(1, 1).start()

    @pl.when(i + 1 < n)
    def _():
        fetch(i + 1, nxt).wait()

    @pl.when(i + 2 < n)
    def _():
        fetch(i + 2, cur).start()

    @pl.when(i < used_ref[0])
    def _():
        rearrange(nxt)
        y = _swiglu(h_scr[cur], wg_ref.at[0, 0], wu_ref.at[0, 0], wd_ref.at[0, 0])
        for s in range(ROW_TILE):
            o_ref[_lane_block(tile, s), :] = y[:, s * LANE:(s + 1) * LANE]

    @pl.when(i >= used_ref[0])
    def _():
        o_ref[...] = jnp.zeros_like(o_ref)


def _combine_kernel(x_ref, mod_ref, gate_ref, da_ref, db_ref, yo_ref, l2w_ref, l2b_ref, o_ref,
                    ya_scr, yb_scr, f_scr, sem, *, per_seq):
    x3 = x_ref[...]
    groups, rows, _ = x3.shape
    n = groups * rows

    def copy(i, slot_ref, dst):
        return pltpu.make_async_copy(yo_ref.at[_tile_of_row(slot_ref[0, 0, i])], dst.at[_tile_of_row(i)], sem)

    _for_each_row(n, lambda i: (copy(i, da_ref, ya_scr).start(priority=0),
                                copy(i, db_ref, yb_scr).start(priority=1)))
    _for_each_row(n, lambda i: (copy(i, da_ref, ya_scr).wait(), copy(i, db_ref, yb_scr).wait()))
    ga, gb = gate_ref[:, 0:1], gate_ref[:, 1:2]
    for s in range(ROW_TILE):
        f_scr[:, s * LANE:(s + 1) * LANE] = ga * ya_scr[_lane_block(n, s), :] + gb * yb_scr[_lane_block(n, s), :]
    f3 = f_scr[...].reshape(groups, rows, D_MODEL)
    o_ref[...] = _layer_norm(ALPHA * x3 + _mod_rows(mod_ref, per_seq, 5) * f3, l2w_ref[...], l2b_ref[...])


def _params(*semantics):
    return pltpu.CompilerParams(dimension_semantics=semantics, vmem_limit_bytes=VMEM_LIMIT_BYTES)


def _const_spec(shape):
    return pl.BlockSpec(shape, lambda *_: (0,) * len(shape), pipeline_mode=pl.Buffered(1))


def _adaln(c_all, w_ada, b_ada):
    nb = c_all.shape[0]
    tn = 1536

    def body(c_ref, w_ref, b_ref, o_ref):
        c = c_ref[...]
        o_ref[0] = _dot(_silu(c).astype(BF16), w_ref[0].astype(BF16)) + b_ref[0]

    return pl.pallas_call(
        body,
        out_shape=jax.ShapeDtypeStruct((DEPTH, nb, 6 * D_MODEL), F32),
        grid=(DEPTH, 6 * D_MODEL // tn),
        in_specs=[pl.BlockSpec((nb, D_MODEL), lambda l, j: (0, 0)),
                  pl.BlockSpec((1, D_MODEL, tn), lambda l, j: (l, 0, j)),
                  pl.BlockSpec((1, 1, tn), lambda l, j: (l, 0, j))],
        out_specs=pl.BlockSpec((1, nb, tn), lambda l, j: (l, 0, j)),
        compiler_params=_params("arbitrary", "arbitrary"),
        name="adaln_modulation",
    )(c_all, w_ada, b_ada.reshape(DEPTH, 1, 6 * D_MODEL))


def _layer_spec(shape, layer):
    return pl.BlockSpec((1,) + shape, lambda *_: (layer,) + (0,) * len(shape), pipeline_mode=pl.Buffered(1))


def _mixer_weight_specs(layer):
    return [_layer_spec((D_MODEL, IN_COLS), layer), _layer_spec((D_MODEL, D_MODEL), layer),
            _const_spec((1, HALF)), _const_spec((1, HALF)),
            _const_spec((N_HEADS, CHUNK, CHUNK)), _const_spec((CHUNK, N_HEADS)),
            _const_spec((DEPTH, HALF)), _const_spec((1, HEAD)),
            _const_spec((1, D_MODEL)), _const_spec((1, D_MODEL)), _const_spec((CHUNK, CHUNK))]


def _mixer_prompt(layer, x, mod_p, wts, lv):
    batch, seq, _ = x.shape
    tile = PROMPT_TILE
    rows_out = seq - CHUNK * ((seq - 1) // CHUNK)
    assert seq % tile == 0 and rows_out == CHUNK
    per = seq // tile

    def nxt(b, s):
        return jnp.minimum(b * per + s + 1, batch * per - 1)

    return pl.pallas_call(
        functools.partial(_mixer_prompt_kernel, layer=layer, tile=tile),
        out_shape=(jax.ShapeDtypeStruct((batch, seq, D_MODEL), F32),
                   jax.ShapeDtypeStruct((batch, N_HEADS, HEAD, HEAD), F32),
                   jax.ShapeDtypeStruct((batch, CHUNK, HALF), F32)),
        grid=(batch, per),
        in_specs=[pl.BlockSpec((1, tile, D_MODEL), lambda b, s: (b, s, 0)),
                  pl.BlockSpec((1, 1, 6, D_MODEL), lambda b, s: (layer, b, 0, 0)),
                  pl.BlockSpec((1, tile, D_MODEL), lambda b, s: (nxt(b, s) // per, nxt(b, s) % per, 0)),
                  pl.BlockSpec((1, 1, 6, D_MODEL), lambda b, s: (layer, nxt(b, s) // per, 0, 0))]
        + _mixer_weight_specs(layer),
        out_specs=(pl.BlockSpec((1, tile, D_MODEL), lambda b, s: (b, s, 0)),
                   pl.BlockSpec((1, N_HEADS, HEAD, HEAD), lambda b, s: (b, 0, 0, 0)),
                   pl.BlockSpec((1, CHUNK, HALF), lambda b, s: (b, 0, 0))),
        scratch_shapes=[pltpu.VMEM((tile, IN_COLS), F32), pltpu.VMEM((tile, IN_COLS), F32),
                        pltpu.VMEM((tile, D_MODEL), BF16), pltpu.VMEM((tile, D_MODEL), BF16),
                        pltpu.VMEM((N_HEADS, HEAD, HEAD), F32)],
        compiler_params=_params("arbitrary", "arbitrary"),
        name="token_mixer_prompt",
    )(x, mod_p, x, mod_p, *wts, lv)


def _mixer_sample(layer, x, mod_s, state, wts, lv, states_so_far):
    nseq_all, seq_len, _ = x.shape
    nseq = SAMPLE_SEQS
    assert nseq * seq_len == CHUNK and nseq_all % nseq == 0 and seq_len == SUBLANE
    in_specs = [pl.BlockSpec((nseq, seq_len, D_MODEL), lambda j: (j, 0, 0)),
                pl.BlockSpec((1, 6, nseq, D_MODEL), lambda j: (layer, 0, j, 0)),
                pl.BlockSpec((1, nseq, N_HEADS, HEAD, HEAD), lambda j: (layer, j, 0, 0, 0))]
    in_specs += _mixer_weight_specs(layer) + [pl.BlockSpec(memory_space=pl.ANY)]
    operands = (x, mod_s, state, *wts, lv, states_so_far)
    aliases = {len(operands) - 1: 1}
    return pl.pallas_call(
        functools.partial(_mixer_sample_kernel, layer=layer, seq_len=seq_len),
        out_shape=(jax.ShapeDtypeStruct((nseq_all, seq_len, D_MODEL), F32),
                   jax.ShapeDtypeStruct(state.shape, F32),
                   jax.ShapeDtypeStruct((nseq_all, seq_len, HALF), F32)),
        grid=(nseq_all // nseq,),
        in_specs=in_specs,
        out_specs=(pl.BlockSpec((nseq, seq_len, D_MODEL), lambda j: (j, 0, 0)),
                   pl.BlockSpec((1, nseq, N_HEADS, HEAD, HEAD), lambda j: (layer, j, 0, 0, 0)),
                   pl.BlockSpec((nseq, seq_len, HALF), lambda j: (j, 0, 0))),
        input_output_aliases=aliases,
        scratch_shapes=[pltpu.VMEM((CHUNK, IN_COLS), F32), pltpu.VMEM((CHUNK, HALF), F32),
                        pltpu.VMEM((N_HEADS, HEAD, CHUNK), BF16), pltpu.VMEM((CHUNK, HALF), BF16),
                        pltpu.VMEM((nseq, seq_len, HALF), F32), pltpu.VMEM((CHUNK, HALF), F32)],
        compiler_params=_params("arbitrary"),
        name="token_mixer_sample",
    )(*operands)


def _row_blocking(x, per_seq, tile):
    batch, seq, _ = x.shape
    if per_seq:
        groups = tile // seq
        assert batch % groups == 0
        grid = (batch // groups,)
        x_spec = pl.BlockSpec((groups, seq, D_MODEL), lambda i: (i, 0, 0))
        return grid, x_spec, groups, lambda layer: pl.BlockSpec((1, 6, groups, D_MODEL), lambda i: (layer, 0, i, 0))
    assert seq % tile == 0
    per = seq // tile
    grid = (batch * per,)
    x_spec = pl.BlockSpec((1, tile, D_MODEL), lambda i: (i // per, i % per, 0))
    return grid, x_spec, per, lambda layer: pl.BlockSpec((1, 1, 6, D_MODEL), lambda i: (layer, i // per, 0, 0))


def _ffn_dense(layer, x, mod, per_seq, wg, wu, wd, l2w, l2b):
    tile = FFN_TILE if not per_seq else CHUNK
    grid, x_spec, _, mod_spec = _row_blocking(x, per_seq, tile)
    return pl.pallas_call(
        functools.partial(_ffn_dense_kernel, per_seq=per_seq),
        out_shape=jax.ShapeDtypeStruct(x.shape, F32),
        grid=grid,
        in_specs=[x_spec, mod_spec(layer),
                  _layer_spec((D_MODEL, D_FF), layer // 2), _layer_spec((D_MODEL, D_FF), layer // 2),
                  _layer_spec((D_FF, D_MODEL), layer // 2),
                  _const_spec((1, D_MODEL)), _const_spec((1, D_MODEL))],
        out_specs=x_spec,
        compiler_params=_params("arbitrary"),
        name="ffn_dense",
    )(x, mod, wg, wu, wd, l2w, l2b)


def _router(layer, x, mod, per_seq, w_router):
    batch, seq, _ = x.shape
    tile = FFN_TILE if not per_seq else CHUNK
    grid, x_spec, _, mod_spec = _row_blocking(x, per_seq, tile)
    n = batch * seq
    return pl.pallas_call(
        functools.partial(_router_kernel, per_seq=per_seq),
        out_shape=(jax.ShapeDtypeStruct((n, 2), jnp.int32), jax.ShapeDtypeStruct((n, 2), F32)),
        grid=grid,
        in_specs=[x_spec, mod_spec(layer), _const_spec((D_MODEL, N_EXPERTS))],
        out_specs=(pl.BlockSpec((tile, 2), lambda i: (i, 0)), pl.BlockSpec((tile, 2), lambda i: (i, 0))),
        compiler_params=_params("arbitrary"),
        name="moe_router",
    )(x, mod, w_router)


def _slot_spec(tile):
    return pl.BlockSpec((1, 1, tile), lambda i: (i, 0, 0), memory_space=pltpu.SMEM)


def _dispatch(layer, x, mod, per_seq, slot_a, slot_b, hg):
    tile = FFN_TILE if not per_seq else CHUNK
    grid, x_spec, _, mod_spec = _row_blocking(x, per_seq, tile)
    return pl.pallas_call(
        functools.partial(_dispatch_kernel, per_seq=per_seq),
        out_shape=jax.ShapeDtypeStruct(hg.shape, hg.dtype),
        grid=grid,
        in_specs=[x_spec, mod_spec(layer), _slot_spec(tile), _slot_spec(tile), pl.BlockSpec(memory_space=pl.ANY)],
        out_specs=pl.BlockSpec(memory_space=pl.ANY),
        scratch_shapes=[pltpu.VMEM((tile * ROW_TILE, LANE), F32), pltpu.SemaphoreType.DMA(())],
        input_output_aliases={4: 0},
        compiler_params=_params("arbitrary"),
        name="moe_dispatch",
    )(x, mod, slot_a.reshape(-1, 1, tile), slot_b.reshape(-1, 1, tile), hg)


def _experts(moe_layer, hg, tile_expert, tiles_used, wg, wu, wd):
    mp = hg.shape[0] // ROW_TILE
    tile = EXPERT_TILE
    assert mp % tile == 0 and mp // tile >= 2
    return pl.pallas_call(
        _expert_kernel,
        out_shape=jax.ShapeDtypeStruct(hg.shape, F32),
        grid_spec=pltpu.PrefetchScalarGridSpec(
            num_scalar_prefetch=2,
            grid=(mp // tile,),
            in_specs=[pl.BlockSpec(memory_space=pl.ANY),
                      pl.BlockSpec((1, 1, D_MODEL, D_FF), lambda i, te, used: (moe_layer, te[i], 0, 0)),
                      pl.BlockSpec((1, 1, D_MODEL, D_FF), lambda i, te, used: (moe_layer, te[i], 0, 0)),
                      pl.BlockSpec((1, 1, D_FF, D_MODEL), lambda i, te, used: (moe_layer, te[i], 0, 0))],
            out_specs=pl.BlockSpec((tile * ROW_TILE, LANE), lambda i, te, used: (i, 0)),
            scratch_shapes=[pltpu.VMEM((2, tile * ROW_TILE, LANE), F32), pltpu.VMEM((2, tile, D_MODEL), BF16),
                            pltpu.SemaphoreType.DMA((2,))]),
        compiler_params=_params("arbitrary"),
        name="moe_experts",
    )(tile_expert, tiles_used, hg, wg, wu, wd)


def _combine(layer, x, mod, per_seq, gate, slot_a, slot_b, yo, l2w, l2b):
    tile = FFN_TILE if not per_seq else CHUNK
    grid, x_spec, _, mod_spec = _row_blocking(x, per_seq, tile)
    return pl.pallas_call(
        functools.partial(_combine_kernel, per_seq=per_seq),
        out_shape=jax.ShapeDtypeStruct(x.shape, F32),
        grid=grid,
        in_specs=[x_spec, mod_spec(layer), pl.BlockSpec((tile, 2), lambda i: (i, 0)),
                  _slot_spec(tile), _slot_spec(tile), pl.BlockSpec(memory_space=pl.ANY),
                  _const_spec((1, D_MODEL)), _const_spec((1, D_MODEL))],
        out_specs=x_spec,
        scratch_shapes=[pltpu.VMEM((tile * ROW_TILE, LANE), F32), pltpu.VMEM((tile * ROW_TILE, LANE), F32),
                        pltpu.VMEM((tile, D_MODEL), F32), pltpu.SemaphoreType.DMA(())],
        compiler_params=_params("arbitrary"),
        name="moe_combine",
    )(x, mod, gate, slot_a.reshape(-1, 1, tile), slot_b.reshape(-1, 1, tile), yo, l2w, l2b)


def _routing_tables(idx, n_pad_rows):
    e_flat = jnp.concatenate([idx[:, 0], idx[:, 1]])
    onehot = (e_flat[:, None] == jnp.arange(N_EXPERTS, dtype=jnp.int32)[None, :]).astype(jnp.int32)
    csum = jnp.cumsum(onehot, axis=0)
    padded = ((csum[-1] + EXPERT_TILE - 1) // EXPERT_TILE) * EXPERT_TILE
    pend = jnp.cumsum(padded)
    slot = jnp.sum(onehot * (csum - 1 + (pend - padded)[None, :]), axis=1).astype(jnp.int32)
    tile_start = jnp.arange(n_pad_rows // EXPERT_TILE, dtype=jnp.int32) * EXPERT_TILE
    tile_expert = jnp.sum((tile_start[:, None] >= pend[None, :]).astype(jnp.int32), axis=1)
    tiles_used = (pend[-1:] // EXPERT_TILE).astype(jnp.int32)
    return slot, jnp.minimum(tile_expert, N_EXPERTS - 1).astype(jnp.int32), tiles_used


def _round_up(a, b):
    return (a + b - 1) // b * b


def kernel(x_prompt, x_sample, state_hgrn, c_prompt, c_sample, w_ada, b_ada, w_in, w_out, a_ln_w, a_ln_b, a_ws, a_bs, lb_logits, b_norm_w, ln1_w, ln1_b, ln2_w, ln2_b, w_ff_gate, w_ff_up, w_ff_down, w_router, e_gate, e_up, e_down):
    batch, seq, _ = x_prompt.shape
    nseq, seq_len, _ = x_sample.shape
    n_prompt, n_sample = batch * seq, nseq * seq_len
    n_tok = n_prompt + n_sample

    mod = _adaln(jnp.concatenate([c_prompt, c_sample], axis=0), w_ada, b_ada)
    mod_p = mod[:, :batch].reshape(DEPTH, batch, 6, D_MODEL)
    mod_s = mod[:, batch:].reshape(DEPTH, nseq, 6, D_MODEL).transpose(0, 2, 1, 3)

    lv_p = jnp.asarray(_level_ids(CHUNK))
    lv_s = jnp.asarray(_level_ids(seq_len))
    reps = CHUNK // seq_len
    n_pad_rows = _round_up(2 * n_tok, EXPERT_TILE) + N_EXPERTS * EXPERT_TILE

    w_in_b, w_out_b = w_in.astype(BF16), w_out.astype(BF16)
    ff_b = tuple(w.astype(BF16) for w in (w_ff_gate, w_ff_up, w_ff_down))
    ex_b = tuple(w.astype(BF16) for w in (e_gate, e_up, e_down))

    xp, xs = x_prompt, x_sample
    st_p, st_s, cv_p, cv_s = [], jnp.zeros_like(state_hgrn), [], []
    for l in range(DEPTH):
        shared = (a_ln_w[l][None], a_ln_b[l][None])
        tail = (lb_logits, b_norm_w[l][None], ln1_w[l][None], ln1_b[l][None])
        wts_p = (w_in_b, w_out_b) + shared + (a_ws[l], a_bs[l].T) + tail
        ws_s = jnp.tile(a_ws[l][:, :seq_len, :seq_len], (1, reps, reps))
        bs_s = jnp.tile(a_bs[l][:, :seq_len].T, (reps, 1))
        wts_s = (w_in_b, w_out_b) + shared + (ws_s, bs_s) + tail
        xp, sp, vp = _mixer_prompt(l, xp, mod_p, wts_p, lv_p)
        xs, st_s, vs = _mixer_sample(l, xs, mod_s, state_hgrn, wts_s, lv_s, st_s)
        st_p.append(sp), cv_p.append(vp), cv_s.append(vs)
        l2w, l2b = ln2_w[l][None], ln2_b[l][None]
        if l % 2 == 0:
            xp = _ffn_dense(l, xp, mod_p, False, *ff_b, l2w, l2b)
            xs = _ffn_dense(l, xs, mod_s, True, *ff_b, l2w, l2b)
        else:
            wr = w_router[l // 2]
            ip, gp = _router(l, xp, mod_p, False, wr)
            is_, gs = _router(l, xs, mod_s, True, wr)
            slot, tile_expert, tiles_used = _routing_tables(jnp.concatenate([ip, is_], axis=0), n_pad_rows)
            sa_p, sa_s = slot[:n_prompt], slot[n_prompt:n_tok]
            sb_p, sb_s = slot[n_tok:n_tok + n_prompt], slot[n_tok + n_prompt:]
            hg = jnp.zeros((n_pad_rows * ROW_TILE, LANE), F32)
            hg = _dispatch(l, xp, mod_p, False, sa_p, sb_p, hg)
            hg = _dispatch(l, xs, mod_s, True, sa_s, sb_s, hg)
            yo = _experts(l // 2, hg, tile_expert, tiles_used, *ex_b)
            xp = _combine(l, xp, mod_p, False, gp, sa_p, sb_p, yo, l2w, l2b)
            xs = _combine(l, xs, mod_s, True, gs, sa_s, sb_s, yo, l2w, l2b)
    return (xp, xs, jnp.stack(st_p), st_s, jnp.stack(cv_p), jnp.stack(cv_s))
```

```python
import functools
import math

import numpy as np
import jax
import jax.numpy as jnp
from jax import lax
from jax.experimental import pallas as pl
from jax.experimental.pallas import tpu as pltpu

F32 = jnp.float32
BF16 = jnp.bfloat16

D_MODEL = 1024
DEPTH = 4
HALF = 512
N_HEADS = 4
HEAD = 128
CHUNK = 128
IN_COLS = 6 * HALF
D_FF = 2816
N_EXPERTS = 8
ALPHA = (2.0 * DEPTH) ** 0.25
LN_EPS = 1e-5
RMS_EPS = 1e-6
LOG2_E = math.log2(math.e)

VMEM_LIMIT_BYTES = 56 * 1024 * 1024
LANE = 128
SUBLANE = 8
ROW_TILE = D_MODEL // LANE
assert ROW_TILE == SUBLANE

PROMPT_TILE = 512
SAMPLE_SEQS = 16
FFN_TILE = 512
EXPERT_TILE = 256
FF_SPLITS = ((0, 1024), (1024, 2048), (2048, D_FF))


def _dot(a, b):
    return jnp.dot(a, b, preferred_element_type=F32)


def _dot_nt(a, b):
    return lax.dot_general(a, b, (((1,), (1,)), ((), ())), preferred_element_type=F32)


def _gelu(x):
    return 0.5 * x * (1.0 + lax.erf(x * (1.0 / math.sqrt(2.0))))


def _silu(x):
    return x * jax.nn.sigmoid(x)


def _layer_norm(z, w, b):
    mu = jnp.mean(z, axis=-1, keepdims=True)
    zc = z - mu
    var = jnp.mean(zc * zc, axis=-1, keepdims=True)
    return zc * lax.rsqrt(var + LN_EPS) * w + b


def _level_ids(block):
    t = np.arange(CHUNK)[:, None]
    s = np.arange(CHUNK)[None, :]
    x = t ^ s
    lv = np.where(x == 0, 0, np.floor(np.log2(np.maximum(x, 1))).astype(np.int64) + 1)
    ok = (s <= t) & (x < block)
    return np.where(ok, lv, -1).astype(np.int32)


def _reference_rows(b, m):
    rows, width = b.shape
    two_m = 2 * m
    if two_m >= SUBLANE:
        nb = rows // two_m
        b3 = b.reshape(nb, two_m, width)
        r = jnp.broadcast_to(b3[:, m - 1:m, :], (nb, two_m, width))
        return r.reshape(rows, width)
    t = lax.broadcasted_iota(jnp.int32, (rows, width), 0)
    tm = t & (two_m - 1)
    down1 = pltpu.roll(b, 1, 0)
    if m == 1:
        return jnp.where(tm == 0, b, down1)
    up1 = pltpu.roll(b, rows - 1, 0)
    down2 = pltpu.roll(b, 2, 0)
    return jnp.where(tm == 0, up1, jnp.where(tm == 1, b, jnp.where(tm == 2, down1, down2)))


def _hgrn_intra(qq, kk, gg, vv, lv, block):
    cm = (lv >= 0).astype(BF16)
    g_hi = gg.astype(BF16)
    rem = gg - g_hi.astype(F32)
    g_mid = rem.astype(BF16)
    g_lo = (rem - g_mid.astype(F32)).astype(BF16)
    b = _dot(cm, g_hi) + _dot(cm, g_mid) + _dot(cm, g_lo)

    heads = [slice(h * HEAD, (h + 1) * HEAD) for h in range(N_HEADS)]
    qb = qq.astype(BF16)
    kb = kk.astype(BF16)
    scores = [jnp.where(lv == 0, _dot_nt(qb[:, hs], kb[:, hs]), 0.0) for hs in heads]
    m = block // 2
    while m >= 1:
        level = int(math.log2(m)) + 1
        e = jnp.exp2(-jnp.abs(b - _reference_rows(b, m)))
        qe = (qq * e).astype(BF16)
        ke = (kk * e).astype(BF16)
        for h, hs in enumerate(heads):
            scores[h] = jnp.where(lv == level, _dot_nt(qe[:, hs], ke[:, hs]), scores[h])
        m //= 2
    vb = vv.astype(BF16)
    o = jnp.concatenate([_dot(scores[h].astype(BF16), vb[:, hs]) for h, hs in enumerate(heads)], axis=1)
    return o, b


def _forget_bound(lbl_ref, layer):
    z = lbl_ref[...]
    z = z - jnp.max(z, axis=0, keepdims=True)
    ez = jnp.exp(z)
    p = ez / jnp.sum(ez, axis=0, keepdims=True)
    c = p[0:1]
    for r in range(1, layer + 1):
        c = c + p[r:r + 1]
    return c - p[0:1]


def _mixer_chunk_front(proj_scr, rows, alnw_ref, alnb_ref, ws_ref, bs_ref, lb, lv):
    u = proj_scr[rows, 0 * HALF:1 * HALF]
    v = proj_scr[rows, 1 * HALF:2 * HALF]
    q = proj_scr[rows, 2 * HALF:3 * HALF]
    f = proj_scr[rows, 3 * HALF:4 * HALF]
    ug = _gelu(u)
    vn = _layer_norm(_gelu(v), alnw_ref[...], alnb_ref[...])
    vnb = vn.astype(BF16)
    a_parts = []
    for h in range(N_HEADS):
        hs = slice(h * HEAD, (h + 1) * HEAD)
        w = jnp.where(lv >= 0, ws_ref[h], 0.0).astype(BF16)
        mixed = _dot(w, vnb[:, hs]) + bs_ref[:, h:h + 1]
        a_parts.append(ug[:, hs] * mixed)
    a_out = jnp.concatenate(a_parts, axis=1)
    fg = lb + (1.0 - lb) * jax.nn.sigmoid(f)
    return a_out, vn, _silu(q), 1.0 - fg, jnp.log(fg) * LOG2_E


def _rms_gate(o, bnw, g):
    parts = []
    for h in range(N_HEADS):
        hs = slice(h * HEAD, (h + 1) * HEAD)
        oh = o[:, hs]
        parts.append(oh * lax.rsqrt(jnp.mean(oh * oh, axis=-1, keepdims=True) + RMS_EPS) * bnw)
    return jnp.concatenate(parts, axis=1) * _silu(g)


def _mixer_prompt_kernel(x_ref, mod_ref, xn_ref, modn_ref, win_ref, wout_ref, alnw_ref, alnb_ref, ws_ref, bs_ref,
                         lbl_ref, bnw_ref, l1w_ref, l1b_ref, lv_ref,
                         x1_ref, st_ref, vn_ref,
                         proj_a, proj_b, hn_scr, mix_scr, s_scr, *, layer, tile):
    step = pl.program_id(1)
    lin = pl.program_id(0) * pl.num_programs(1) + step
    n_chunks = tile // CHUNK
    col_splits = [(IN_COLS * c // n_chunks, IN_COLS * (c + 1) // n_chunks) for c in range(n_chunks)]

    @pl.when(step == 0)
    def _():
        s_scr[...] = jnp.zeros_like(s_scr)

    mod = mod_ref[0, 0]
    x = x_ref[0]

    @pl.when(lin == 0)
    def _():
        proj_a[...] = _dot((x * (1.0 + mod[1:2]) + mod[0:1]).astype(BF16), win_ref[0])

    modn = modn_ref[0, 0]
    hn_scr[...] = (xn_ref[0] * (1.0 + modn[1:2]) + modn[0:1]).astype(BF16)
    lv = lv_ref[...]
    lb = _forget_bound(lbl_ref, layer)

    def chunk(c, proj_scr):
        rows = slice(c * CHUNK, (c + 1) * CHUNK)
        a_out, vn, qq, kk, gg = _mixer_chunk_front(proj_scr, rows, alnw_ref, alnb_ref, ws_ref, bs_ref, lb, lv)
        vn_ref[0] = vn
        vv = proj_scr[rows, 4 * HALF:5 * HALF]
        o_in, b = _hgrn_intra(qq, kk, gg, vv, lv, CHUNK)
        qh = (qq * jnp.exp2(b)).astype(BF16)
        b_last = b[CHUNK - 1:CHUNK, :]
        kdec = kk * jnp.exp2(b_last - b)
        e_last = jnp.exp2(b_last)
        o_parts = []
        for hh in range(N_HEADS):
            hs = slice(hh * HEAD, (hh + 1) * HEAD)
            s_old = s_scr[hh]
            o_parts.append(o_in[:, hs] + _dot(qh[:, hs], s_old.astype(BF16)))
            dec = jnp.broadcast_to(e_last[:, hs], (HEAD, HEAD)).T
            s_scr[hh] = dec * s_old + _dot(kdec[:, hs].T.astype(BF16), vv[:, hs].astype(BF16))
        g = proj_scr[rows, 5 * HALF:6 * HALF]
        b_out = _rms_gate(jnp.concatenate(o_parts, axis=1), bnw_ref[...], g)
        mix_scr[rows, 0:HALF] = a_out.astype(BF16)
        mix_scr[rows, HALF:2 * HALF] = b_out.astype(BF16)

    def run(proj_cur, proj_nxt):
        for c, (c0, c1) in enumerate(col_splits):
            proj_nxt[:, c0:c1] = _dot(hn_scr[...], win_ref[0, :, c0:c1])
            chunk(c, proj_cur)

    @pl.when(lin % 2 == 0)
    def _():
        run(proj_a, proj_b)

    @pl.when(lin % 2 == 1)
    def _():
        run(proj_b, proj_a)

    y = _dot(mix_scr[...], wout_ref[0])
    x1_ref[0] = _layer_norm(ALPHA * x + mod[2:3] * y, l1w_ref[...], l1b_ref[...])

    @pl.when(step == pl.num_programs(1) - 1)
    def _():
        st_ref[0] = s_scr[...]


def _mixer_sample_kernel(x_ref, mod_ref, s0_ref, win_ref, wout_ref, alnw_ref, alnb_ref, ws_ref, bs_ref,
                         lbl_ref, bnw_ref, l1w_ref, l1b_ref, lv_ref, st_in_ref,
                         x1_ref, st_ref, vn_ref,
                         proj_scr, qh_scr, kt_scr, vb_scr, el_scr, o_scr, *, layer, seq_len):
    del st_in_ref
    nseq = SAMPLE_SEQS
    x3 = x_ref[...]
    mod = mod_ref[0]
    h3 = x3 * (1.0 + mod[1][:, None, :]) + mod[0][:, None, :]
    proj_scr[...] = _dot(h3.reshape(CHUNK, D_MODEL).astype(BF16), win_ref[0])
    lv = lv_ref[...]
    lb = _forget_bound(lbl_ref, layer)
    rows = slice(0, CHUNK)
    a_out, vn, qq, kk, gg = _mixer_chunk_front(proj_scr, rows, alnw_ref, alnb_ref, ws_ref, bs_ref, lb, lv)
    vn_ref[...] = vn.reshape(nseq, seq_len, HALF)
    vv = proj_scr[rows, 4 * HALF:5 * HALF]
    o_in, b = _hgrn_intra(qq, kk, gg, vv, lv, seq_len)
    qh_scr[...] = qq * jnp.exp2(b)
    b3 = b.reshape(nseq, seq_len, HALF)
    b_last = jnp.broadcast_to(b3[:, seq_len - 1:seq_len, :], (nseq, seq_len, HALF))
    el_scr[...] = jnp.exp2(b_last)
    kdec = kk * jnp.exp2(b_last.reshape(CHUNK, HALF) - b)
    for hh in range(N_HEADS):
        hs = slice(hh * HEAD, (hh + 1) * HEAD)
        kt_scr[hh] = kdec[:, hs].T.astype(BF16)
    vb_scr[...] = vv.astype(BF16)
    row_seq = lax.broadcasted_iota(jnp.int32, (HEAD, CHUNK), 1) // seq_len

    def per_seq(j, carry):
        rws = pl.ds(pl.multiple_of(j * seq_len, seq_len), seq_len)
        own = row_seq == j
        el = el_scr[j]
        for hh in range(N_HEADS):
            hs = slice(hh * HEAD, (hh + 1) * HEAD)
            s_old = s0_ref[0, j, hh]
            o_scr[rws, hs] = _dot(qh_scr[rws, hs].astype(BF16), s_old.astype(BF16))
            dec = jnp.broadcast_to(el[0:1, hs], (HEAD, HEAD)).T
            kt = jnp.where(own, kt_scr[hh], jnp.zeros((), BF16))
            st_ref[0, j, hh] = dec * s_old + _dot(kt, vb_scr[:, hs])
        return carry

    lax.fori_loop(0, nseq, per_seq, 0)

    g = proj_scr[rows, 5 * HALF:6 * HALF]
    b_out = _rms_gate(o_in + o_scr[...], bnw_ref[...], g)
    mix = jnp.concatenate([a_out, b_out], axis=1).astype(BF16)
    y3 = _dot(mix, wout_ref[0]).reshape(nseq, seq_len, D_MODEL)
    x1_ref[...] = _layer_norm(ALPHA * x3 + mod[2][:, None, :] * y3, l1w_ref[...], l1b_ref[...])


def _mod_rows(mod_ref, per_seq, j):
    if per_seq:
        return mod_ref[0, j][:, None, :]
    return mod_ref[0, 0][j:j + 1][None]


def _swiglu(h, wg_ref, wu_ref, wd_ref):
    acc = None
    for f0, f1 in FF_SPLITS:
        act = (_silu(_dot(h, wg_ref[:, f0:f1])) * _dot(h, wu_ref[:, f0:f1])).astype(BF16)
        part = _dot(act, wd_ref[f0:f1, :])
        acc = part if acc is None else acc + part
    return acc


def _ffn_dense_kernel(x_ref, mod_ref, wg_ref, wu_ref, wd_ref, l2w_ref, l2b_ref, o_ref, *, per_seq):
    x3 = x_ref[...]
    groups, rows, _ = x3.shape
    h = (x3 * (1.0 + _mod_rows(mod_ref, per_seq, 4)) + _mod_rows(mod_ref, per_seq, 3))
    h = h.reshape(groups * rows, D_MODEL).astype(BF16)
    f3 = _swiglu(h, wg_ref.at[0], wu_ref.at[0], wd_ref.at[0]).reshape(groups, rows, D_MODEL)
    o_ref[...] = _layer_norm(ALPHA * x3 + _mod_rows(mod_ref, per_seq, 5) * f3, l2w_ref[...], l2b_ref[...])


def _moe_input(x_ref, mod_ref, per_seq):
    x3 = x_ref[...]
    groups, rows, _ = x3.shape
    h3 = x3 * (1.0 + _mod_rows(mod_ref, per_seq, 4)) + _mod_rows(mod_ref, per_seq, 3)
    return h3.reshape(groups * rows, D_MODEL)


def _router_kernel(x_ref, mod_ref, wr_ref, idx_ref, gate_ref, *, per_seq):
    h = _moe_input(x_ref, mod_ref, per_seq)
    n = h.shape[0]
    wr = wr_ref[...]
    h_hi, w_hi = h.astype(BF16), wr.astype(BF16)
    h_lo, w_lo = (h - h_hi.astype(F32)).astype(BF16), (wr - w_hi.astype(F32)).astype(BF16)
    logits = _dot(h_hi, w_hi) + (_dot(h_hi, w_lo) + _dot(h_lo, w_hi))
    z = jnp.exp(logits - jnp.max(logits, axis=-1, keepdims=True))
    p = z / jnp.sum(z, axis=-1, keepdims=True)
    lane = lax.broadcasted_iota(jnp.int32, p.shape, 1)
    p1 = jnp.max(p, axis=-1, keepdims=True)
    i1 = jnp.min(jnp.where(p == p1, lane, N_EXPERTS), axis=-1, keepdims=True)
    rest = jnp.where(lane == i1, -1.0, p)
    p2 = jnp.max(rest, axis=-1, keepdims=True)
    i2 = jnp.min(jnp.where(rest == p2, lane, N_EXPERTS), axis=-1, keepdims=True)
    two = lax.broadcasted_iota(jnp.int32, (n, 2), 1)
    idx_ref[...] = jnp.where(two == 0, i1, i2)
    gate_ref[...] = jnp.where(two == 0, p1, p2) / (p1 + p2)


def _for_each_row(n, fn):
    def body(i, carry):
        fn(i)
        return carry
    lax.fori_loop(0, n, body, 0, unroll=4)


def _lane_block(n, s):
    return pl.ds(s, n, stride=ROW_TILE)


def _tile_of_row(r):
    return pl.ds(pl.multiple_of(r * ROW_TILE, ROW_TILE), ROW_TILE)


def _dispatch_kernel(x_ref, mod_ref, da_ref, db_ref, hg_in_ref, hg_ref, rows_scr, sem, *, per_seq):
    del hg_in_ref
    h = _moe_input(x_ref, mod_ref, per_seq)
    n = h.shape[0]
    for s in range(ROW_TILE):
        rows_scr[_lane_block(n, s), :] = h[:, s * LANE:(s + 1) * LANE]

    def copy(i, slot_ref):
        return pltpu.make_async_copy(rows_scr.at[_tile_of_row(i)], hg_ref.at[_tile_of_row(slot_ref[0, 0, i])], sem)

    _for_each_row(n, lambda i: (copy(i, da_ref).start(priority=0), copy(i, db_ref).start(priority=1)))
    all_rows = pltpu.make_async_copy(rows_scr, hg_ref.at[pl.ds(0, n * ROW_TILE)], sem)
    all_rows.wait()
    all_rows.wait()


def _expert_kernel(te_ref, used_ref, hg_ref, wg_ref, wu_ref, wd_ref, o_ref, raw_scr, h_scr, sem):
    del te_ref
    i = pl.program_id(0)
    n = pl.num_programs(0)
    tile = h_scr.shape[1]
    cur, nxt = i % 2, (i + 1) % 2

    def fetch(t, slot):
        rows = pl.ds(pl.multiple_of(t * tile * ROW_TILE, tile * ROW_TILE), tile * ROW_TILE)
        return pltpu.make_async_copy(hg_ref.at[rows], raw_scr.at[slot], sem.at[slot])

    def rearrange(slot):
        for s in range(ROW_TILE):
            h_scr[slot, :, s * LANE:(s + 1) * LANE] = raw_scr[slot, _lane_block(tile, s), :].astype(BF16)

    @pl.when(i == 0)
    def _():
        fetch(0, 0).start()
        fetch(0, 0).wait()
        rearrange(0)
        fetch(1, 1).start()

    @pl.when(i + 1 < n)
    def _():
        fetch(i + 1, nxt).wait()

    @pl.when(i + 2 < n)
    def _():
        fetch(i + 2, cur).start()

    @pl.when(i < used_ref[0])
    def _():
        rearrange(nxt)
        y = _swiglu(h_scr[cur], wg_ref.at[0, 0], wu_ref.at[0, 0], wd_ref.at[0, 0])
        for s in range(ROW_TILE):
            o_ref[_lane_block(tile, s), :] = y[:, s * LANE:(s + 1) * LANE]

    @pl.when(i >= used_ref[0])
    def _():
        o_ref[...] = jnp.zeros_like(o_ref)


def _combine_kernel(x_ref, mod_ref, gate_ref, da_ref, db_ref, yo_ref, l2w_ref, l2b_ref, o_ref,
                    ya_scr, yb_scr, f_scr, sem, *, per_seq):
    x3 = x_ref[...]
    groups, rows, _ = x3.shape
    n = groups * rows

    def copy(i, slot_ref, dst):
        return pltpu.make_async_copy(yo_ref.at[_tile_of_row(slot_ref[0, 0, i])], dst.at[_tile_of_row(i)], sem)

    _for_each_row(n, lambda i: (copy(i, da_ref, ya_scr).start(priority=0),
                                copy(i, db_ref, yb_scr).start(priority=1)))
    for dst in (ya_scr, yb_scr):
        pltpu.make_async_copy(yo_ref.at[pl.ds(0, n * ROW_TILE)], dst, sem).wait()
    ga, gb = gate_ref[:, 0:1], gate_ref[:, 1:2]
    for s in range(ROW_TILE):
        f_scr[:, s * LANE:(s + 1) * LANE] = ga * ya_scr[_lane_block(n, s), :] + gb * yb_scr[_lane_block(n, s), :]
    f3 = f_scr[...].reshape(groups, rows, D_MODEL)
    o_ref[...] = _layer_norm(ALPHA * x3 + _mod_rows(mod_ref, per_seq, 5) * f3, l2w_ref[...], l2b_ref[...])


def _params(*semantics):
    return pltpu.CompilerParams(dimension_semantics=semantics, vmem_limit_bytes=VMEM_LIMIT_BYTES)


def _const_spec(shape):
    return pl.BlockSpec(shape, lambda *_: (0,) * len(shape), pipeline_mode=pl.Buffered(1))


def _adaln(c_all, w_ada, b_ada):
    nb = c_all.shape[0]
    tn = 1536

    def body(c_ref, w_ref, b_ref, o_ref):
        c = c_ref[...]
        o_ref[0] = _dot(_silu(c).astype(BF16), w_ref[0].astype(BF16)) + b_ref[0]

    return pl.pallas_call(
        body,
        out_shape=jax.ShapeDtypeStruct((DEPTH, nb, 6 * D_MODEL), F32),
        grid=(DEPTH, 6 * D_MODEL // tn),
        in_specs=[pl.BlockSpec((nb, D_MODEL), lambda l, j: (0, 0)),
                  pl.BlockSpec((1, D_MODEL, tn), lambda l, j: (l, 0, j)),
                  pl.BlockSpec((1, 1, tn), lambda l, j: (l, 0, j))],
        out_specs=pl.BlockSpec((1, nb, tn), lambda l, j: (l, 0, j)),
        compiler_params=_params("arbitrary", "arbitrary"),
        name="adaln_modulation",
    )(c_all, w_ada, b_ada.reshape(DEPTH, 1, 6 * D_MODEL))


def _cast_bf16(w):
    rows, cols = w.shape[-2:]
    w3 = w.reshape(-1, rows, cols)

    def body(w_ref, o_ref):
        o_ref[...] = w_ref[...].astype(BF16)

    out = pl.pallas_call(
        body,
        out_shape=jax.ShapeDtypeStruct(w3.shape, BF16),
        grid=(w3.shape[0],),
        in_specs=[pl.BlockSpec((1, rows, cols), lambda g: (g, 0, 0))],
        out_specs=pl.BlockSpec((1, rows, cols), lambda g: (g, 0, 0)),
        compiler_params=_params("arbitrary"),
        name="cast_bf16",
    )(w3)
    return out.reshape(w.shape)


def _layer_spec(shape, layer):
    return pl.BlockSpec((1,) + shape, lambda *_: (layer,) + (0,) * len(shape), pipeline_mode=pl.Buffered(1))


def _mixer_weight_specs(layer):
    return [_layer_spec((D_MODEL, IN_COLS), layer), _layer_spec((D_MODEL, D_MODEL), layer),
            _const_spec((1, HALF)), _const_spec((1, HALF)),
            _const_spec((N_HEADS, CHUNK, CHUNK)), _const_spec((CHUNK, N_HEADS)),
            _const_spec((DEPTH, HALF)), _const_spec((1, HEAD)),
            _const_spec((1, D_MODEL)), _const_spec((1, D_MODEL)), _const_spec((CHUNK, CHUNK))]


def _mixer_prompt(layer, x, mod_p, wts, lv):
    batch, seq, _ = x.shape
    tile = PROMPT_TILE
    rows_out = seq - CHUNK * ((seq - 1) // CHUNK)
    assert seq % tile == 0 and rows_out == CHUNK
    per = seq // tile

    def nxt(b, s):
        return jnp.minimum(b * per + s + 1, batch * per - 1)

    return pl.pallas_call(
        functools.partial(_mixer_prompt_kernel, layer=layer, tile=tile),
        out_shape=(jax.ShapeDtypeStruct((batch, seq, D_MODEL), F32),
                   jax.ShapeDtypeStruct((batch, N_HEADS, HEAD, HEAD), F32),
                   jax.ShapeDtypeStruct((batch, CHUNK, HALF), F32)),
        grid=(batch, per),
        in_specs=[pl.BlockSpec((1, tile, D_MODEL), lambda b, s: (b, s, 0)),
                  pl.BlockSpec((1, 1, 6, D_MODEL), lambda b, s: (layer, b, 0, 0)),
                  pl.BlockSpec((1, tile, D_MODEL), lambda b, s: (nxt(b, s) // per, nxt(b, s) % per, 0)),
                  pl.BlockSpec((1, 1, 6, D_MODEL), lambda b, s: (layer, nxt(b, s) // per, 0, 0))]
        + _mixer_weight_specs(layer),
        out_specs=(pl.BlockSpec((1, tile, D_MODEL), lambda b, s: (b, s, 0)),
                   pl.BlockSpec((1, N_HEADS, HEAD, HEAD), lambda b, s: (b, 0, 0, 0)),
                   pl.BlockSpec((1, CHUNK, HALF), lambda b, s: (b, 0, 0))),
        scratch_shapes=[pltpu.VMEM((tile, IN_COLS), F32), pltpu.VMEM((tile, IN_COLS), F32),
                        pltpu.VMEM((tile, D_MODEL), BF16), pltpu.VMEM((tile, D_MODEL), BF16),
                        pltpu.VMEM((N_HEADS, HEAD, HEAD), F32)],
        compiler_params=_params("arbitrary", "arbitrary"),
        name="token_mixer_prompt",
    )(x, mod_p, x, mod_p, *wts, lv)


def _mixer_sample(layer, x, mod_s, state, wts, lv, states_so_far):
    nseq_all, seq_len, _ = x.shape
    nseq = SAMPLE_SEQS
    assert nseq * seq_len == CHUNK and nseq_all % nseq == 0 and seq_len == SUBLANE
    in_specs = [pl.BlockSpec((nseq, seq_len, D_MODEL), lambda j: (j, 0, 0)),
                pl.BlockSpec((1, 6, nseq, D_MODEL), lambda j: (layer, 0, j, 0)),
                pl.BlockSpec((1, nseq, N_HEADS, HEAD, HEAD), lambda j: (layer, j, 0, 0, 0))]
    in_specs += _mixer_weight_specs(layer) + [pl.BlockSpec(memory_space=pl.ANY)]
    operands = (x, mod_s, state, *wts, lv, states_so_far)
    aliases = {len(operands) - 1: 1}
    return pl.pallas_call(
        functools.partial(_mixer_sample_kernel, layer=layer, seq_len=seq_len),
        out_shape=(jax.ShapeDtypeStruct((nseq_all, seq_len, D_MODEL), F32),
                   jax.ShapeDtypeStruct(state.shape, F32),
                   jax.ShapeDtypeStruct((nseq_all, seq_len, HALF), F32)),
        grid=(nseq_all // nseq,),
        in_specs=in_specs,
        out_specs=(pl.BlockSpec((nseq, seq_len, D_MODEL), lambda j: (j, 0, 0)),
                   pl.BlockSpec((1, nseq, N_HEADS, HEAD, HEAD), lambda j: (layer, j, 0, 0, 0)),
                   pl.BlockSpec((nseq, seq_len, HALF), lambda j: (j, 0, 0))),
        input_output_aliases=aliases,
        scratch_shapes=[pltpu.VMEM((CHUNK, IN_COLS), F32), pltpu.VMEM((CHUNK, HALF), F32),
                        pltpu.VMEM((N_HEADS, HEAD, CHUNK), BF16), pltpu.VMEM((CHUNK, HALF), BF16),
                        pltpu.VMEM((nseq, seq_len, HALF), F32), pltpu.VMEM((CHUNK, HALF), F32)],
        compiler_params=_params("arbitrary"),
        name="token_mixer_sample",
    )(*operands)


def _row_blocking(x, per_seq, tile):
    batch, seq, _ = x.shape
    if per_seq:
        groups = tile // seq
        assert batch % groups == 0
        grid = (batch // groups,)
        x_spec = pl.BlockSpec((groups, seq, D_MODEL), lambda i: (i, 0, 0))
        return grid, x_spec, groups, lambda layer: pl.BlockSpec((1, 6, groups, D_MODEL), lambda i: (layer, 0, i, 0))
    assert seq % tile == 0
    per = seq // tile
    grid = (batch * per,)
    x_spec = pl.BlockSpec((1, tile, D_MODEL), lambda i: (i // per, i % per, 0))
    return grid, x_spec, per, lambda layer: pl.BlockSpec((1, 1, 6, D_MODEL), lambda i: (layer, i // per, 0, 0))


def _ffn_dense(layer, x, mod, per_seq, wg, wu, wd, l2w, l2b):
    tile = FFN_TILE if not per_seq else CHUNK
    grid, x_spec, _, mod_spec = _row_blocking(x, per_seq, tile)
    return pl.pallas_call(
        functools.partial(_ffn_dense_kernel, per_seq=per_seq),
        out_shape=jax.ShapeDtypeStruct(x.shape, F32),
        grid=grid,
        in_specs=[x_spec, mod_spec(layer),
                  _layer_spec((D_MODEL, D_FF), layer // 2), _layer_spec((D_MODEL, D_FF), layer // 2),
                  _layer_spec((D_FF, D_MODEL), layer // 2),
                  _const_spec((1, D_MODEL)), _const_spec((1, D_MODEL))],
        out_specs=x_spec,
        compiler_params=_params("arbitrary"),
        name="ffn_dense",
    )(x, mod, wg, wu, wd, l2w, l2b)


def _router(layer, x, mod, per_seq, w_router):
    batch, seq, _ = x.shape
    tile = FFN_TILE if not per_seq else CHUNK
    grid, x_spec, _, mod_spec = _row_blocking(x, per_seq, tile)
    n = batch * seq
    return pl.pallas_call(
        functools.partial(_router_kernel, per_seq=per_seq),
        out_shape=(jax.ShapeDtypeStruct((n, 2), jnp.int32), jax.ShapeDtypeStruct((n, 2), F32)),
        grid=grid,
        in_specs=[x_spec, mod_spec(layer), _const_spec((D_MODEL, N_EXPERTS))],
        out_specs=(pl.BlockSpec((tile, 2), lambda i: (i, 0)), pl.BlockSpec((tile, 2), lambda i: (i, 0))),
        compiler_params=_params("arbitrary"),
        name="moe_router",
    )(x, mod, w_router)


def _slot_spec(tile):
    return pl.BlockSpec((1, 1, tile), lambda i: (i, 0, 0), memory_space=pltpu.SMEM)


def _dispatch(layer, x, mod, per_seq, slot_a, slot_b, hg):
    tile = FFN_TILE if not per_seq else CHUNK
    grid, x_spec, _, mod_spec = _row_blocking(x, per_seq, tile)
    return pl.pallas_call(
        functools.partial(_dispatch_kernel, per_seq=per_seq),
        out_shape=jax.ShapeDtypeStruct(hg.shape, hg.dtype),
        grid=grid,
        in_specs=[x_spec, mod_spec(layer), _slot_spec(tile), _slot_spec(tile), pl.BlockSpec(memory_space=pl.ANY)],
        out_specs=pl.BlockSpec(memory_space=pl.ANY),
        scratch_shapes=[pltpu.VMEM((tile * ROW_TILE, LANE), F32), pltpu.SemaphoreType.DMA(())],
        input_output_aliases={4: 0},
        compiler_params=_params("arbitrary"),
        name="moe_dispatch",
    )(x, mod, slot_a.reshape(-1, 1, tile), slot_b.reshape(-1, 1, tile), hg)


def _experts(moe_layer, hg, tile_expert, tiles_used, wg, wu, wd):
    mp = hg.shape[0] // ROW_TILE
    tile = EXPERT_TILE
    assert mp % tile == 0 and mp // tile >= 2
    return pl.pallas_call(
        _expert_kernel,
        out_shape=jax.ShapeDtypeStruct(hg.shape, F32),
        grid_spec=pltpu.PrefetchScalarGridSpec(
            num_scalar_prefetch=2,
            grid=(mp // tile,),
            in_specs=[pl.BlockSpec(memory_space=pl.ANY),
                      pl.BlockSpec((1, 1, D_MODEL, D_FF), lambda i, te, used: (moe_layer, te[i], 0, 0)),
                      pl.BlockSpec((1, 1, D_MODEL, D_FF), lambda i, te, used: (moe_layer, te[i], 0, 0)),
                      pl.BlockSpec((1, 1, D_FF, D_MODEL), lambda i, te, used: (moe_layer, te[i], 0, 0))],
            out_specs=pl.BlockSpec((tile * ROW_TILE, LANE), lambda i, te, used: (i, 0)),
            scratch_shapes=[pltpu.VMEM((2, tile * ROW_TILE, LANE), F32), pltpu.VMEM((2, tile, D_MODEL), BF16),
                            pltpu.SemaphoreType.DMA((2,))]),
        compiler_params=_params("arbitrary"),
        name="moe_experts",
    )(tile_expert, tiles_used, hg, wg, wu, wd)


def _combine(layer, x, mod, per_seq, gate, slot_a, slot_b, yo, l2w, l2b):
    tile = FFN_TILE if not per_seq else CHUNK
    grid, x_spec, _, mod_spec = _row_blocking(x, per_seq, tile)
    return pl.pallas_call(
        functools.partial(_combine_kernel, per_seq=per_seq),
        out_shape=jax.ShapeDtypeStruct(x.shape, F32),
        grid=grid,
        in_specs=[x_spec, mod_spec(layer), pl.BlockSpec((tile, 2), lambda i: (i, 0)),
                  _slot_spec(tile), _slot_spec(tile), pl.BlockSpec(memory_space=pl.ANY),
                  _const_spec((1, D_MODEL)), _const_spec((1, D_MODEL))],
        out_specs=x_spec,
        scratch_shapes=[pltpu.VMEM((tile * ROW_TILE, LANE), F32), pltpu.VMEM((tile * ROW_TILE, LANE), F32),
                        pltpu.VMEM((tile, D_MODEL), F32), pltpu.SemaphoreType.DMA(())],
        compiler_params=_params("arbitrary"),
        name="moe_combine",
    )(x, mod, gate, slot_a.reshape(-1, 1, tile), slot_b.reshape(-1, 1, tile), yo, l2w, l2b)


def _routing_tables(idx, n_pad_rows):
    e_flat = jnp.concatenate([idx[:, 0], idx[:, 1]])
    onehot = (e_flat[:, None] == jnp.arange(N_EXPERTS, dtype=jnp.int32)[None, :]).astype(jnp.int32)
    csum = jnp.cumsum(onehot, axis=0)
    padded = ((csum[-1] + EXPERT_TILE - 1) // EXPERT_TILE) * EXPERT_TILE
    pend = jnp.cumsum(padded)
    slot = jnp.sum(onehot * (csum - 1 + (pend - padded)[None, :]), axis=1).astype(jnp.int32)
    tile_start = jnp.arange(n_pad_rows // EXPERT_TILE, dtype=jnp.int32) * EXPERT_TILE
    tile_expert = jnp.sum((tile_start[:, None] >= pend[None, :]).astype(jnp.int32), axis=1)
    tiles_used = (pend[-1:] // EXPERT_TILE).astype(jnp.int32)
    return slot, jnp.minimum(tile_expert, N_EXPERTS - 1).astype(jnp.int32), tiles_used


def _round_up(a, b):
    return (a + b - 1) // b * b


def kernel(x_prompt, x_sample, state_hgrn, c_prompt, c_sample, w_ada, b_ada, w_in, w_out, a_ln_w, a_ln_b, a_ws, a_bs, lb_logits, b_norm_w, ln1_w, ln1_b, ln2_w, ln2_b, w_ff_gate, w_ff_up, w_ff_down, w_router, e_gate, e_up, e_down):
    batch, seq, _ = x_prompt.shape
    nseq, seq_len, _ = x_sample.shape
    n_prompt, n_sample = batch * seq, nseq * seq_len
    n_tok = n_prompt + n_sample

    mod = _adaln(jnp.concatenate([c_prompt, c_sample], axis=0), w_ada, b_ada)
    mod_p = mod[:, :batch].reshape(DEPTH, batch, 6, D_MODEL)
    mod_s = mod[:, batch:].reshape(DEPTH, nseq, 6, D_MODEL).transpose(0, 2, 1, 3)

    lv_p = jnp.asarray(_level_ids(CHUNK))
    lv_s = jnp.asarray(_level_ids(seq_len))
    reps = CHUNK // seq_len
    n_pad_rows = _round_up(2 * n_tok, EXPERT_TILE) + N_EXPERTS * EXPERT_TILE

    w_in_b, w_out_b = _cast_bf16(w_in), _cast_bf16(w_out)
    ff_b = tuple(_cast_bf16(w) for w in (w_ff_gate, w_ff_up, w_ff_down))
    ex_b = tuple(_cast_bf16(w) for w in (e_gate, e_up, e_down))

    xp, xs = x_prompt, x_sample
    st_p, st_s, cv_p, cv_s = [], jnp.zeros_like(state_hgrn), [], []
    for l in range(DEPTH):
        shared = (a_ln_w[l][None], a_ln_b[l][None])
        tail = (lb_logits, b_norm_w[l][None], ln1_w[l][None], ln1_b[l][None])
        wts_p = (w_in_b, w_out_b) + shared + (a_ws[l], a_bs[l].T) + tail
        ws_s = jnp.tile(a_ws[l][:, :seq_len, :seq_len], (1, reps, reps))
        bs_s = jnp.tile(a_bs[l][:, :seq_len].T, (reps, 1))
        wts_s = (w_in_b, w_out_b) + shared + (ws_s, bs_s) + tail
        xp, sp, vp = _mixer_prompt(l, xp, mod_p, wts_p, lv_p)
        xs, st_s, vs = _mixer_sample(l, xs, mod_s, state_hgrn, wts_s, lv_s, st_s)
        st_p.append(sp), cv_p.append(vp), cv_s.append(vs)
        l2w, l2b = ln2_w[l][None], ln2_b[l][None]
        if l % 2 == 0:
            xp = _ffn_dense(l, xp, mod_p, False, *ff_b, l2w, l2b)
            xs = _ffn_dense(l, xs, mod_s, True, *ff_b, l2w, l2b)
        else:
            wr = w_router[l // 2]
            ip, gp = _router(l, xp, mod_p, False, wr)
            is_, gs = _router(l, xs, mod_s, True, wr)
            slot, tile_expert, tiles_used = _routing_tables(jnp.concatenate([ip, is_], axis=0), n_pad_rows)
            sa_p, sa_s = slot[:n_prompt], slot[n_prompt:n_tok]
            sb_p, sb_s = slot[n_tok:n_tok + n_prompt], slot[n_tok + n_prompt:]
            hg = jnp.zeros((n_pad_rows * ROW_TILE, LANE), F32)
            hg = _dispatch(l, xp, mod_p, False, sa_p, sb_p, hg)
            hg = _dispatch(l, xs, mod_s, True, sa_s, sb_s, hg)
            yo = _experts(l // 2, hg, tile_expert, tiles_used, *ex_b)
            xp = _combine(l, xp, mod_p, False, gp, sa_p, sb_p, yo, l2w, l2b)
            xs = _combine(l, xs, mod_s, True, gs, sa_s, sb_s, yo, l2w, l2b)
    return (xp, xs, jnp.stack(st_p), st_s, jnp.stack(cv_p), jnp.stack(cv_s))
```

```python
import functools
import math

import numpy as np
import jax
import jax.numpy as jnp
from jax import lax
from jax.experimental import pallas as pl
from jax.experimental.pallas import tpu as pltpu

F32 = jnp.float32
BF16 = jnp.bfloat16

D_MODEL = 1024
DEPTH = 4
HALF = 512
N_HEADS = 4
HEAD = 128
CHUNK = 128
IN_COLS = 6 * HALF
D_FF = 2816
N_EXPERTS = 8
ALPHA = (2.0 * DEPTH) ** 0.25
LN_EPS = 1e-5
RMS_EPS = 1e-6
LOG2_E = math.log2(math.e)

VMEM_LIMIT_BYTES = 56 * 1024 * 1024
LANE = 128
SUBLANE = 8
ROW_TILE = D_MODEL // LANE
assert ROW_TILE == SUBLANE

PROMPT_TILE = 512
SAMPLE_SEQS = 16
FFN_TILE = 512
EXPERT_TILE = 256
FF_SPLITS = ((0, 1024), (1024, 2048), (2048, D_FF))


def _dot(a, b):
    return jnp.dot(a, b, preferred_element_type=F32)


def _dot_nt(a, b):
    return lax.dot_general(a, b, (((1,), (1,)), ((), ())), preferred_element_type=F32)


def _gelu(x):
    return 0.5 * x * (1.0 + lax.erf(x * (1.0 / math.sqrt(2.0))))


def _silu(x):
    return x * jax.nn.sigmoid(x)


def _layer_norm(z, w, b):
    mu = jnp.mean(z, axis=-1, keepdims=True)
    zc = z - mu
    var = jnp.mean(zc * zc, axis=-1, keepdims=True)
    return zc * lax.rsqrt(var + LN_EPS) * w + b


def _level_ids(block):
    t = np.arange(CHUNK)[:, None]
    s = np.arange(CHUNK)[None, :]
    x = t ^ s
    lv = np.where(x == 0, 0, np.floor(np.log2(np.maximum(x, 1))).astype(np.int64) + 1)
    ok = (s <= t) & (x < block)
    return np.where(ok, lv, -1).astype(np.int32)


def _reference_rows(b, m):
    rows, width = b.shape
    two_m = 2 * m
    if two_m >= SUBLANE:
        nb = rows // two_m
        b3 = b.reshape(nb, two_m, width)
        r = jnp.broadcast_to(b3[:, m - 1:m, :], (nb, two_m, width))
        return r.reshape(rows, width)
    t = lax.broadcasted_iota(jnp.int32, (rows, width), 0)
    tm = t & (two_m - 1)
    down1 = pltpu.roll(b, 1, 0)
    if m == 1:
        return jnp.where(tm == 0, b, down1)
    up1 = pltpu.roll(b, rows - 1, 0)
    down2 = pltpu.roll(b, 2, 0)
    return jnp.where(tm == 0, up1, jnp.where(tm == 1, b, jnp.where(tm == 2, down1, down2)))


def _hgrn_intra(qq, kk, gg, vv, lv, block):
    cm = (lv >= 0).astype(BF16)
    g_hi = gg.astype(BF16)
    rem = gg - g_hi.astype(F32)
    g_mid = rem.astype(BF16)
    g_lo = (rem - g_mid.astype(F32)).astype(BF16)
    b = _dot(cm, g_hi) + _dot(cm, g_mid) + _dot(cm, g_lo)

    heads = [slice(h * HEAD, (h + 1) * HEAD) for h in range(N_HEADS)]
    qb = qq.astype(BF16)
    kb = kk.astype(BF16)
    scores = [jnp.where(lv == 0, _dot_nt(qb[:, hs], kb[:, hs]), 0.0) for hs in heads]
    m = block // 2
    while m >= 1:
        level = int(math.log2(m)) + 1
        e = jnp.exp2(-jnp.abs(b - _reference_rows(b, m)))
        qe = (qq * e).astype(BF16)
        ke = (kk * e).astype(BF16)
        for h, hs in enumerate(heads):
            scores[h] = jnp.where(lv == level, _dot_nt(qe[:, hs], ke[:, hs]), scores[h])
        m //= 2
    vb = vv.astype(BF16)
    o = jnp.concatenate([_dot(scores[h].astype(BF16), vb[:, hs]) for h, hs in enumerate(heads)], axis=1)
    return o, b


def _cast_blocks(in_refs, out_refs):
    for src, dst in zip(in_refs, out_refs):
        dst[...] = src[0].astype(BF16)


def _forget_bound(lbl_ref, layer):
    z = lbl_ref[...]
    z = z - jnp.max(z, axis=0, keepdims=True)
    ez = jnp.exp(z)
    p = ez / jnp.sum(ez, axis=0, keepdims=True)
    c = p[0:1]
    for r in range(1, layer + 1):
        c = c + p[r:r + 1]
    return c - p[0:1]


def _mixer_chunk_front(proj_scr, rows, alnw_ref, alnb_ref, ws_ref, bs_ref, lb, lv):
    u = proj_scr[rows, 0 * HALF:1 * HALF]
    v = proj_scr[rows, 1 * HALF:2 * HALF]
    q = proj_scr[rows, 2 * HALF:3 * HALF]
    f = proj_scr[rows, 3 * HALF:4 * HALF]
    ug = _gelu(u)
    vn = _layer_norm(_gelu(v), alnw_ref[...], alnb_ref[...])
    vnb = vn.astype(BF16)
    a_parts = []
    for h in range(N_HEADS):
        hs = slice(h * HEAD, (h + 1) * HEAD)
        w = jnp.where(lv >= 0, ws_ref[h], 0.0).astype(BF16)
        mixed = _dot(w, vnb[:, hs]) + bs_ref[:, h:h + 1]
        a_parts.append(ug[:, hs] * mixed)
    a_out = jnp.concatenate(a_parts, axis=1)
    fg = lb + (1.0 - lb) * jax.nn.sigmoid(f)
    return a_out, vn, _silu(q), 1.0 - fg, jnp.log(fg) * LOG2_E


def _rms_gate(o, bnw, g):
    parts = []
    for h in range(N_HEADS):
        hs = slice(h * HEAD, (h + 1) * HEAD)
        oh = o[:, hs]
        parts.append(oh * lax.rsqrt(jnp.mean(oh * oh, axis=-1, keepdims=True) + RMS_EPS) * bnw)
    return jnp.concatenate(parts, axis=1) * _silu(g)


def _mixer_prompt_kernel(x_ref, mod_ref, xn_ref, modn_ref, win_ref, wout_ref, alnw_ref, alnb_ref, ws_ref, bs_ref,
                         lbl_ref, bnw_ref, l1w_ref, l1b_ref, lv_ref, *rest, layer, tile, n_cast):
    cast_in, rest = rest[:n_cast], rest[n_cast:]
    (x1_ref, st_ref, vn_ref), rest = rest[:3], rest[3:]
    cast_out, (proj_a, proj_b, hn_scr, mix_scr, s_scr) = rest[:n_cast], rest[n_cast:]
    _cast_blocks(cast_in, cast_out)
    step = pl.program_id(1)
    lin = pl.program_id(0) * pl.num_programs(1) + step
    n_chunks = tile // CHUNK
    col_splits = [(IN_COLS * c // n_chunks, IN_COLS * (c + 1) // n_chunks) for c in range(n_chunks)]

    @pl.when(step == 0)
    def _():
        s_scr[...] = jnp.zeros_like(s_scr)

    mod = mod_ref[0, 0]
    x = x_ref[0]

    @pl.when(lin == 0)
    def _():
        proj_a[...] = _dot((x * (1.0 + mod[1:2]) + mod[0:1]).astype(BF16), win_ref[0])

    modn = modn_ref[0, 0]
    hn_scr[...] = (xn_ref[0] * (1.0 + modn[1:2]) + modn[0:1]).astype(BF16)
    lv = lv_ref[...]
    lb = _forget_bound(lbl_ref, layer)

    def chunk(c, proj_scr):
        rows = slice(c * CHUNK, (c + 1) * CHUNK)
        a_out, vn, qq, kk, gg = _mixer_chunk_front(proj_scr, rows, alnw_ref, alnb_ref, ws_ref, bs_ref, lb, lv)
        vn_ref[0] = vn
        vv = proj_scr[rows, 4 * HALF:5 * HALF]
        o_in, b = _hgrn_intra(qq, kk, gg, vv, lv, CHUNK)
        qh = (qq * jnp.exp2(b)).astype(BF16)
        b_last = b[CHUNK - 1:CHUNK, :]
        kdec = kk * jnp.exp2(b_last - b)
        e_last = jnp.exp2(b_last)
        o_parts = []
        for hh in range(N_HEADS):
            hs = slice(hh * HEAD, (hh + 1) * HEAD)
            s_old = s_scr[hh]
            o_parts.append(o_in[:, hs] + _dot(qh[:, hs], s_old.astype(BF16)))
            dec = jnp.broadcast_to(e_last[:, hs], (HEAD, HEAD)).T
            s_scr[hh] = dec * s_old + _dot(kdec[:, hs].T.astype(BF16), vv[:, hs].astype(BF16))
        g = proj_scr[rows, 5 * HALF:6 * HALF]
        b_out = _rms_gate(jnp.concatenate(o_parts, axis=1), bnw_ref[...], g)
        mix_scr[rows, 0:HALF] = a_out.astype(BF16)
        mix_scr[rows, HALF:2 * HALF] = b_out.astype(BF16)

    def run(proj_cur, proj_nxt):
        for c, (c0, c1) in enumerate(col_splits):
            proj_nxt[:, c0:c1] = _dot(hn_scr[...], win_ref[0, :, c0:c1])
            chunk(c, proj_cur)

    @pl.when(lin % 2 == 0)
    def _():
        run(proj_a, proj_b)

    @pl.when(lin % 2 == 1)
    def _():
        run(proj_b, proj_a)

    y = _dot(mix_scr[...], wout_ref[0])
    x1_ref[0] = _layer_norm(ALPHA * x + mod[2:3] * y, l1w_ref[...], l1b_ref[...])

    @pl.when(step == pl.num_programs(1) - 1)
    def _():
        st_ref[0] = s_scr[...]


def _mixer_sample_kernel(x_ref, mod_ref, s0_ref, win_ref, wout_ref, alnw_ref, alnb_ref, ws_ref, bs_ref,
                         lbl_ref, bnw_ref, l1w_ref, l1b_ref, lv_ref, st_in_ref,
                         x1_ref, st_ref, vn_ref,
                         proj_scr, qh_scr, kt_scr, vb_scr, el_scr, o_scr, *, layer, seq_len):
    del st_in_ref
    nseq = SAMPLE_SEQS
    x3 = x_ref[...]
    mod = mod_ref[0]
    h3 = x3 * (1.0 + mod[1][:, None, :]) + mod[0][:, None, :]
    proj_scr[...] = _dot(h3.reshape(CHUNK, D_MODEL).astype(BF16), win_ref[0])
    lv = lv_ref[...]
    lb = _forget_bound(lbl_ref, layer)
    rows = slice(0, CHUNK)
    a_out, vn, qq, kk, gg = _mixer_chunk_front(proj_scr, rows, alnw_ref, alnb_ref, ws_ref, bs_ref, lb, lv)
    vn_ref[...] = vn.reshape(nseq, seq_len, HALF)
    vv = proj_scr[rows, 4 * HALF:5 * HALF]
    o_in, b = _hgrn_intra(qq, kk, gg, vv, lv, seq_len)
    qh_scr[...] = qq * jnp.exp2(b)
    b3 = b.reshape(nseq, seq_len, HALF)
    b_last = jnp.broadcast_to(b3[:, seq_len - 1:seq_len, :], (nseq, seq_len, HALF))
    el_scr[...] = jnp.exp2(b_last)
    kdec = kk * jnp.exp2(b_last.reshape(CHUNK, HALF) - b)
    for hh in range(N_HEADS):
        hs = slice(hh * HEAD, (hh + 1) * HEAD)
        kt_scr[hh] = kdec[:, hs].T.astype(BF16)
    vb_scr[...] = vv.astype(BF16)
    row_seq = lax.broadcasted_iota(jnp.int32, (HEAD, CHUNK), 1) // seq_len

    def per_seq(j, carry):
        rws = pl.ds(pl.multiple_of(j * seq_len, seq_len), seq_len)
        own = row_seq == j
        el = el_scr[j]
        for hh in range(N_HEADS):
            hs = slice(hh * HEAD, (hh + 1) * HEAD)
            s_old = s0_ref[0, j, hh]
            o_scr[rws, hs] = _dot(qh_scr[rws, hs].astype(BF16), s_old.astype(BF16))
            dec = jnp.broadcast_to(el[0:1, hs], (HEAD, HEAD)).T
            kt = jnp.where(own, kt_scr[hh], jnp.zeros((), BF16))
            st_ref[0, j, hh] = dec * s_old + _dot(kt, vb_scr[:, hs])
        return carry

    lax.fori_loop(0, nseq, per_seq, 0)

    g = proj_scr[rows, 5 * HALF:6 * HALF]
    b_out = _rms_gate(o_in + o_scr[...], bnw_ref[...], g)
    mix = jnp.concatenate([a_out, b_out], axis=1).astype(BF16)
    y3 = _dot(mix, wout_ref[0]).reshape(nseq, seq_len, D_MODEL)
    x1_ref[...] = _layer_norm(ALPHA * x3 + mod[2][:, None, :] * y3, l1w_ref[...], l1b_ref[...])


def _mod_rows(mod_ref, per_seq, j):
    if per_seq:
        return mod_ref[0, j][:, None, :]
    return mod_ref[0, 0][j:j + 1][None]


def _swiglu(h, wg_ref, wu_ref, wd_ref):
    acc = None
    for f0, f1 in FF_SPLITS:
        act = (_silu(_dot(h, wg_ref[:, f0:f1])) * _dot(h, wu_ref[:, f0:f1])).astype(BF16)
        part = _dot(act, wd_ref[f0:f1, :])
        acc = part if acc is None else acc + part
    return acc


def _ffn_dense_kernel(x_ref, mod_ref, wg_ref, wu_ref, wd_ref, l2w_ref, l2b_ref, *rest, per_seq, n_cast):
    cast_in, o_ref, cast_out = rest[:n_cast], rest[n_cast], rest[n_cast + 1:]
    _cast_blocks(cast_in, cast_out)
    x3 = x_ref[...]
    groups, rows, _ = x3.shape
    h = (x3 * (1.0 + _mod_rows(mod_ref, per_seq, 4)) + _mod_rows(mod_ref, per_seq, 3))
    h = h.reshape(groups * rows, D_MODEL).astype(BF16)
    f3 = _swiglu(h, wg_ref.at[0], wu_ref.at[0], wd_ref.at[0]).reshape(groups, rows, D_MODEL)
    o_ref[...] = _layer_norm(ALPHA * x3 + _mod_rows(mod_ref, per_seq, 5) * f3, l2w_ref[...], l2b_ref[...])


def _moe_input(x_ref, mod_ref, per_seq):
    x3 = x_ref[...]
    groups, rows, _ = x3.shape
    h3 = x3 * (1.0 + _mod_rows(mod_ref, per_seq, 4)) + _mod_rows(mod_ref, per_seq, 3)
    return h3.reshape(groups * rows, D_MODEL)


def _router_kernel(x_ref, mod_ref, wr_ref, idx_ref, gate_ref, *, per_seq):
    h = _moe_input(x_ref, mod_ref, per_seq)
    n = h.shape[0]
    wr = wr_ref[...]
    h_hi, w_hi = h.astype(BF16), wr.astype(BF16)
    h_lo, w_lo = (h - h_hi.astype(F32)).astype(BF16), (wr - w_hi.astype(F32)).astype(BF16)
    logits = _dot(h_hi, w_hi) + (_dot(h_hi, w_lo) + _dot(h_lo, w_hi))
    z = jnp.exp(logits - jnp.max(logits, axis=-1, keepdims=True))
    p = z / jnp.sum(z, axis=-1, keepdims=True)
    lane = lax.broadcasted_iota(jnp.int32, p.shape, 1)
    p1 = jnp.max(p, axis=-1, keepdims=True)
    i1 = jnp.min(jnp.where(p == p1, lane, N_EXPERTS), axis=-1, keepdims=True)
    rest = jnp.where(lane == i1, -1.0, p)
    p2 = jnp.max(rest, axis=-1, keepdims=True)
    i2 = jnp.min(jnp.where(rest == p2, lane, N_EXPERTS), axis=-1, keepdims=True)
    two = lax.broadcasted_iota(jnp.int32, (n, 2), 1)
    idx_ref[...] = jnp.where(two == 0, i1, i2)
    gate_ref[...] = jnp.where(two == 0, p1, p2) / (p1 + p2)


def _for_each_row(n, fn):
    def body(i, carry):
        fn(i)
        return carry
    lax.fori_loop(0, n, body, 0, unroll=4)


def _lane_block(n, s):
    return pl.ds(s, n, stride=ROW_TILE)


def _tile_of_row(r):
    return pl.ds(pl.multiple_of(r * ROW_TILE, ROW_TILE), ROW_TILE)


def _dispatch_kernel(x_ref, mod_ref, da_ref, db_ref, hg_in_ref, hg_ref, rows_scr, sem, *, per_seq):
    del hg_in_ref
    h = _moe_input(x_ref, mod_ref, per_seq)
    n = h.shape[0]
    for s in range(ROW_TILE):
        rows_scr[_lane_block(n, s), :] = h[:, s * LANE:(s + 1) * LANE]

    def copy(i, slot_ref):
        return pltpu.make_async_copy(rows_scr.at[_tile_of_row(i)], hg_ref.at[_tile_of_row(slot_ref[0, 0, i])], sem)

    _for_each_row(n, lambda i: (copy(i, da_ref).start(priority=0), copy(i, db_ref).start(priority=1)))
    all_rows = pltpu.make_async_copy(rows_scr, hg_ref.at[pl.ds(0, n * ROW_TILE)], sem)
    all_rows.wait()
    all_rows.wait()


def _expert_kernel(te_ref, used_ref, hg_ref, wg_ref, wu_ref, wd_ref, o_ref, raw_scr, h_scr, sem):
    del te_ref
    i = pl.program_id(0)
    n = pl.num_programs(0)
    tile = h_scr.shape[1]
    cur, nxt = i % 2, (i + 1) % 2

    def fetch(t, slot):
        rows = pl.ds(pl.multiple_of(t * tile * ROW_TILE, tile * ROW_TILE), tile * ROW_TILE)
        return pltpu.make_async_copy(hg_ref.at[rows], raw_scr.at[slot], sem.at[slot])

    def rearrange(slot):
        for s in range(ROW_TILE):
            h_scr[slot, :, s * LANE:(s + 1) * LANE] = raw_scr[slot, _lane_block(tile, s), :].astype(BF16)

    @pl.when(i == 0)
    def _():
        fetch(0, 0).start()
        fetch(0, 0).wait()
        rearrange(0)
        fetch(1, 1).start()

    @pl.when(i + 1 < n)
    def _():
        fetch(i + 1, nxt).wait()

    @pl.when(i + 2 < n)
    def _():
        fetch(i + 2, cur).start()

    @pl.when(i < used_ref[0])
    def _():
        rearrange(nxt)
        y = _swiglu(h_scr[cur], wg_ref.at[0], wu_ref.at[0], wd_ref.at[0])
        for s in range(ROW_TILE):
            o_ref[_lane_block(tile, s), :] = y[:, s * LANE:(s + 1) * LANE]

    @pl.when(i >= used_ref[0])
    def _():
        o_ref[...] = jnp.zeros_like(o_ref)


def _combine_kernel(x_ref, mod_ref, gate_ref, da_ref, db_ref, yo_ref, l2w_ref, l2b_ref, o_ref,
                    ya_scr, yb_scr, f_scr, sem, *, per_seq):
    x3 = x_ref[...]
    groups, rows, _ = x3.shape
    n = groups * rows

    def copy(i, slot_ref, dst):
        return pltpu.make_async_copy(yo_ref.at[_tile_of_row(slot_ref[0, 0, i])], dst.at[_tile_of_row(i)], sem)

    _for_each_row(n, lambda i: (copy(i, da_ref, ya_scr).start(priority=0),
                                copy(i, db_ref, yb_scr).start(priority=1)))
    for dst in (ya_scr, yb_scr):
        pltpu.make_async_copy(yo_ref.at[pl.ds(0, n * ROW_TILE)], dst, sem).wait()
    ga, gb = gate_ref[:, 0:1], gate_ref[:, 1:2]
    for s in range(ROW_TILE):
        f_scr[:, s * LANE:(s + 1) * LANE] = ga * ya_scr[_lane_block(n, s), :] + gb * yb_scr[_lane_block(n, s), :]
    f3 = f_scr[...].reshape(groups, rows, D_MODEL)
    o_ref[...] = _layer_norm(ALPHA * x3 + _mod_rows(mod_ref, per_seq, 5) * f3, l2w_ref[...], l2b_ref[...])


def _params(*semantics):
    return pltpu.CompilerParams(dimension_semantics=semantics, vmem_limit_bytes=VMEM_LIMIT_BYTES)


def _const_spec(shape):
    return pl.BlockSpec(shape, lambda *_: (0,) * len(shape), pipeline_mode=pl.Buffered(1))


def _adaln(c_all, w_ada, b_ada):
    nb = c_all.shape[0]
    tn = 1536

    def body(c_ref, w_ref, b_ref, o_ref):
        c = c_ref[...]
        o_ref[0] = _dot(_silu(c).astype(BF16), w_ref[0].astype(BF16)) + b_ref[0]

    return pl.pallas_call(
        body,
        out_shape=jax.ShapeDtypeStruct((DEPTH, nb, 6 * D_MODEL), F32),
        grid=(DEPTH, 6 * D_MODEL // tn),
        in_specs=[pl.BlockSpec((nb, D_MODEL), lambda l, j: (0, 0)),
                  pl.BlockSpec((1, D_MODEL, tn), lambda l, j: (l, 0, j)),
                  pl.BlockSpec((1, 1, tn), lambda l, j: (l, 0, j))],
        out_specs=pl.BlockSpec((1, nb, tn), lambda l, j: (l, 0, j)),
        compiler_params=_params("arbitrary", "arbitrary"),
        name="adaln_modulation",
    )(c_all, w_ada, b_ada.reshape(DEPTH, 1, 6 * D_MODEL))


def _cast_bf16(w):
    rows, cols = w.shape[-2:]
    w3 = w.reshape(-1, rows, cols)

    def body(w_ref, o_ref):
        o_ref[...] = w_ref[...].astype(BF16)

    out = pl.pallas_call(
        body,
        out_shape=jax.ShapeDtypeStruct(w3.shape, BF16),
        grid=(w3.shape[0],),
        in_specs=[pl.BlockSpec((1, rows, cols), lambda g: (g, 0, 0))],
        out_specs=pl.BlockSpec((1, rows, cols), lambda g: (g, 0, 0)),
        compiler_params=_params("arbitrary"),
        name="cast_bf16",
    )(w3)
    return out.reshape(w.shape)


def _cast_plan(weights, moe_layer, n_steps, step_of):
    in_specs, out_specs, out_shapes = [], [], []
    for w in weights:
        _, n_exp, rows, cols = w.shape
        assert n_steps % n_exp == 0, (n_steps, n_exp)
        parts = n_steps // n_exp
        rb = rows // parts
        assert rb * parts == rows and rb % 16 == 0
        in_specs.append(pl.BlockSpec(
            (1, 1, rb, cols), lambda *g, parts=parts: (moe_layer, step_of(*g) // parts, step_of(*g) % parts, 0)))
        out_specs.append(pl.BlockSpec(
            (1, rb, cols), lambda *g, parts=parts: (step_of(*g) // parts, step_of(*g) % parts, 0)))
        out_shapes.append(jax.ShapeDtypeStruct((n_exp, rows, cols), BF16))
    return in_specs, out_specs, out_shapes


def _layer_spec(shape, layer):
    return pl.BlockSpec((1,) + shape, lambda *_: (layer,) + (0,) * len(shape), pipeline_mode=pl.Buffered(1))


def _mixer_weight_specs(layer):
    return [_layer_spec((D_MODEL, IN_COLS), layer), _layer_spec((D_MODEL, D_MODEL), layer),
            _const_spec((1, HALF)), _const_spec((1, HALF)),
            _const_spec((N_HEADS, CHUNK, CHUNK)), _const_spec((CHUNK, N_HEADS)),
            _const_spec((DEPTH, HALF)), _const_spec((1, HEAD)),
            _const_spec((1, D_MODEL)), _const_spec((1, D_MODEL)), _const_spec((CHUNK, CHUNK))]


def _mixer_prompt(layer, x, mod_p, wts, lv, cast=(), cast_layer=0):
    batch, seq, _ = x.shape
    tile = PROMPT_TILE
    rows_out = seq - CHUNK * ((seq - 1) // CHUNK)
    assert seq % tile == 0 and rows_out == CHUNK
    per = seq // tile

    def nxt(b, s):
        return jnp.minimum(b * per + s + 1, batch * per - 1)

    c_in, c_out, c_shapes = _cast_plan(cast, cast_layer, batch * per, lambda b, s: b * per + s)
    return pl.pallas_call(
        functools.partial(_mixer_prompt_kernel, layer=layer, tile=tile, n_cast=len(cast)),
        out_shape=(jax.ShapeDtypeStruct((batch, seq, D_MODEL), F32),
                   jax.ShapeDtypeStruct((batch, N_HEADS, HEAD, HEAD), F32),
                   jax.ShapeDtypeStruct((batch, CHUNK, HALF), F32), *c_shapes),
        grid=(batch, per),
        in_specs=[pl.BlockSpec((1, tile, D_MODEL), lambda b, s: (b, s, 0)),
                  pl.BlockSpec((1, 1, 6, D_MODEL), lambda b, s: (layer, b, 0, 0)),
                  pl.BlockSpec((1, tile, D_MODEL), lambda b, s: (nxt(b, s) // per, nxt(b, s) % per, 0)),
                  pl.BlockSpec((1, 1, 6, D_MODEL), lambda b, s: (layer, nxt(b, s) // per, 0, 0))]
        + _mixer_weight_specs(layer) + c_in,
        out_specs=(pl.BlockSpec((1, tile, D_MODEL), lambda b, s: (b, s, 0)),
                   pl.BlockSpec((1, N_HEADS, HEAD, HEAD), lambda b, s: (b, 0, 0, 0)),
                   pl.BlockSpec((1, CHUNK, HALF), lambda b, s: (b, 0, 0)), *c_out),
        scratch_shapes=[pltpu.VMEM((tile, IN_COLS), F32), pltpu.VMEM((tile, IN_COLS), F32),
                        pltpu.VMEM((tile, D_MODEL), BF16), pltpu.VMEM((tile, D_MODEL), BF16),
                        pltpu.VMEM((N_HEADS, HEAD, HEAD), F32)],
        compiler_params=_params("arbitrary", "arbitrary"),
        name="token_mixer_prompt",
    )(x, mod_p, x, mod_p, *wts, lv, *cast)


def _mixer_sample(layer, x, mod_s, state, wts, lv, states_so_far):
    nseq_all, seq_len, _ = x.shape
    nseq = SAMPLE_SEQS
    assert nseq * seq_len == CHUNK and nseq_all % nseq == 0 and seq_len == SUBLANE
    in_specs = [pl.BlockSpec((nseq, seq_len, D_MODEL), lambda j: (j, 0, 0)),
                pl.BlockSpec((1, 6, nseq, D_MODEL), lambda j: (layer, 0, j, 0)),
                pl.BlockSpec((1, nseq, N_HEADS, HEAD, HEAD), lambda j: (layer, j, 0, 0, 0))]
    in_specs += _mixer_weight_specs(layer) + [pl.BlockSpec(memory_space=pl.ANY)]
    operands = (x, mod_s, state, *wts, lv, states_so_far)
    aliases = {len(operands) - 1: 1}
    return pl.pallas_call(
        functools.partial(_mixer_sample_kernel, layer=layer, seq_len=seq_len),
        out_shape=(jax.ShapeDtypeStruct((nseq_all, seq_len, D_MODEL), F32),
                   jax.ShapeDtypeStruct(state.shape, F32),
                   jax.ShapeDtypeStruct((nseq_all, seq_len, HALF), F32)),
        grid=(nseq_all // nseq,),
        in_specs=in_specs,
        out_specs=(pl.BlockSpec((nseq, seq_len, D_MODEL), lambda j: (j, 0, 0)),
                   pl.BlockSpec((1, nseq, N_HEADS, HEAD, HEAD), lambda j: (layer, j, 0, 0, 0)),
                   pl.BlockSpec((nseq, seq_len, HALF), lambda j: (j, 0, 0))),
        input_output_aliases=aliases,
        scratch_shapes=[pltpu.VMEM((CHUNK, IN_COLS), F32), pltpu.VMEM((CHUNK, HALF), F32),
                        pltpu.VMEM((N_HEADS, HEAD, CHUNK), BF16), pltpu.VMEM((CHUNK, HALF), BF16),
                        pltpu.VMEM((nseq, seq_len, HALF), F32), pltpu.VMEM((CHUNK, HALF), F32)],
        compiler_params=_params("arbitrary"),
        name="token_mixer_sample",
    )(*operands)


def _row_blocking(x, per_seq, tile):
    batch, seq, _ = x.shape
    if per_seq:
        groups = tile // seq
        assert batch % groups == 0
        grid = (batch // groups,)
        x_spec = pl.BlockSpec((groups, seq, D_MODEL), lambda i: (i, 0, 0))
        return grid, x_spec, groups, lambda layer: pl.BlockSpec((1, 6, groups, D_MODEL), lambda i: (layer, 0, i, 0))
    assert seq % tile == 0
    per = seq // tile
    grid = (batch * per,)
    x_spec = pl.BlockSpec((1, tile, D_MODEL), lambda i: (i // per, i % per, 0))
    return grid, x_spec, per, lambda layer: pl.BlockSpec((1, 1, 6, D_MODEL), lambda i: (layer, i // per, 0, 0))


def _ffn_dense(layer, x, mod, per_seq, wg, wu, wd, l2w, l2b, cast=(), cast_layer=0):
    tile = FFN_TILE if not per_seq else CHUNK
    grid, x_spec, _, mod_spec = _row_blocking(x, per_seq, tile)
    c_in, c_out, c_shapes = _cast_plan(cast, cast_layer, grid[0], lambda i: i)
    return pl.pallas_call(
        functools.partial(_ffn_dense_kernel, per_seq=per_seq, n_cast=len(cast)),
        out_shape=(jax.ShapeDtypeStruct(x.shape, F32), *c_shapes),
        grid=grid,
        in_specs=[x_spec, mod_spec(layer),
                  _layer_spec((D_MODEL, D_FF), layer // 2), _layer_spec((D_MODEL, D_FF), layer // 2),
                  _layer_spec((D_FF, D_MODEL), layer // 2),
                  _const_spec((1, D_MODEL)), _const_spec((1, D_MODEL))] + c_in,
        out_specs=(x_spec, *c_out),
        compiler_params=_params("arbitrary"),
        name="ffn_dense",
    )(x, mod, wg, wu, wd, l2w, l2b, *cast)


def _router(layer, x, mod, per_seq, w_router):
    batch, seq, _ = x.shape
    tile = FFN_TILE if not per_seq else CHUNK
    grid, x_spec, _, mod_spec = _row_blocking(x, per_seq, tile)
    n = batch * seq
    return pl.pallas_call(
        functools.partial(_router_kernel, per_seq=per_seq),
        out_shape=(jax.ShapeDtypeStruct((n, 2), jnp.int32), jax.ShapeDtypeStruct((n, 2), F32)),
        grid=grid,
        in_specs=[x_spec, mod_spec(layer), _const_spec((D_MODEL, N_EXPERTS))],
        out_specs=(pl.BlockSpec((tile, 2), lambda i: (i, 0)), pl.BlockSpec((tile, 2), lambda i: (i, 0))),
        compiler_params=_params("arbitrary"),
        name="moe_router",
    )(x, mod, w_router)


def _slot_spec(tile):
    return pl.BlockSpec((1, 1, tile), lambda i: (i, 0, 0), memory_space=pltpu.SMEM)


def _dispatch(layer, x, mod, per_seq, slot_a, slot_b, hg):
    tile = FFN_TILE if not per_seq else CHUNK
    grid, x_spec, _, mod_spec = _row_blocking(x, per_seq, tile)
    return pl.pallas_call(
        functools.partial(_dispatch_kernel, per_seq=per_seq),
        out_shape=jax.ShapeDtypeStruct(hg.shape, hg.dtype),
        grid=grid,
        in_specs=[x_spec, mod_spec(layer), _slot_spec(tile), _slot_spec(tile), pl.BlockSpec(memory_space=pl.ANY)],
        out_specs=pl.BlockSpec(memory_space=pl.ANY),
        scratch_shapes=[pltpu.VMEM((tile * ROW_TILE, LANE), F32), pltpu.SemaphoreType.DMA(())],
        input_output_aliases={4: 0},
        compiler_params=_params("arbitrary"),
        name="moe_dispatch",
    )(x, mod, slot_a.reshape(-1, 1, tile), slot_b.reshape(-1, 1, tile), hg)


def _experts(hg, tile_expert, tiles_used, wg, wu, wd):
    mp = hg.shape[0] // ROW_TILE
    tile = EXPERT_TILE
    assert mp % tile == 0 and mp // tile >= 2
    return pl.pallas_call(
        _expert_kernel,
        out_shape=jax.ShapeDtypeStruct(hg.shape, F32),
        grid_spec=pltpu.PrefetchScalarGridSpec(
            num_scalar_prefetch=2,
            grid=(mp // tile,),
            in_specs=[pl.BlockSpec(memory_space=pl.ANY),
                      pl.BlockSpec((1, D_MODEL, D_FF), lambda i, te, used: (te[i], 0, 0)),
                      pl.BlockSpec((1, D_MODEL, D_FF), lambda i, te, used: (te[i], 0, 0)),
                      pl.BlockSpec((1, D_FF, D_MODEL), lambda i, te, used: (te[i], 0, 0))],
            out_specs=pl.BlockSpec((tile * ROW_TILE, LANE), lambda i, te, used: (i, 0)),
            scratch_shapes=[pltpu.VMEM((2, tile * ROW_TILE, LANE), F32), pltpu.VMEM((2, tile, D_MODEL), BF16),
                            pltpu.SemaphoreType.DMA((2,))]),
        compiler_params=_params("arbitrary"),
        name="moe_experts",
    )(tile_expert, tiles_used, hg, wg, wu, wd)


def _combine(layer, x, mod, per_seq, gate, slot_a, slot_b, yo, l2w, l2b):
    tile = FFN_TILE if not per_seq else CHUNK
    grid, x_spec, _, mod_spec = _row_blocking(x, per_seq, tile)
    return pl.pallas_call(
        functools.partial(_combine_kernel, per_seq=per_seq),
        out_shape=jax.ShapeDtypeStruct(x.shape, F32),
        grid=grid,
        in_specs=[x_spec, mod_spec(layer), pl.BlockSpec((tile, 2), lambda i: (i, 0)),
                  _slot_spec(tile), _slot_spec(tile), pl.BlockSpec(memory_space=pl.ANY),
                  _const_spec((1, D_MODEL)), _const_spec((1, D_MODEL))],
        out_specs=x_spec,
        scratch_shapes=[pltpu.VMEM((tile * ROW_TILE, LANE), F32), pltpu.VMEM((tile * ROW_TILE, LANE), F32),
                        pltpu.VMEM((tile, D_MODEL), F32), pltpu.SemaphoreType.DMA(())],
        compiler_params=_params("arbitrary"),
        name="moe_combine",
    )(x, mod, gate, slot_a.reshape(-1, 1, tile), slot_b.reshape(-1, 1, tile), yo, l2w, l2b)


def _routing_tables(idx, n_pad_rows):
    e_flat = jnp.concatenate([idx[:, 0], idx[:, 1]])
    onehot = (e_flat[:, None] == jnp.arange(N_EXPERTS, dtype=jnp.int32)[None, :]).astype(jnp.int32)
    csum = jnp.cumsum(onehot, axis=0)
    padded = ((csum[-1] + EXPERT_TILE - 1) // EXPERT_TILE) * EXPERT_TILE
    pend = jnp.cumsum(padded)
    slot = jnp.sum(onehot * (csum - 1 + (pend - padded)[None, :]), axis=1).astype(jnp.int32)
    tile_start = jnp.arange(n_pad_rows // EXPERT_TILE, dtype=jnp.int32) * EXPERT_TILE
    tile_expert = jnp.sum((tile_start[:, None] >= pend[None, :]).astype(jnp.int32), axis=1)
    tiles_used = (pend[-1:] // EXPERT_TILE).astype(jnp.int32)
    return slot, jnp.minimum(tile_expert, N_EXPERTS - 1).astype(jnp.int32), tiles_used


def _round_up(a, b):
    return (a + b - 1) // b * b


def kernel(x_prompt, x_sample, state_hgrn, c_prompt, c_sample, w_ada, b_ada, w_in, w_out, a_ln_w, a_ln_b, a_ws, a_bs, lb_logits, b_norm_w, ln1_w, ln1_b, ln2_w, ln2_b, w_ff_gate, w_ff_up, w_ff_down, w_router, e_gate, e_up, e_down):
    batch, seq, _ = x_prompt.shape
    nseq, seq_len, _ = x_sample.shape
    n_prompt, n_sample = batch * seq, nseq * seq_len
    n_tok = n_prompt + n_sample

    mod = _adaln(jnp.concatenate([c_prompt, c_sample], axis=0), w_ada, b_ada)
    mod_p = mod[:, :batch].reshape(DEPTH, batch, 6, D_MODEL)
    mod_s = mod[:, batch:].reshape(DEPTH, nseq, 6, D_MODEL).transpose(0, 2, 1, 3)

    lv_p = jnp.asarray(_level_ids(CHUNK))
    lv_s = jnp.asarray(_level_ids(seq_len))
    reps = CHUNK // seq_len
    n_pad_rows = _round_up(2 * n_tok, EXPERT_TILE) + N_EXPERTS * EXPERT_TILE

    w_in_b, w_out_b = _cast_bf16(w_in), _cast_bf16(w_out)
    ff_b = tuple(_cast_bf16(w) for w in (w_ff_gate, w_ff_up, w_ff_down))
    assert DEPTH % 2 == 0

    xp, xs = x_prompt, x_sample
    st_p, st_s, cv_p, cv_s = [], jnp.zeros_like(state_hgrn), [], []
    for l in range(DEPTH):
        shared = (a_ln_w[l][None], a_ln_b[l][None])
        tail = (lb_logits, b_norm_w[l][None], ln1_w[l][None], ln1_b[l][None])
        wts_p = (w_in_b, w_out_b) + shared + (a_ws[l], a_bs[l].T) + tail
        ws_s = jnp.tile(a_ws[l][:, :seq_len, :seq_len], (1, reps, reps))
        bs_s = jnp.tile(a_bs[l][:, :seq_len].T, (reps, 1))
        wts_s = (w_in_b, w_out_b) + shared + (ws_s, bs_s) + tail
        if l % 2 == 0:
            xp, sp, vp = _mixer_prompt(l, xp, mod_p, wts_p, lv_p)
        else:
            xp, sp, vp, ed_b = _mixer_prompt(l, xp, mod_p, wts_p, lv_p, (e_down,), l // 2)
        xs, st_s, vs = _mixer_sample(l, xs, mod_s, state_hgrn, wts_s, lv_s, st_s)
        st_p.append(sp), cv_p.append(vp), cv_s.append(vs)
        l2w, l2b = ln2_w[l][None], ln2_b[l][None]
        if l % 2 == 0:
            xp, eg_b, eu_b = _ffn_dense(l, xp, mod_p, False, *ff_b, l2w, l2b, (e_gate, e_up), l // 2)
            xs, = _ffn_dense(l, xs, mod_s, True, *ff_b, l2w, l2b)
        else:
            wr = w_router[l // 2]
            ip, gp = _router(l, xp, mod_p, False, wr)
            is_, gs = _router(l, xs, mod_s, True, wr)
            slot, tile_expert, tiles_used = _routing_tables(jnp.concatenate([ip, is_], axis=0), n_pad_rows)
            sa_p, sa_s = slot[:n_prompt], slot[n_prompt:n_tok]
            sb_p, sb_s = slot[n_tok:n_tok + n_prompt], slot[n_tok + n_prompt:]
            hg = jnp.zeros((n_pad_rows * ROW_TILE, LANE), F32)
            hg = _dispatch(l, xp, mod_p, False, sa_p, sb_p, hg)
            hg = _dispatch(l, xs, mod_s, True, sa_s, sb_s, hg)
            yo = _experts(hg, tile_expert, tiles_used, eg_b, eu_b, ed_b)
            xp = _combine(l, xp, mod_p, False, gp, sa_p, sb_p, yo, l2w, l2b)
            xs = _combine(l, xs, mod_s, True, gs, sa_s, sb_s, yo, l2w, l2b)
    return (xp, xs, jnp.stack(st_p), st_s, jnp.stack(cv_p), jnp.stack(cv_s))
```

```python
import functools
import math

import numpy as np
import jax
import jax.numpy as jnp
from jax import lax
from jax.experimental import pallas as pl
from jax.experimental.pallas import tpu as pltpu

F32 = jnp.float32
BF16 = jnp.bfloat16

D_MODEL = 1024
DEPTH = 4
HALF = 512
N_HEADS = 4
HEAD = 128
CHUNK = 128
IN_COLS = 6 * HALF
D_FF = 2816
N_EXPERTS = 8
ALPHA = (2.0 * DEPTH) ** 0.25
LN_EPS = 1e-5
RMS_EPS = 1e-6
LOG2_E = math.log2(math.e)

VMEM_LIMIT_BYTES = 56 * 1024 * 1024
LANE = 128
SUBLANE = 8
ROW_TILE = D_MODEL // LANE
assert ROW_TILE == SUBLANE

PROMPT_TILE = 512
SAMPLE_SEQS = 16
FFN_TILE = 512
EXPERT_TILE = 256
TAIL_WINDOWS = N_EXPERTS + 1
FF_SPLITS = ((0, 1024), (1024, 2048), (2048, D_FF))


def _dot(a, b):
    return jnp.dot(a, b, preferred_element_type=F32)


def _dot_nt(a, b):
    return lax.dot_general(a, b, (((1,), (1,)), ((), ())), preferred_element_type=F32)


def _gelu(x):
    return 0.5 * x * (1.0 + lax.erf(x * (1.0 / math.sqrt(2.0))))


def _silu(x):
    return x * jax.nn.sigmoid(x)


def _layer_norm(z, w, b):
    mu = jnp.mean(z, axis=-1, keepdims=True)
    zc = z - mu
    var = jnp.mean(zc * zc, axis=-1, keepdims=True)
    return zc * lax.rsqrt(var + LN_EPS) * w + b


def _level_ids(block):
    t = np.arange(CHUNK)[:, None]
    s = np.arange(CHUNK)[None, :]
    x = t ^ s
    lv = np.where(x == 0, 0, np.floor(np.log2(np.maximum(x, 1))).astype(np.int64) + 1)
    ok = (s <= t) & (x < block)
    return np.where(ok, lv, -1).astype(np.int32)


def _reference_rows(b, m):
    rows, width = b.shape
    two_m = 2 * m
    if two_m >= SUBLANE:
        nb = rows // two_m
        b3 = b.reshape(nb, two_m, width)
        r = jnp.broadcast_to(b3[:, m - 1:m, :], (nb, two_m, width))
        return r.reshape(rows, width)
    t = lax.broadcasted_iota(jnp.int32, (rows, width), 0)
    tm = t & (two_m - 1)
    down1 = pltpu.roll(b, 1, 0)
    if m == 1:
        return jnp.where(tm == 0, b, down1)
    up1 = pltpu.roll(b, rows - 1, 0)
    down2 = pltpu.roll(b, 2, 0)
    return jnp.where(tm == 0, up1, jnp.where(tm == 1, b, jnp.where(tm == 2, down1, down2)))


def _hgrn_intra(qq, kk, gg, vv, lv, block):
    cm = (lv >= 0).astype(BF16)
    g_hi = gg.astype(BF16)
    rem = gg - g_hi.astype(F32)
    g_mid = rem.astype(BF16)
    g_lo = (rem - g_mid.astype(F32)).astype(BF16)
    b = _dot(cm, g_hi) + _dot(cm, g_mid) + _dot(cm, g_lo)

    heads = [slice(h * HEAD, (h + 1) * HEAD) for h in range(N_HEADS)]
    qb = qq.astype(BF16)
    kb = kk.astype(BF16)
    scores = [jnp.where(lv == 0, _dot_nt(qb[:, hs], kb[:, hs]), 0.0) for hs in heads]
    m = block // 2
    while m >= 1:
        level = int(math.log2(m)) + 1
        e = jnp.exp2(-jnp.abs(b - _reference_rows(b, m)))
        qe = (qq * e).astype(BF16)
        ke = (kk * e).astype(BF16)
        for h, hs in enumerate(heads):
            scores[h] = jnp.where(lv == level, _dot_nt(qe[:, hs], ke[:, hs]), scores[h])
        m //= 2
    vb = vv.astype(BF16)
    o = jnp.concatenate([_dot(scores[h].astype(BF16), vb[:, hs]) for h, hs in enumerate(heads)], axis=1)
    return o, b


def _cast_blocks(in_refs, out_refs):
    for src, dst in zip(in_refs, out_refs):
        dst[...] = src[0].astype(BF16)


def _forget_bound(lbl_ref, layer):
    z = lbl_ref[...]
    z = z - jnp.max(z, axis=0, keepdims=True)
    ez = jnp.exp(z)
    p = ez / jnp.sum(ez, axis=0, keepdims=True)
    c = p[0:1]
    for r in range(1, layer + 1):
        c = c + p[r:r + 1]
    return c - p[0:1]


def _mixer_chunk_front(proj_scr, rows, alnw_ref, alnb_ref, ws_ref, bs_ref, lb, lv):
    u = proj_scr[rows, 0 * HALF:1 * HALF]
    v = proj_scr[rows, 1 * HALF:2 * HALF]
    q = proj_scr[rows, 2 * HALF:3 * HALF]
    f = proj_scr[rows, 3 * HALF:4 * HALF]
    ug = _gelu(u)
    vn = _layer_norm(_gelu(v), alnw_ref[...], alnb_ref[...])
    vnb = vn.astype(BF16)
    a_parts = []
    for h in range(N_HEADS):
        hs = slice(h * HEAD, (h + 1) * HEAD)
        w = jnp.where(lv >= 0, ws_ref[h], 0.0).astype(BF16)
        mixed = _dot(w, vnb[:, hs]) + bs_ref[:, h:h + 1]
        a_parts.append(ug[:, hs] * mixed)
    a_out = jnp.concatenate(a_parts, axis=1)
    fg = lb + (1.0 - lb) * jax.nn.sigmoid(f)
    return a_out, vn, _silu(q), 1.0 - fg, jnp.log(fg) * LOG2_E


def _rms_gate(o, bnw, g):
    parts = []
    for h in range(N_HEADS):
        hs = slice(h * HEAD, (h + 1) * HEAD)
        oh = o[:, hs]
        parts.append(oh * lax.rsqrt(jnp.mean(oh * oh, axis=-1, keepdims=True) + RMS_EPS) * bnw)
    return jnp.concatenate(parts, axis=1) * _silu(g)


def _mixer_prompt_kernel(x_ref, mod_ref, xn_ref, modn_ref, win_ref, wout_ref, alnw_ref, alnb_ref, ws_ref, bs_ref,
                         lbl_ref, bnw_ref, l1w_ref, l1b_ref, lv_ref, *rest, layer, tile, n_cast):
    cast_in, rest = rest[:n_cast], rest[n_cast:]
    (x1_ref, st_ref, vn_ref), rest = rest[:3], rest[3:]
    cast_out, (proj_a, proj_b, hn_scr, mix_scr, s_scr) = rest[:n_cast], rest[n_cast:]
    _cast_blocks(cast_in, cast_out)
    step = pl.program_id(1)
    lin = pl.program_id(0) * pl.num_programs(1) + step
    n_chunks = tile // CHUNK
    col_splits = [(IN_COLS * c // n_chunks, IN_COLS * (c + 1) // n_chunks) for c in range(n_chunks)]

    @pl.when(step == 0)
    def _():
        s_scr[...] = jnp.zeros_like(s_scr)

    mod = mod_ref[0, 0]
    x = x_ref[0]

    @pl.when(lin == 0)
    def _():
        proj_a[...] = _dot((x * (1.0 + mod[1:2]) + mod[0:1]).astype(BF16), win_ref[0])

    modn = modn_ref[0, 0]
    hn_scr[...] = (xn_ref[0] * (1.0 + modn[1:2]) + modn[0:1]).astype(BF16)
    lv = lv_ref[...]
    lb = _forget_bound(lbl_ref, layer)

    def chunk(c, proj_scr):
        rows = slice(c * CHUNK, (c + 1) * CHUNK)
        a_out, vn, qq, kk, gg = _mixer_chunk_front(proj_scr, rows, alnw_ref, alnb_ref, ws_ref, bs_ref, lb, lv)
        vn_ref[0] = vn
        vv = proj_scr[rows, 4 * HALF:5 * HALF]
        o_in, b = _hgrn_intra(qq, kk, gg, vv, lv, CHUNK)
        qh = (qq * jnp.exp2(b)).astype(BF16)
        b_last = b[CHUNK - 1:CHUNK, :]
        kdec = kk * jnp.exp2(b_last - b)
        e_last = jnp.exp2(b_last)
        o_parts = []
        for hh in range(N_HEADS):
            hs = slice(hh * HEAD, (hh + 1) * HEAD)
            s_old = s_scr[hh]
            o_parts.append(o_in[:, hs] + _dot(qh[:, hs], s_old.astype(BF16)))
            dec = jnp.broadcast_to(e_last[:, hs], (HEAD, HEAD)).T
            s_scr[hh] = dec * s_old + _dot(kdec[:, hs].T.astype(BF16), vv[:, hs].astype(BF16))
        g = proj_scr[rows, 5 * HALF:6 * HALF]
        b_out = _rms_gate(jnp.concatenate(o_parts, axis=1), bnw_ref[...], g)
        mix_scr[rows, 0:HALF] = a_out.astype(BF16)
        mix_scr[rows, HALF:2 * HALF] = b_out.astype(BF16)

    def run(proj_cur, proj_nxt):
        for c, (c0, c1) in enumerate(col_splits):
            proj_nxt[:, c0:c1] = _dot(hn_scr[...], win_ref[0, :, c0:c1])
            chunk(c, proj_cur)

    @pl.when(lin % 2 == 0)
    def _():
        run(proj_a, proj_b)

    @pl.when(lin % 2 == 1)
    def _():
        run(proj_b, proj_a)

    y = _dot(mix_scr[...], wout_ref[0])
    x1_ref[0] = _layer_norm(ALPHA * x + mod[2:3] * y, l1w_ref[...], l1b_ref[...])

    @pl.when(step == pl.num_programs(1) - 1)
    def _():
        st_ref[0] = s_scr[...]


def _mixer_sample_kernel(x_ref, mod_ref, s0_ref, win_ref, wout_ref, alnw_ref, alnb_ref, ws_ref, bs_ref,
                         lbl_ref, bnw_ref, l1w_ref, l1b_ref, lv_ref, st_in_ref,
                         x1_ref, st_ref, vn_ref,
                         proj_scr, qh_scr, kt_scr, vb_scr, el_scr, o_scr, *, layer, seq_len):
    del st_in_ref
    nseq = SAMPLE_SEQS
    x3 = x_ref[...]
    mod = mod_ref[0]
    h3 = x3 * (1.0 + mod[1][:, None, :]) + mod[0][:, None, :]
    proj_scr[...] = _dot(h3.reshape(CHUNK, D_MODEL).astype(BF16), win_ref[0])
    lv = lv_ref[...]
    lb = _forget_bound(lbl_ref, layer)
    rows = slice(0, CHUNK)
    a_out, vn, qq, kk, gg = _mixer_chunk_front(proj_scr, rows, alnw_ref, alnb_ref, ws_ref, bs_ref, lb, lv)
    vn_ref[...] = vn.reshape(nseq, seq_len, HALF)
    vv = proj_scr[rows, 4 * HALF:5 * HALF]
    o_in, b = _hgrn_intra(qq, kk, gg, vv, lv, seq_len)
    qh_scr[...] = qq * jnp.exp2(b)
    b3 = b.reshape(nseq, seq_len, HALF)
    b_last = jnp.broadcast_to(b3[:, seq_len - 1:seq_len, :], (nseq, seq_len, HALF))
    el_scr[...] = jnp.exp2(b_last)
    kdec = kk * jnp.exp2(b_last.reshape(CHUNK, HALF) - b)
    for hh in range(N_HEADS):
        hs = slice(hh * HEAD, (hh + 1) * HEAD)
        kt_scr[hh] = kdec[:, hs].T.astype(BF16)
    vb_scr[...] = vv.astype(BF16)
    row_seq = lax.broadcasted_iota(jnp.int32, (HEAD, CHUNK), 1) // seq_len

    def per_seq(j, carry):
        rws = pl.ds(pl.multiple_of(j * seq_len, seq_len), seq_len)
        own = row_seq == j
        el = el_scr[j]
        for hh in range(N_HEADS):
            hs = slice(hh * HEAD, (hh + 1) * HEAD)
            s_old = s0_ref[0, j, hh]
            o_scr[rws, hs] = _dot(qh_scr[rws, hs].astype(BF16), s_old.astype(BF16))
            dec = jnp.broadcast_to(el[0:1, hs], (HEAD, HEAD)).T
            kt = jnp.where(own, kt_scr[hh], jnp.zeros((), BF16))
            st_ref[0, j, hh] = dec * s_old + _dot(kt, vb_scr[:, hs])
        return carry

    lax.fori_loop(0, nseq, per_seq, 0)

    g = proj_scr[rows, 5 * HALF:6 * HALF]
    b_out = _rms_gate(o_in + o_scr[...], bnw_ref[...], g)
    mix = jnp.concatenate([a_out, b_out], axis=1).astype(BF16)
    y3 = _dot(mix, wout_ref[0]).reshape(nseq, seq_len, D_MODEL)
    x1_ref[...] = _layer_norm(ALPHA * x3 + mod[2][:, None, :] * y3, l1w_ref[...], l1b_ref[...])


def _mod_rows(mod_ref, per_seq, j):
    if per_seq:
        return mod_ref[0, j][:, None, :]
    return mod_ref[0, 0][j:j + 1][None]


def _swiglu(h, wg_ref, wu_ref, wd_ref):
    acc = None
    for f0, f1 in FF_SPLITS:
        act = (_silu(_dot(h, wg_ref[:, f0:f1])) * _dot(h, wu_ref[:, f0:f1])).astype(BF16)
        part = _dot(act, wd_ref[f0:f1, :])
        acc = part if acc is None else acc + part
    return acc


def _ffn_dense_kernel(x_ref, mod_ref, wg_ref, wu_ref, wd_ref, l2w_ref, l2b_ref, *rest, per_seq, n_cast):
    cast_in, o_ref, cast_out = rest[:n_cast], rest[n_cast], rest[n_cast + 1:]
    _cast_blocks(cast_in, cast_out)
    x3 = x_ref[...]
    groups, rows, _ = x3.shape
    h = (x3 * (1.0 + _mod_rows(mod_ref, per_seq, 4)) + _mod_rows(mod_ref, per_seq, 3))
    h = h.reshape(groups * rows, D_MODEL).astype(BF16)
    f3 = _swiglu(h, wg_ref.at[0], wu_ref.at[0], wd_ref.at[0]).reshape(groups, rows, D_MODEL)
    o_ref[...] = _layer_norm(ALPHA * x3 + _mod_rows(mod_ref, per_seq, 5) * f3, l2w_ref[...], l2b_ref[...])


def _moe_input(x_ref, mod_ref, per_seq):
    x3 = x_ref[...]
    groups, rows, _ = x3.shape
    h3 = x3 * (1.0 + _mod_rows(mod_ref, per_seq, 4)) + _mod_rows(mod_ref, per_seq, 3)
    return h3.reshape(groups * rows, D_MODEL)


def _router_kernel(x_ref, mod_ref, wr_ref, idx_ref, gate_ref, *, per_seq):
    h = _moe_input(x_ref, mod_ref, per_seq)
    n = h.shape[0]
    wr = wr_ref[...]
    h_hi, w_hi = h.astype(BF16), wr.astype(BF16)
    h_lo, w_lo = (h - h_hi.astype(F32)).astype(BF16), (wr - w_hi.astype(F32)).astype(BF16)
    logits = _dot(h_hi, w_hi) + (_dot(h_hi, w_lo) + _dot(h_lo, w_hi))
    z = jnp.exp(logits - jnp.max(logits, axis=-1, keepdims=True))
    p = z / jnp.sum(z, axis=-1, keepdims=True)
    lane = lax.broadcasted_iota(jnp.int32, p.shape, 1)
    p1 = jnp.max(p, axis=-1, keepdims=True)
    i1 = jnp.min(jnp.where(p == p1, lane, N_EXPERTS), axis=-1, keepdims=True)
    rest = jnp.where(lane == i1, -1.0, p)
    p2 = jnp.max(rest, axis=-1, keepdims=True)
    i2 = jnp.min(jnp.where(rest == p2, lane, N_EXPERTS), axis=-1, keepdims=True)
    two = lax.broadcasted_iota(jnp.int32, (n, 2), 1)
    idx_ref[...] = jnp.where(two == 0, i1, i2)
    gate_ref[...] = jnp.where(two == 0, p1, p2) / (p1 + p2)


def _for_each_row(n, fn):
    def body(i, carry):
        fn(i)
        return carry
    lax.fori_loop(0, n, body, 0, unroll=4)


def _lane_block(n, s):
    return pl.ds(s, n, stride=ROW_TILE)


def _tile_of_row(r):
    return pl.ds(pl.multiple_of(r * ROW_TILE, ROW_TILE), ROW_TILE)


def _dispatch_kernel(xp_ref, modp_ref, xs_ref, mods_ref, da_ref, db_ref, win_ref, hg_ref, rows_scr, sem, *,
                     prompt_steps):
    step = pl.program_id(0)
    zero_rows = EXPERT_TILE * ROW_TILE

    def zero_fill(w):
        start = pl.multiple_of(win_ref[0, w] * ROW_TILE, ROW_TILE)
        return pltpu.make_async_copy(rows_scr.at[pl.ds(0, zero_rows)], hg_ref.at[pl.ds(start, zero_rows)], sem)

    @pl.when(step == 0)
    def _():
        rows_scr[pl.ds(0, zero_rows), :] = jnp.zeros((zero_rows, LANE), F32)
        for w in range(win_ref.shape[1]):
            zero_fill(w).start()
            zero_fill(w).wait()

    n = da_ref.shape[2]

    def stage(x_ref, mod_ref, per_seq):
        h = _moe_input(x_ref, mod_ref, per_seq)
        for s in range(ROW_TILE):
            rows_scr[_lane_block(n, s), :] = h[:, s * LANE:(s + 1) * LANE]

    @pl.when(step < prompt_steps)
    def _():
        stage(xp_ref, modp_ref, False)

    @pl.when(step >= prompt_steps)
    def _():
        stage(xs_ref, mods_ref, True)

    def copy(i, slot_ref):
        return pltpu.make_async_copy(rows_scr.at[_tile_of_row(i)], hg_ref.at[_tile_of_row(slot_ref[0, 0, i])], sem)

    _for_each_row(n, lambda i: (copy(i, da_ref).start(priority=0), copy(i, db_ref).start(priority=1)))
    all_rows = pltpu.make_async_copy(rows_scr.at[pl.ds(0, n * ROW_TILE)], hg_ref.at[pl.ds(0, n * ROW_TILE)], sem)
    all_rows.wait()
    all_rows.wait()


def _expert_kernel(te_ref, used_ref, hg_ref, wg_ref, wu_ref, wd_ref, o_ref, raw_scr, h_scr, sem):
    del te_ref
    i = pl.program_id(0)
    tile = h_scr.shape[1]
    cur, nxt = i % 2, (i + 1) % 2

    def fetch(t, slot):
        rows = pl.ds(pl.multiple_of(t * tile * ROW_TILE, tile * ROW_TILE), tile * ROW_TILE)
        return pltpu.make_async_copy(hg_ref.at[rows], raw_scr.at[slot], sem.at[slot])

    def rearrange(slot):
        for s in range(ROW_TILE):
            h_scr[slot, :, s * LANE:(s + 1) * LANE] = raw_scr[slot, _lane_block(tile, s), :].astype(BF16)

    used = used_ref[0]

    @pl.when(i == 0)
    def _():
        fetch(0, 0).start()
        raw_scr[1] = jnp.zeros(raw_scr.shape[1:], F32)
        fetch(0, 0).wait()
        rearrange(0)

        @pl.when(1 < used)
        def _():
            fetch(1, 1).start()

    @pl.when(i + 1 < used)
    def _():
        fetch(i + 1, nxt).wait()

    @pl.when(i + 2 < used)
    def _():
        fetch(i + 2, cur).start()

    @pl.when(i < used)
    def _():
        rearrange(nxt)
        y = _swiglu(h_scr[cur], wg_ref.at[0], wu_ref.at[0], wd_ref.at[0])
        for s in range(ROW_TILE):
            o_ref[_lane_block(tile, s), :] = y[:, s * LANE:(s + 1) * LANE]

    @pl.when(i >= used)
    def _():
        o_ref[...] = jnp.zeros_like(o_ref)


def _combine_kernel(x_ref, mod_ref, gate_ref, da_ref, db_ref, yo_ref, l2w_ref, l2b_ref, o_ref,
                    ya_scr, yb_scr, f_scr, sem, *, per_seq):
    x3 = x_ref[...]
    groups, rows, _ = x3.shape
    n = groups * rows

    def copy(i, slot_ref, dst):
        return pltpu.make_async_copy(yo_ref.at[_tile_of_row(slot_ref[0, 0, i])], dst.at[_tile_of_row(i)], sem)

    _for_each_row(n, lambda i: (copy(i, da_ref, ya_scr).start(priority=0),
                                copy(i, db_ref, yb_scr).start(priority=1)))
    for dst in (ya_scr, yb_scr):
        pltpu.make_async_copy(yo_ref.at[pl.ds(0, n * ROW_TILE)], dst, sem).wait()
    ga, gb = gate_ref[:, 0:1], gate_ref[:, 1:2]
    for s in range(ROW_TILE):
        f_scr[:, s * LANE:(s + 1) * LANE] = ga * ya_scr[_lane_block(n, s), :] + gb * yb_scr[_lane_block(n, s), :]
    f3 = f_scr[...].reshape(groups, rows, D_MODEL)
    o_ref[...] = _layer_norm(ALPHA * x3 + _mod_rows(mod_ref, per_seq, 5) * f3, l2w_ref[...], l2b_ref[...])


def _params(*semantics):
    return pltpu.CompilerParams(dimension_semantics=semantics, vmem_limit_bytes=VMEM_LIMIT_BYTES)


def _const_spec(shape):
    return pl.BlockSpec(shape, lambda *_: (0,) * len(shape), pipeline_mode=pl.Buffered(1))


def _adaln(c_all, w_ada, b_ada):
    nb = c_all.shape[0]
    tn = 1536

    def body(c_ref, w_ref, b_ref, o_ref):
        c = c_ref[...]
        o_ref[0] = _dot(_silu(c).astype(BF16), w_ref[0].astype(BF16)) + b_ref[0]

    return pl.pallas_call(
        body,
        out_shape=jax.ShapeDtypeStruct((DEPTH, nb, 6 * D_MODEL), F32),
        grid=(DEPTH, 6 * D_MODEL // tn),
        in_specs=[pl.BlockSpec((nb, D_MODEL), lambda l, j: (0, 0)),
                  pl.BlockSpec((1, D_MODEL, tn), lambda l, j: (l, 0, j)),
                  pl.BlockSpec((1, 1, tn), lambda l, j: (l, 0, j))],
        out_specs=pl.BlockSpec((1, nb, tn), lambda l, j: (l, 0, j)),
        compiler_params=_params("arbitrary", "arbitrary"),
        name="adaln_modulation",
    )(c_all, w_ada, b_ada.reshape(DEPTH, 1, 6 * D_MODEL))


def _cast_bf16(w):
    rows, cols = w.shape[-2:]
    w3 = w.reshape(-1, rows, cols)

    def body(w_ref, o_ref):
        o_ref[...] = w_ref[...].astype(BF16)

    out = pl.pallas_call(
        body,
        out_shape=jax.ShapeDtypeStruct(w3.shape, BF16),
        grid=(w3.shape[0],),
        in_specs=[pl.BlockSpec((1, rows, cols), lambda g: (g, 0, 0))],
        out_specs=pl.BlockSpec((1, rows, cols), lambda g: (g, 0, 0)),
        compiler_params=_params("arbitrary"),
        name="cast_bf16",
    )(w3)
    return out.reshape(w.shape)


def _cast_plan(weights, moe_layer, n_steps, step_of):
    in_specs, out_specs, out_shapes = [], [], []
    for w in weights:
        _, n_exp, rows, cols = w.shape
        assert n_steps % n_exp == 0, (n_steps, n_exp)
        parts = n_steps // n_exp
        rb = rows // parts
        assert rb * parts == rows and rb % 16 == 0
        in_specs.append(pl.BlockSpec(
            (1, 1, rb, cols), lambda *g, parts=parts: (moe_layer, step_of(*g) // parts, step_of(*g) % parts, 0)))
        out_specs.append(pl.BlockSpec(
            (1, rb, cols), lambda *g, parts=parts: (step_of(*g) // parts, step_of(*g) % parts, 0)))
        out_shapes.append(jax.ShapeDtypeStruct((n_exp, rows, cols), BF16))
    return in_specs, out_specs, out_shapes


def _layer_spec(shape, layer):
    return pl.BlockSpec((1,) + shape, lambda *_: (layer,) + (0,) * len(shape), pipeline_mode=pl.Buffered(1))


def _mixer_weight_specs(layer):
    return [_layer_spec((D_MODEL, IN_COLS), layer), _layer_spec((D_MODEL, D_MODEL), layer),
            _const_spec((1, HALF)), _const_spec((1, HALF)),
            _const_spec((N_HEADS, CHUNK, CHUNK)), _const_spec((CHUNK, N_HEADS)),
            _const_spec((DEPTH, HALF)), _const_spec((1, HEAD)),
            _const_spec((1, D_MODEL)), _const_spec((1, D_MODEL)), _const_spec((CHUNK, CHUNK))]


def _mixer_prompt(layer, x, mod_p, wts, lv, cast=(), cast_layer=0):
    batch, seq, _ = x.shape
    tile = PROMPT_TILE
    rows_out = seq - CHUNK * ((seq - 1) // CHUNK)
    assert seq % tile == 0 and rows_out == CHUNK
    per = seq // tile

    def nxt(b, s):
        return jnp.minimum(b * per + s + 1, batch * per - 1)

    c_in, c_out, c_shapes = _cast_plan(cast, cast_layer, batch * per, lambda b, s: b * per + s)
    return pl.pallas_call(
        functools.partial(_mixer_prompt_kernel, layer=layer, tile=tile, n_cast=len(cast)),
        out_shape=(jax.ShapeDtypeStruct((batch, seq, D_MODEL), F32),
                   jax.ShapeDtypeStruct((batch, N_HEADS, HEAD, HEAD), F32),
                   jax.ShapeDtypeStruct((batch, CHUNK, HALF), F32), *c_shapes),
        grid=(batch, per),
        in_specs=[pl.BlockSpec((1, tile, D_MODEL), lambda b, s: (b, s, 0)),
                  pl.BlockSpec((1, 1, 6, D_MODEL), lambda b, s: (layer, b, 0, 0)),
                  pl.BlockSpec((1, tile, D_MODEL), lambda b, s: (nxt(b, s) // per, nxt(b, s) % per, 0)),
                  pl.BlockSpec((1, 1, 6, D_MODEL), lambda b, s: (layer, nxt(b, s) // per, 0, 0))]
        + _mixer_weight_specs(layer) + c_in,
        out_specs=(pl.BlockSpec((1, tile, D_MODEL), lambda b, s: (b, s, 0)),
                   pl.BlockSpec((1, N_HEADS, HEAD, HEAD), lambda b, s: (b, 0, 0, 0)),
                   pl.BlockSpec((1, CHUNK, HALF), lambda b, s: (b, 0, 0)), *c_out),
        scratch_shapes=[pltpu.VMEM((tile, IN_COLS), F32), pltpu.VMEM((tile, IN_COLS), F32),
                        pltpu.VMEM((tile, D_MODEL), BF16), pltpu.VMEM((tile, D_MODEL), BF16),
                        pltpu.VMEM((N_HEADS, HEAD, HEAD), F32)],
        compiler_params=_params("arbitrary", "arbitrary"),
        name="token_mixer_prompt",
    )(x, mod_p, x, mod_p, *wts, lv, *cast)


def _mixer_sample(layer, x, mod_s, state, wts, lv, states_so_far):
    nseq_all, seq_len, _ = x.shape
    nseq = SAMPLE_SEQS
    assert nseq * seq_len == CHUNK and nseq_all % nseq == 0 and seq_len == SUBLANE
    in_specs = [pl.BlockSpec((nseq, seq_len, D_MODEL), lambda j: (j, 0, 0)),
                pl.BlockSpec((1, 6, nseq, D_MODEL), lambda j: (layer, 0, j, 0)),
                pl.BlockSpec((1, nseq, N_HEADS, HEAD, HEAD), lambda j: (layer, j, 0, 0, 0))]
    in_specs += _mixer_weight_specs(layer) + [pl.BlockSpec(memory_space=pl.ANY)]
    operands = (x, mod_s, state, *wts, lv, states_so_far)
    aliases = {len(operands) - 1: 1}
    return pl.pallas_call(
        functools.partial(_mixer_sample_kernel, layer=layer, seq_len=seq_len),
        out_shape=(jax.ShapeDtypeStruct((nseq_all, seq_len, D_MODEL), F32),
                   jax.ShapeDtypeStruct(state.shape, F32),
                   jax.ShapeDtypeStruct((nseq_all, seq_len, HALF), F32)),
        grid=(nseq_all // nseq,),
        in_specs=in_specs,
        out_specs=(pl.BlockSpec((nseq, seq_len, D_MODEL), lambda j: (j, 0, 0)),
                   pl.BlockSpec((1, nseq, N_HEADS, HEAD, HEAD), lambda j: (layer, j, 0, 0, 0)),
                   pl.BlockSpec((nseq, seq_len, HALF), lambda j: (j, 0, 0))),
        input_output_aliases=aliases,
        scratch_shapes=[pltpu.VMEM((CHUNK, IN_COLS), F32), pltpu.VMEM((CHUNK, HALF), F32),
                        pltpu.VMEM((N_HEADS, HEAD, CHUNK), BF16), pltpu.VMEM((CHUNK, HALF), BF16),
                        pltpu.VMEM((nseq, seq_len, HALF), F32), pltpu.VMEM((CHUNK, HALF), F32)],
        compiler_params=_params("arbitrary"),
        name="token_mixer_sample",
    )(*operands)


def _row_blocking(x, per_seq, tile):
    batch, seq, _ = x.shape
    if per_seq:
        groups = tile // seq
        assert batch % groups == 0
        grid = (batch // groups,)
        x_spec = pl.BlockSpec((groups, seq, D_MODEL), lambda i: (i, 0, 0))
        return grid, x_spec, groups, lambda layer: pl.BlockSpec((1, 6, groups, D_MODEL), lambda i: (layer, 0, i, 0))
    assert seq % tile == 0
    per = seq // tile
    grid = (batch * per,)
    x_spec = pl.BlockSpec((1, tile, D_MODEL), lambda i: (i // per, i % per, 0))
    return grid, x_spec, per, lambda layer: pl.BlockSpec((1, 1, 6, D_MODEL), lambda i: (layer, i // per, 0, 0))


def _ffn_dense(layer, x, mod, per_seq, wg, wu, wd, l2w, l2b, cast=(), cast_layer=0):
    tile = FFN_TILE if not per_seq else CHUNK
    grid, x_spec, _, mod_spec = _row_blocking(x, per_seq, tile)
    c_in, c_out, c_shapes = _cast_plan(cast, cast_layer, grid[0], lambda i: i)
    return pl.pallas_call(
        functools.partial(_ffn_dense_kernel, per_seq=per_seq, n_cast=len(cast)),
        out_shape=(jax.ShapeDtypeStruct(x.shape, F32), *c_shapes),
        grid=grid,
        in_specs=[x_spec, mod_spec(layer),
                  _layer_spec((D_MODEL, D_FF), layer // 2), _layer_spec((D_MODEL, D_FF), layer // 2),
                  _layer_spec((D_FF, D_MODEL), layer // 2),
                  _const_spec((1, D_MODEL)), _const_spec((1, D_MODEL))] + c_in,
        out_specs=(x_spec, *c_out),
        compiler_params=_params("arbitrary"),
        name="ffn_dense",
    )(x, mod, wg, wu, wd, l2w, l2b, *cast)


def _router(layer, x, mod, per_seq, w_router):
    batch, seq, _ = x.shape
    tile = FFN_TILE if not per_seq else CHUNK
    grid, x_spec, _, mod_spec = _row_blocking(x, per_seq, tile)
    n = batch * seq
    return pl.pallas_call(
        functools.partial(_router_kernel, per_seq=per_seq),
        out_shape=(jax.ShapeDtypeStruct((n, 2), jnp.int32), jax.ShapeDtypeStruct((n, 2), F32)),
        grid=grid,
        in_specs=[x_spec, mod_spec(layer), _const_spec((D_MODEL, N_EXPERTS))],
        out_specs=(pl.BlockSpec((tile, 2), lambda i: (i, 0)), pl.BlockSpec((tile, 2), lambda i: (i, 0))),
        compiler_params=_params("arbitrary"),
        name="moe_router",
    )(x, mod, w_router)


def _slot_spec(tile):
    return pl.BlockSpec((1, 1, tile), lambda i: (i, 0, 0), memory_space=pltpu.SMEM)


def _dispatch(layer, xp, mod_p, xs, mod_s, slot_a, slot_b, zero_windows, n_rows):
    n_prompt, n_sample = xp.shape[0] * xp.shape[1], xs.shape[0] * xs.shape[1]
    tile = math.gcd(FFN_TILE, n_sample)
    _, xp_spec, _, modp_spec = _row_blocking(xp, False, tile)
    _, xs_spec, _, mods_spec = _row_blocking(xs, True, tile)
    p_steps, s_steps = n_prompt // tile, n_sample // tile

    def first(spec):
        return pl.BlockSpec(spec.block_shape, lambda i: spec.index_map(jnp.minimum(i, p_steps - 1)))

    def second(spec):
        return pl.BlockSpec(spec.block_shape, lambda i: spec.index_map(jnp.maximum(i - p_steps, 0)))

    return pl.pallas_call(
        functools.partial(_dispatch_kernel, prompt_steps=p_steps),
        out_shape=jax.ShapeDtypeStruct((n_rows * ROW_TILE, LANE), F32),
        grid=(p_steps + s_steps,),
        in_specs=[first(xp_spec), first(modp_spec(layer)), second(xs_spec), second(mods_spec(layer)),
                  _slot_spec(tile), _slot_spec(tile), pl.BlockSpec(memory_space=pltpu.SMEM)],
        out_specs=pl.BlockSpec(memory_space=pl.ANY),
        scratch_shapes=[pltpu.VMEM((max(tile, EXPERT_TILE) * ROW_TILE, LANE), F32), pltpu.SemaphoreType.DMA(())],
        compiler_params=_params("arbitrary"),
        name="moe_dispatch",
    )(xp, mod_p, xs, mod_s, slot_a.reshape(-1, 1, tile), slot_b.reshape(-1, 1, tile), zero_windows)


def _experts(hg, mp, tile_expert, tiles_used, wg, wu, wd):
    tile = EXPERT_TILE
    assert mp % tile == 0 and mp // tile >= 2 and hg.shape[0] >= mp * ROW_TILE
    return pl.pallas_call(
        _expert_kernel,
        out_shape=jax.ShapeDtypeStruct((mp * ROW_TILE, LANE), F32),
        grid_spec=pltpu.PrefetchScalarGridSpec(
            num_scalar_prefetch=2,
            grid=(mp // tile,),
            in_specs=[pl.BlockSpec(memory_space=pl.ANY),
                      pl.BlockSpec((1, D_MODEL, D_FF), lambda i, te, used: (te[i], 0, 0)),
                      pl.BlockSpec((1, D_MODEL, D_FF), lambda i, te, used: (te[i], 0, 0)),
                      pl.BlockSpec((1, D_FF, D_MODEL), lambda i, te, used: (te[i], 0, 0))],
            out_specs=pl.BlockSpec((tile * ROW_TILE, LANE), lambda i, te, used: (i, 0)),
            scratch_shapes=[pltpu.VMEM((2, tile * ROW_TILE, LANE), F32), pltpu.VMEM((2, tile, D_MODEL), BF16),
                            pltpu.SemaphoreType.DMA((2,))]),
        compiler_params=_params("arbitrary"),
        name="moe_experts",
    )(tile_expert, tiles_used, hg, wg, wu, wd)


def _combine(layer, x, mod, per_seq, gate, slot_a, slot_b, yo, l2w, l2b):
    tile = FFN_TILE if not per_seq else CHUNK
    grid, x_spec, _, mod_spec = _row_blocking(x, per_seq, tile)
    return pl.pallas_call(
        functools.partial(_combine_kernel, per_seq=per_seq),
        out_shape=jax.ShapeDtypeStruct(x.shape, F32),
        grid=grid,
        in_specs=[x_spec, mod_spec(layer), pl.BlockSpec((tile, 2), lambda i: (i, 0)),
                  _slot_spec(tile), _slot_spec(tile), pl.BlockSpec(memory_space=pl.ANY),
                  _const_spec((1, D_MODEL)), _const_spec((1, D_MODEL))],
        out_specs=x_spec,
        scratch_shapes=[pltpu.VMEM((tile * ROW_TILE, LANE), F32), pltpu.VMEM((tile * ROW_TILE, LANE), F32),
                        pltpu.VMEM((tile, D_MODEL), F32), pltpu.SemaphoreType.DMA(())],
        compiler_params=_params("arbitrary"),
        name="moe_combine",
    )(x, mod, gate, slot_a.reshape(-1, 1, tile), slot_b.reshape(-1, 1, tile), yo, l2w, l2b)


def _routing_tables(idx, n_pad_rows):
    e_flat = jnp.concatenate([idx[:, 0], idx[:, 1]])
    onehot = (e_flat[:, None] == jnp.arange(N_EXPERTS, dtype=jnp.int32)[None, :]).astype(jnp.int32)
    csum = jnp.cumsum(onehot, axis=0)
    padded = ((csum[-1] + EXPERT_TILE - 1) // EXPERT_TILE) * EXPERT_TILE
    pend = jnp.cumsum(padded)
    slot = jnp.sum(onehot * (csum - 1 + (pend - padded)[None, :]), axis=1).astype(jnp.int32)
    tile_start = jnp.arange(n_pad_rows // EXPERT_TILE, dtype=jnp.int32) * EXPERT_TILE
    tile_expert = jnp.sum((tile_start[:, None] >= pend[None, :]).astype(jnp.int32), axis=1)
    tiles_used = (pend[-1:] // EXPERT_TILE).astype(jnp.int32)
    pad_start = pend - padded + csum[-1]
    tail = jnp.minimum(pend[-1] + tile_start[:TAIL_WINDOWS], n_pad_rows)
    zero_windows = jnp.concatenate([pad_start, tail]).astype(jnp.int32)[None, :]
    return slot, jnp.minimum(tile_expert, N_EXPERTS - 1).astype(jnp.int32), tiles_used, zero_windows


def _round_up(a, b):
    return (a + b - 1) // b * b


def kernel(x_prompt, x_sample, state_hgrn, c_prompt, c_sample, w_ada, b_ada, w_in, w_out, a_ln_w, a_ln_b, a_ws, a_bs, lb_logits, b_norm_w, ln1_w, ln1_b, ln2_w, ln2_b, w_ff_gate, w_ff_up, w_ff_down, w_router, e_gate, e_up, e_down):
    batch, seq, _ = x_prompt.shape
    nseq, seq_len, _ = x_sample.shape
    n_prompt, n_sample = batch * seq, nseq * seq_len
    n_tok = n_prompt + n_sample

    mod = _adaln(jnp.concatenate([c_prompt, c_sample], axis=0), w_ada, b_ada)
    mod_p = mod[:, :batch].reshape(DEPTH, batch, 6, D_MODEL)
    mod_s = mod[:, batch:].reshape(DEPTH, nseq, 6, D_MODEL).transpose(0, 2, 1, 3)

    lv_p = jnp.asarray(_level_ids(CHUNK))
    lv_s = jnp.asarray(_level_ids(seq_len))
    reps = CHUNK // seq_len
    n_pad_rows = _round_up(2 * n_tok, EXPERT_TILE) + N_EXPERTS * EXPERT_TILE

    w_in_b, w_out_b = _cast_bf16(w_in), _cast_bf16(w_out)
    ff_b = tuple(_cast_bf16(w) for w in (w_ff_gate, w_ff_up, w_ff_down))
    assert DEPTH % 2 == 0

    xp, xs = x_prompt, x_sample
    st_p, st_s, cv_p, cv_s = [], jnp.zeros_like(state_hgrn), [], []
    for l in range(DEPTH):
        shared = (a_ln_w[l][None], a_ln_b[l][None])
        tail = (lb_logits, b_norm_w[l][None], ln1_w[l][None], ln1_b[l][None])
        wts_p = (w_in_b, w_out_b) + shared + (a_ws[l], a_bs[l].T) + tail
        ws_s = jnp.tile(a_ws[l][:, :seq_len, :seq_len], (1, reps, reps))
        bs_s = jnp.tile(a_bs[l][:, :seq_len].T, (reps, 1))
        wts_s = (w_in_b, w_out_b) + shared + (ws_s, bs_s) + tail
        if l % 2 == 0:
            xp, sp, vp = _mixer_prompt(l, xp, mod_p, wts_p, lv_p)
        else:
            xp, sp, vp, ed_b = _mixer_prompt(l, xp, mod_p, wts_p, lv_p, (e_down,), l // 2)
        xs, st_s, vs = _mixer_sample(l, xs, mod_s, state_hgrn, wts_s, lv_s, st_s)
        st_p.append(sp), cv_p.append(vp), cv_s.append(vs)
        l2w, l2b = ln2_w[l][None], ln2_b[l][None]
        if l % 2 == 0:
            xp, eg_b, eu_b = _ffn_dense(l, xp, mod_p, False, *ff_b, l2w, l2b, (e_gate, e_up), l // 2)
            xs, = _ffn_dense(l, xs, mod_s, True, *ff_b, l2w, l2b)
        else:
            wr = w_router[l // 2]
            ip, gp = _router(l, xp, mod_p, False, wr)
            is_, gs = _router(l, xs, mod_s, True, wr)
            slot, tile_expert, tiles_used, zero_windows = _routing_tables(
                jnp.concatenate([ip, is_], axis=0), n_pad_rows)
            sa_p, sa_s = slot[:n_prompt], slot[n_prompt:n_tok]
            sb_p, sb_s = slot[n_tok:n_tok + n_prompt], slot[n_tok + n_prompt:]
            hg = _dispatch(l, xp, mod_p, xs, mod_s, slot[:n_tok], slot[n_tok:], zero_windows,
                           n_pad_rows + EXPERT_TILE)
            yo = _experts(hg, n_pad_rows, tile_expert, tiles_used, eg_b, eu_b, ed_b)
            xp = _combine(l, xp, mod_p, False, gp, sa_p, sb_p, yo, l2w, l2b)
            xs = _combine(l, xs, mod_s, True, gs, sa_s, sb_s, yo, l2w, l2b)
    return (xp, xs, jnp.stack(st_p), st_s, jnp.stack(cv_p), jnp.stack(cv_s))
```

```python
import functools
import math

import numpy as np
import jax
import jax.numpy as jnp
from jax import lax
from jax.experimental import pallas as pl
from jax.experimental.pallas import tpu as pltpu

F32 = jnp.float32
BF16 = jnp.bfloat16

D_MODEL = 1024
DEPTH = 4
HALF = 512
N_HEADS = 4
HEAD = 128
CHUNK = 128
IN_COLS = 6 * HALF
D_FF = 2816
N_EXPERTS = 8
ALPHA = (2.0 * DEPTH) ** 0.25
LN_EPS = 1e-5
RMS_EPS = 1e-6
LOG2_E = math.log2(math.e)

VMEM_LIMIT_BYTES = 56 * 1024 * 1024
LANE = 128
SUBLANE = 8
ROW_TILE = D_MODEL // LANE
assert ROW_TILE == SUBLANE

PROMPT_TILE = 512
SAMPLE_SEQS = 16
FFN_TILE = 512
EXPERT_TILE = 256
TAIL_WINDOWS = N_EXPERTS + 1
FF_SPLITS = ((0, 1024), (1024, 2048), (2048, D_FF))


def _dot(a, b):
    return jnp.dot(a, b, preferred_element_type=F32)


def _dot_nt(a, b):
    return lax.dot_general(a, b, (((1,), (1,)), ((), ())), preferred_element_type=F32)


def _gelu(x):
    return 0.5 * x * (1.0 + lax.erf(x * (1.0 / math.sqrt(2.0))))


def _silu(x):
    return x * jax.nn.sigmoid(x)


def _layer_norm(z, w, b):
    mu = jnp.mean(z, axis=-1, keepdims=True)
    zc = z - mu
    var = jnp.mean(zc * zc, axis=-1, keepdims=True)
    return zc * lax.rsqrt(var + LN_EPS) * w + b


def _level_ids(block):
    t = np.arange(CHUNK)[:, None]
    s = np.arange(CHUNK)[None, :]
    x = t ^ s
    lv = np.where(x == 0, 0, np.floor(np.log2(np.maximum(x, 1))).astype(np.int64) + 1)
    ok = (s <= t) & (x < block)
    return np.where(ok, lv, -1).astype(np.int32)


def _reference_rows(b, m):
    rows, width = b.shape
    two_m = 2 * m
    if two_m >= SUBLANE:
        nb = rows // two_m
        b3 = b.reshape(nb, two_m, width)
        r = jnp.broadcast_to(b3[:, m - 1:m, :], (nb, two_m, width))
        return r.reshape(rows, width)
    t = lax.broadcasted_iota(jnp.int32, (rows, width), 0)
    tm = t & (two_m - 1)
    down1 = pltpu.roll(b, 1, 0)
    if m == 1:
        return jnp.where(tm == 0, b, down1)
    up1 = pltpu.roll(b, rows - 1, 0)
    down2 = pltpu.roll(b, 2, 0)
    return jnp.where(tm == 0, up1, jnp.where(tm == 1, b, jnp.where(tm == 2, down1, down2)))


def _hgrn_intra(qq, kk, gg, vv, lv, block):
    cm = (lv >= 0).astype(BF16)
    g_hi = gg.astype(BF16)
    rem = gg - g_hi.astype(F32)
    g_mid = rem.astype(BF16)
    g_lo = (rem - g_mid.astype(F32)).astype(BF16)
    b = _dot(cm, g_hi) + _dot(cm, g_mid) + _dot(cm, g_lo)

    heads = [slice(h * HEAD, (h + 1) * HEAD) for h in range(N_HEADS)]
    qb = qq.astype(BF16)
    kb = kk.astype(BF16)
    scores = [jnp.where(lv == 0, _dot_nt(qb[:, hs], kb[:, hs]), 0.0) for hs in heads]
    m = block // 2
    while m >= 1:
        level = int(math.log2(m)) + 1
        e = jnp.exp2(-jnp.abs(b - _reference_rows(b, m)))
        qe = (qq * e).astype(BF16)
        ke = (kk * e).astype(BF16)
        for h, hs in enumerate(heads):
            scores[h] = jnp.where(lv == level, _dot_nt(qe[:, hs], ke[:, hs]), scores[h])
        m //= 2
    vb = vv.astype(BF16)
    o = jnp.concatenate([_dot(scores[h].astype(BF16), vb[:, hs]) for h, hs in enumerate(heads)], axis=1)
    return o, b


def _cast_blocks(in_refs, out_refs):
    for src, dst in zip(in_refs, out_refs):
        dst[...] = src[0].astype(BF16)


def _forget_bound(lbl_ref, layer):
    z = lbl_ref[...]
    z = z - jnp.max(z, axis=0, keepdims=True)
    ez = jnp.exp(z)
    p = ez / jnp.sum(ez, axis=0, keepdims=True)
    c = p[0:1]
    for r in range(1, layer + 1):
        c = c + p[r:r + 1]
    return c - p[0:1]


def _mixer_chunk_front(proj_scr, rows, alnw_ref, alnb_ref, ws_ref, bs_ref, lb, lv):
    u = proj_scr[rows, 0 * HALF:1 * HALF]
    v = proj_scr[rows, 1 * HALF:2 * HALF]
    q = proj_scr[rows, 2 * HALF:3 * HALF]
    f = proj_scr[rows, 3 * HALF:4 * HALF]
    ug = _gelu(u)
    vn = _layer_norm(_gelu(v), alnw_ref[...], alnb_ref[...])
    vnb = vn.astype(BF16)
    a_parts = []
    for h in range(N_HEADS):
        hs = slice(h * HEAD, (h + 1) * HEAD)
        w = jnp.where(lv >= 0, ws_ref[h], 0.0).astype(BF16)
        mixed = _dot(w, vnb[:, hs]) + bs_ref[:, h:h + 1]
        a_parts.append(ug[:, hs] * mixed)
    a_out = jnp.concatenate(a_parts, axis=1)
    fg = lb + (1.0 - lb) * jax.nn.sigmoid(f)
    return a_out, vn, _silu(q), 1.0 - fg, jnp.log(fg) * LOG2_E


def _rms_gate(o, bnw, g):
    parts = []
    for h in range(N_HEADS):
        hs = slice(h * HEAD, (h + 1) * HEAD)
        oh = o[:, hs]
        parts.append(oh * lax.rsqrt(jnp.mean(oh * oh, axis=-1, keepdims=True) + RMS_EPS) * bnw)
    return jnp.concatenate(parts, axis=1) * _silu(g)


def _mixer_prompt_kernel(x_ref, mod_ref, xn_ref, modn_ref, win_ref, wout_ref, alnw_ref, alnb_ref, ws_ref, bs_ref,
                         lbl_ref, bnw_ref, l1w_ref, l1b_ref, lv_ref, *rest, layer, tile, n_cast):
    cast_in, rest = rest[:n_cast], rest[n_cast:]
    (x1_ref, st_ref, vn_ref), rest = rest[:3], rest[3:]
    cast_out, (proj_a, proj_b, hn_scr, mix_scr, s_scr) = rest[:n_cast], rest[n_cast:]
    _cast_blocks(cast_in, cast_out)
    step = pl.program_id(1)
    lin = pl.program_id(0) * pl.num_programs(1) + step
    n_chunks = tile // CHUNK
    col_splits = [(IN_COLS * c // n_chunks, IN_COLS * (c + 1) // n_chunks) for c in range(n_chunks)]

    @pl.when(step == 0)
    def _():
        s_scr[...] = jnp.zeros_like(s_scr)

    mod = mod_ref[0, 0]
    x = x_ref[0]

    @pl.when(lin == 0)
    def _():
        proj_a[...] = _dot((x * (1.0 + mod[1:2]) + mod[0:1]).astype(BF16), win_ref[0])

    modn = modn_ref[0, 0]
    hn_scr[...] = (xn_ref[0] * (1.0 + modn[1:2]) + modn[0:1]).astype(BF16)
    lv = lv_ref[...]
    lb = _forget_bound(lbl_ref, layer)

    def chunk(c, proj_scr):
        rows = slice(c * CHUNK, (c + 1) * CHUNK)
        a_out, vn, qq, kk, gg = _mixer_chunk_front(proj_scr, rows, alnw_ref, alnb_ref, ws_ref, bs_ref, lb, lv)
        vn_ref[0] = vn
        vv = proj_scr[rows, 4 * HALF:5 * HALF]
        o_in, b = _hgrn_intra(qq, kk, gg, vv, lv, CHUNK)
        qh = (qq * jnp.exp2(b)).astype(BF16)
        b_last = b[CHUNK - 1:CHUNK, :]
        kdec = kk * jnp.exp2(b_last - b)
        e_last = jnp.exp2(b_last)
        o_parts = []
        for hh in range(N_HEADS):
            hs = slice(hh * HEAD, (hh + 1) * HEAD)
            s_old = s_scr[hh]
            o_parts.append(o_in[:, hs] + _dot(qh[:, hs], s_old.astype(BF16)))
            dec = jnp.broadcast_to(e_last[:, hs], (HEAD, HEAD)).T
            s_scr[hh] = dec * s_old + _dot(kdec[:, hs].T.astype(BF16), vv[:, hs].astype(BF16))
        g = proj_scr[rows, 5 * HALF:6 * HALF]
        b_out = _rms_gate(jnp.concatenate(o_parts, axis=1), bnw_ref[...], g)
        mix_scr[rows, 0:HALF] = a_out.astype(BF16)
        mix_scr[rows, HALF:2 * HALF] = b_out.astype(BF16)

    def run(proj_cur, proj_nxt):
        for c, (c0, c1) in enumerate(col_splits):
            proj_nxt[:, c0:c1] = _dot(hn_scr[...], win_ref[0, :, c0:c1])
            chunk(c, proj_cur)

    @pl.when(lin % 2 == 0)
    def _():
        run(proj_a, proj_b)

    @pl.when(lin % 2 == 1)
    def _():
        run(proj_b, proj_a)

    y = _dot(mix_scr[...], wout_ref[0])
    x1_ref[0] = _layer_norm(ALPHA * x + mod[2:3] * y, l1w_ref[...], l1b_ref[...])

    @pl.when(step == pl.num_programs(1) - 1)
    def _():
        st_ref[0] = s_scr[...]


def _mixer_sample_kernel(x_ref, mod_ref, s0_ref, win_ref, wout_ref, alnw_ref, alnb_ref, ws_ref, bs_ref,
                         lbl_ref, bnw_ref, l1w_ref, l1b_ref, lv_ref, st_in_ref,
                         x1_ref, st_ref, vn_ref,
                         proj_scr, qh_scr, kt_scr, vb_scr, el_scr, o_scr, *, layer, seq_len):
    del st_in_ref
    nseq = SAMPLE_SEQS
    x3 = x_ref[...]
    mod = mod_ref[0]
    h3 = x3 * (1.0 + mod[1][:, None, :]) + mod[0][:, None, :]
    proj_scr[...] = _dot(h3.reshape(CHUNK, D_MODEL).astype(BF16), win_ref[0])
    lv = lv_ref[...]
    lb = _forget_bound(lbl_ref, layer)
    rows = slice(0, CHUNK)
    a_out, vn, qq, kk, gg = _mixer_chunk_front(proj_scr, rows, alnw_ref, alnb_ref, ws_ref, bs_ref, lb, lv)
    vn_ref[...] = vn.reshape(nseq, seq_len, HALF)
    vv = proj_scr[rows, 4 * HALF:5 * HALF]
    o_in, b = _hgrn_intra(qq, kk, gg, vv, lv, seq_len)
    qh_scr[...] = qq * jnp.exp2(b)
    b3 = b.reshape(nseq, seq_len, HALF)
    b_last = jnp.broadcast_to(b3[:, seq_len - 1:seq_len, :], (nseq, seq_len, HALF))
    el_scr[...] = jnp.exp2(b_last)
    kdec = kk * jnp.exp2(b_last.reshape(CHUNK, HALF) - b)
    for hh in range(N_HEADS):
        hs = slice(hh * HEAD, (hh + 1) * HEAD)
        kt_scr[hh] = kdec[:, hs].T.astype(BF16)
    vb_scr[...] = vv.astype(BF16)
    row_seq = lax.broadcasted_iota(jnp.int32, (HEAD, CHUNK), 1) // seq_len

    def per_seq(j, carry):
        rws = pl.ds(pl.multiple_of(j * seq_len, seq_len), seq_len)
        own = row_seq == j
        el = el_scr[j]
        for hh in range(N_HEADS):
            hs = slice(hh * HEAD, (hh + 1) * HEAD)
            s_old = s0_ref[0, j, hh]
            o_scr[rws, hs] = _dot(qh_scr[rws, hs].astype(BF16), s_old.astype(BF16))
            dec = jnp.broadcast_to(el[0:1, hs], (HEAD, HEAD)).T
            kt = jnp.where(own, kt_scr[hh], jnp.zeros((), BF16))
            st_ref[0, j, hh] = dec * s_old + _dot(kt, vb_scr[:, hs])
        return carry

    lax.fori_loop(0, nseq, per_seq, 0)

    g = proj_scr[rows, 5 * HALF:6 * HALF]
    b_out = _rms_gate(o_in + o_scr[...], bnw_ref[...], g)
    mix = jnp.concatenate([a_out, b_out], axis=1).astype(BF16)
    y3 = _dot(mix, wout_ref[0]).reshape(nseq, seq_len, D_MODEL)
    x1_ref[...] = _layer_norm(ALPHA * x3 + mod[2][:, None, :] * y3, l1w_ref[...], l1b_ref[...])


def _mod_rows(mod_ref, per_seq, j):
    if per_seq:
        return mod_ref[0, j][:, None, :]
    return mod_ref[0, 0][j:j + 1][None]


def _swiglu(h, wg_ref, wu_ref, wd_ref):
    acc = None
    for f0, f1 in FF_SPLITS:
        act = (_silu(_dot(h, wg_ref[:, f0:f1])) * _dot(h, wu_ref[:, f0:f1])).astype(BF16)
        part = _dot(act, wd_ref[f0:f1, :])
        acc = part if acc is None else acc + part
    return acc


def _ffn_dense_kernel(x_ref, mod_ref, wg_ref, wu_ref, wd_ref, l2w_ref, l2b_ref, *rest, per_seq, n_cast):
    cast_in, o_ref, cast_out = rest[:n_cast], rest[n_cast], rest[n_cast + 1:]
    _cast_blocks(cast_in, cast_out)
    x3 = x_ref[...]
    groups, rows, _ = x3.shape
    h = (x3 * (1.0 + _mod_rows(mod_ref, per_seq, 4)) + _mod_rows(mod_ref, per_seq, 3))
    h = h.reshape(groups * rows, D_MODEL).astype(BF16)
    f3 = _swiglu(h, wg_ref.at[0], wu_ref.at[0], wd_ref.at[0]).reshape(groups, rows, D_MODEL)
    o_ref[...] = _layer_norm(ALPHA * x3 + _mod_rows(mod_ref, per_seq, 5) * f3, l2w_ref[...], l2b_ref[...])


def _moe_input(x_ref, mod_ref, per_seq):
    x3 = x_ref[...]
    groups, rows, _ = x3.shape
    h3 = x3 * (1.0 + _mod_rows(mod_ref, per_seq, 4)) + _mod_rows(mod_ref, per_seq, 3)
    return h3.reshape(groups * rows, D_MODEL)


def _router_kernel(x_ref, mod_ref, wr_ref, idx_ref, gate_ref, *, per_seq):
    h = _moe_input(x_ref, mod_ref, per_seq)
    n = h.shape[0]
    wr = wr_ref[...]
    h_hi, w_hi = h.astype(BF16), wr.astype(BF16)
    h_lo, w_lo = (h - h_hi.astype(F32)).astype(BF16), (wr - w_hi.astype(F32)).astype(BF16)
    logits = _dot(h_hi, w_hi) + (_dot(h_hi, w_lo) + _dot(h_lo, w_hi))
    z = jnp.exp(logits - jnp.max(logits, axis=-1, keepdims=True))
    p = z / jnp.sum(z, axis=-1, keepdims=True)
    lane = lax.broadcasted_iota(jnp.int32, p.shape, 1)
    p1 = jnp.max(p, axis=-1, keepdims=True)
    i1 = jnp.min(jnp.where(p == p1, lane, N_EXPERTS), axis=-1, keepdims=True)
    rest = jnp.where(lane == i1, -1.0, p)
    p2 = jnp.max(rest, axis=-1, keepdims=True)
    i2 = jnp.min(jnp.where(rest == p2, lane, N_EXPERTS), axis=-1, keepdims=True)
    two = lax.broadcasted_iota(jnp.int32, (n, 2), 1)
    idx_ref[...] = jnp.where(two == 0, i1, i2)
    gate_ref[...] = jnp.where(two == 0, p1, p2) / (p1 + p2)


def _for_each_row(n, fn):
    def body(i, carry):
        fn(i)
        return carry
    lax.fori_loop(0, n, body, 0, unroll=4)


def _lane_block(n, s):
    return pl.ds(s, n, stride=ROW_TILE)


def _tile_of_row(r):
    return pl.ds(pl.multiple_of(r * ROW_TILE, ROW_TILE), ROW_TILE)


def _dispatch_kernel(xp_ref, modp_ref, xs_ref, mods_ref, da_ref, db_ref, win_ref, hg_ref, rows_scr, sem, *,
                     prompt_steps):
    step = pl.program_id(0)
    zero_rows = EXPERT_TILE * ROW_TILE

    def zero_fill(w):
        start = pl.multiple_of(win_ref[0, w] * ROW_TILE, ROW_TILE)
        return pltpu.make_async_copy(rows_scr.at[pl.ds(0, zero_rows)], hg_ref.at[pl.ds(start, zero_rows)], sem)

    @pl.when(step == 0)
    def _():
        rows_scr[pl.ds(0, zero_rows), :] = jnp.zeros((zero_rows, LANE), F32)
        for w in range(win_ref.shape[1]):
            zero_fill(w).start()
            zero_fill(w).wait()

    n = da_ref.shape[2]

    def stage(x_ref, mod_ref, per_seq):
        h = _moe_input(x_ref, mod_ref, per_seq)
        for s in range(ROW_TILE):
            rows_scr[_lane_block(n, s), :] = h[:, s * LANE:(s + 1) * LANE]

    @pl.when(step < prompt_steps)
    def _():
        stage(xp_ref, modp_ref, False)

    @pl.when(step >= prompt_steps)
    def _():
        stage(xs_ref, mods_ref, True)

    def copy(i, slot_ref):
        return pltpu.make_async_copy(rows_scr.at[_tile_of_row(i)], hg_ref.at[_tile_of_row(slot_ref[0, 0, i])], sem)

    _for_each_row(n, lambda i: (copy(i, da_ref).start(priority=0), copy(i, db_ref).start(priority=1)))
    all_rows = pltpu.make_async_copy(rows_scr.at[pl.ds(0, n * ROW_TILE)], hg_ref.at[pl.ds(0, n * ROW_TILE)], sem)
    all_rows.wait()
    all_rows.wait()


def _expert_kernel(te_ref, used_ref, hg_ref, wg_ref, wu_ref, wd_ref, o_ref, raw_scr, h_scr, sem):
    del te_ref
    i = pl.program_id(0)
    tile = h_scr.shape[1]
    cur, nxt = i % 2, (i + 1) % 2

    def fetch(t, slot):
        rows = pl.ds(pl.multiple_of(t * tile * ROW_TILE, tile * ROW_TILE), tile * ROW_TILE)
        return pltpu.make_async_copy(hg_ref.at[rows], raw_scr.at[slot], sem.at[slot])

    def rearrange(slot):
        for s in range(ROW_TILE):
            h_scr[slot, :, s * LANE:(s + 1) * LANE] = raw_scr[slot, _lane_block(tile, s), :].astype(BF16)

    used = used_ref[0]

    @pl.when(i == 0)
    def _():
        fetch(0, 0).start()
        raw_scr[1] = jnp.zeros(raw_scr.shape[1:], F32)
        fetch(0, 0).wait()
        rearrange(0)

        @pl.when(1 < used)
        def _():
            fetch(1, 1).start()

    @pl.when(i + 1 < used)
    def _():
        fetch(i + 1, nxt).wait()

    @pl.when(i + 2 < used)
    def _():
        fetch(i + 2, cur).start()

    @pl.when(i < used)
    def _():
        rearrange(nxt)
        y = _swiglu(h_scr[cur], wg_ref.at[0], wu_ref.at[0], wd_ref.at[0])
        for s in range(ROW_TILE):
            o_ref[_lane_block(tile, s), :] = y[:, s * LANE:(s + 1) * LANE]

    @pl.when(i >= used)
    def _():
        o_ref[...] = jnp.zeros_like(o_ref)


def _combine_kernel(x_ref, mod_ref, gate_ref, da_ref, db_ref, yo_ref, l2w_ref, l2b_ref, o_ref,
                    ya_scr, yb_scr, f_scr, sem, *, per_seq):
    step = pl.program_id(0)
    last = pl.num_programs(0) - 1
    groups, rows, _ = x_ref.shape
    n = groups * rows

    def copy(i, slot_ref, dst, par):
        return pltpu.make_async_copy(yo_ref.at[_tile_of_row(slot_ref[0, 0, i])], dst.at[par, _tile_of_row(i)],
                                     sem.at[par])

    @pl.when(step < last)
    def _():
        par = step % 2
        _for_each_row(n, lambda i: (copy(i, da_ref, ya_scr, par).start(priority=0),
                                    copy(i, db_ref, yb_scr, par).start(priority=1)))

    @pl.when(step > 0)
    def _():
        par = (step - 1) % 2
        for dst in (ya_scr, yb_scr):
            pltpu.make_async_copy(yo_ref.at[pl.ds(0, n * ROW_TILE)], dst.at[par], sem.at[par]).wait()
        x3 = x_ref[...]
        ga, gb = gate_ref[:, 0:1], gate_ref[:, 1:2]
        for s in range(ROW_TILE):
            f_scr[:, s * LANE:(s + 1) * LANE] = (ga * ya_scr[par, _lane_block(n, s), :]
                                                 + gb * yb_scr[par, _lane_block(n, s), :])
        f3 = f_scr[...].reshape(groups, rows, D_MODEL)
        o_ref[...] = _layer_norm(ALPHA * x3 + _mod_rows(mod_ref, per_seq, 5) * f3, l2w_ref[...], l2b_ref[...])


def _params(*semantics):
    return pltpu.CompilerParams(dimension_semantics=semantics, vmem_limit_bytes=VMEM_LIMIT_BYTES)


def _const_spec(shape):
    return pl.BlockSpec(shape, lambda *_: (0,) * len(shape), pipeline_mode=pl.Buffered(1))


def _adaln(c_all, w_ada, b_ada):
    nb = c_all.shape[0]
    tn = 1536

    def body(c_ref, w_ref, b_ref, o_ref):
        c = c_ref[...]
        o_ref[0] = _dot(_silu(c).astype(BF16), w_ref[0].astype(BF16)) + b_ref[0]

    return pl.pallas_call(
        body,
        out_shape=jax.ShapeDtypeStruct((DEPTH, nb, 6 * D_MODEL), F32),
        grid=(DEPTH, 6 * D_MODEL // tn),
        in_specs=[pl.BlockSpec((nb, D_MODEL), lambda l, j: (0, 0)),
                  pl.BlockSpec((1, D_MODEL, tn), lambda l, j: (l, 0, j)),
                  pl.BlockSpec((1, 1, tn), lambda l, j: (l, 0, j))],
        out_specs=pl.BlockSpec((1, nb, tn), lambda l, j: (l, 0, j)),
        compiler_params=_params("arbitrary", "arbitrary"),
        name="adaln_modulation",
    )(c_all, w_ada, b_ada.reshape(DEPTH, 1, 6 * D_MODEL))


def _cast_bf16(w):
    rows, cols = w.shape[-2:]
    w3 = w.reshape(-1, rows, cols)

    def body(w_ref, o_ref):
        o_ref[...] = w_ref[...].astype(BF16)

    out = pl.pallas_call(
        body,
        out_shape=jax.ShapeDtypeStruct(w3.shape, BF16),
        grid=(w3.shape[0],),
        in_specs=[pl.BlockSpec((1, rows, cols), lambda g: (g, 0, 0))],
        out_specs=pl.BlockSpec((1, rows, cols), lambda g: (g, 0, 0)),
        compiler_params=_params("arbitrary"),
        name="cast_bf16",
    )(w3)
    return out.reshape(w.shape)


def _cast_plan(weights, moe_layer, n_steps, step_of):
    in_specs, out_specs, out_shapes = [], [], []
    for w in weights:
        _, n_exp, rows, cols = w.shape
        assert n_steps % n_exp == 0, (n_steps, n_exp)
        parts = n_steps // n_exp
        rb = rows // parts
        assert rb * parts == rows and rb % 16 == 0
        in_specs.append(pl.BlockSpec(
            (1, 1, rb, cols), lambda *g, parts=parts: (moe_layer, step_of(*g) // parts, step_of(*g) % parts, 0)))
        out_specs.append(pl.BlockSpec(
            (1, rb, cols), lambda *g, parts=parts: (step_of(*g) // parts, step_of(*g) % parts, 0)))
        out_shapes.append(jax.ShapeDtypeStruct((n_exp, rows, cols), BF16))
    return in_specs, out_specs, out_shapes


def _layer_spec(shape, layer):
    return pl.BlockSpec((1,) + shape, lambda *_: (layer,) + (0,) * len(shape), pipeline_mode=pl.Buffered(1))


def _mixer_weight_specs(layer):
    return [_layer_spec((D_MODEL, IN_COLS), layer), _layer_spec((D_MODEL, D_MODEL), layer),
            _const_spec((1, HALF)), _const_spec((1, HALF)),
            _const_spec((N_HEADS, CHUNK, CHUNK)), _const_spec((CHUNK, N_HEADS)),
            _const_spec((DEPTH, HALF)), _const_spec((1, HEAD)),
            _const_spec((1, D_MODEL)), _const_spec((1, D_MODEL)), _const_spec((CHUNK, CHUNK))]


def _mixer_prompt(layer, x, mod_p, wts, lv, cast=(), cast_layer=0):
    batch, seq, _ = x.shape
    tile = PROMPT_TILE
    rows_out = seq - CHUNK * ((seq - 1) // CHUNK)
    assert seq % tile == 0 and rows_out == CHUNK
    per = seq // tile

    def nxt(b, s):
        return jnp.minimum(b * per + s + 1, batch * per - 1)

    c_in, c_out, c_shapes = _cast_plan(cast, cast_layer, batch * per, lambda b, s: b * per + s)
    return pl.pallas_call(
        functools.partial(_mixer_prompt_kernel, layer=layer, tile=tile, n_cast=len(cast)),
        out_shape=(jax.ShapeDtypeStruct((batch, seq, D_MODEL), F32),
                   jax.ShapeDtypeStruct((batch, N_HEADS, HEAD, HEAD), F32),
                   jax.ShapeDtypeStruct((batch, CHUNK, HALF), F32), *c_shapes),
        grid=(batch, per),
        in_specs=[pl.BlockSpec((1, tile, D_MODEL), lambda b, s: (b, s, 0)),
                  pl.BlockSpec((1, 1, 6, D_MODEL), lambda b, s: (layer, b, 0, 0)),
                  pl.BlockSpec((1, tile, D_MODEL), lambda b, s: (nxt(b, s) // per, nxt(b, s) % per, 0)),
                  pl.BlockSpec((1, 1, 6, D_MODEL), lambda b, s: (layer, nxt(b, s) // per, 0, 0))]
        + _mixer_weight_specs(layer) + c_in,
        out_specs=(pl.BlockSpec((1, tile, D_MODEL), lambda b, s: (b, s, 0)),
                   pl.BlockSpec((1, N_HEADS, HEAD, HEAD), lambda b, s: (b, 0, 0, 0)),
                   pl.BlockSpec((1, CHUNK, HALF), lambda b, s: (b, 0, 0)), *c_out),
        scratch_shapes=[pltpu.VMEM((tile, IN_COLS), F32), pltpu.VMEM((tile, IN_COLS), F32),
                        pltpu.VMEM((tile, D_MODEL), BF16), pltpu.VMEM((tile, D_MODEL), BF16),
                        pltpu.VMEM((N_HEADS, HEAD, HEAD), F32)],
        compiler_params=_params("arbitrary", "arbitrary"),
        name="token_mixer_prompt",
    )(x, mod_p, x, mod_p, *wts, lv, *cast)


def _mixer_sample(layer, x, mod_s, state, wts, lv, states_so_far):
    nseq_all, seq_len, _ = x.shape
    nseq = SAMPLE_SEQS
    assert nseq * seq_len == CHUNK and nseq_all % nseq == 0 and seq_len == SUBLANE
    in_specs = [pl.BlockSpec((nseq, seq_len, D_MODEL), lambda j: (j, 0, 0)),
                pl.BlockSpec((1, 6, nseq, D_MODEL), lambda j: (layer, 0, j, 0)),
                pl.BlockSpec((1, nseq, N_HEADS, HEAD, HEAD), lambda j: (layer, j, 0, 0, 0))]
    in_specs += _mixer_weight_specs(layer) + [pl.BlockSpec(memory_space=pl.ANY)]
    operands = (x, mod_s, state, *wts, lv, states_so_far)
    aliases = {len(operands) - 1: 1}
    return pl.pallas_call(
        functools.partial(_mixer_sample_kernel, layer=layer, seq_len=seq_len),
        out_shape=(jax.ShapeDtypeStruct((nseq_all, seq_len, D_MODEL), F32),
                   jax.ShapeDtypeStruct(state.shape, F32),
                   jax.ShapeDtypeStruct((nseq_all, seq_len, HALF), F32)),
        grid=(nseq_all // nseq,),
        in_specs=in_specs,
        out_specs=(pl.BlockSpec((nseq, seq_len, D_MODEL), lambda j: (j, 0, 0)),
                   pl.BlockSpec((1, nseq, N_HEADS, HEAD, HEAD), lambda j: (layer, j, 0, 0, 0)),
                   pl.BlockSpec((nseq, seq_len, HALF), lambda j: (j, 0, 0))),
        input_output_aliases=aliases,
        scratch_shapes=[pltpu.VMEM((CHUNK, IN_COLS), F32), pltpu.VMEM((CHUNK, HALF), F32),
                        pltpu.VMEM((N_HEADS, HEAD, CHUNK), BF16), pltpu.VMEM((CHUNK, HALF), BF16),
                        pltpu.VMEM((nseq, seq_len, HALF), F32), pltpu.VMEM((CHUNK, HALF), F32)],
        compiler_params=_params("arbitrary"),
        name="token_mixer_sample",
    )(*operands)


def _row_blocking(x, per_seq, tile):
    batch, seq, _ = x.shape
    if per_seq:
        groups = tile // seq
        assert batch % groups == 0
        grid = (batch // groups,)
        x_spec = pl.BlockSpec((groups, seq, D_MODEL), lambda i: (i, 0, 0))
        return grid, x_spec, groups, lambda layer: pl.BlockSpec((1, 6, groups, D_MODEL), lambda i: (layer, 0, i, 0))
    assert seq % tile == 0
    per = seq // tile
    grid = (batch * per,)
    x_spec = pl.BlockSpec((1, tile, D_MODEL), lambda i: (i // per, i % per, 0))
    return grid, x_spec, per, lambda layer: pl.BlockSpec((1, 1, 6, D_MODEL), lambda i: (layer, i // per, 0, 0))


def _ffn_dense(layer, x, mod, per_seq, wg, wu, wd, l2w, l2b, cast=(), cast_layer=0):
    tile = FFN_TILE if not per_seq else CHUNK
    grid, x_spec, _, mod_spec = _row_blocking(x, per_seq, tile)
    c_in, c_out, c_shapes = _cast_plan(cast, cast_layer, grid[0], lambda i: i)
    return pl.pallas_call(
        functools.partial(_ffn_dense_kernel, per_seq=per_seq, n_cast=len(cast)),
        out_shape=(jax.ShapeDtypeStruct(x.shape, F32), *c_shapes),
        grid=grid,
        in_specs=[x_spec, mod_spec(layer),
                  _layer_spec((D_MODEL, D_FF), layer // 2), _layer_spec((D_MODEL, D_FF), layer // 2),
                  _layer_spec((D_FF, D_MODEL), layer // 2),
                  _const_spec((1, D_MODEL)), _const_spec((1, D_MODEL))] + c_in,
        out_specs=(x_spec, *c_out),
        compiler_params=_params("arbitrary"),
        name="ffn_dense",
    )(x, mod, wg, wu, wd, l2w, l2b, *cast)


def _router(layer, x, mod, per_seq, w_router):
    batch, seq, _ = x.shape
    tile = FFN_TILE if not per_seq else CHUNK
    grid, x_spec, _, mod_spec = _row_blocking(x, per_seq, tile)
    n = batch * seq
    return pl.pallas_call(
        functools.partial(_router_kernel, per_seq=per_seq),
        out_shape=(jax.ShapeDtypeStruct((n, 2), jnp.int32), jax.ShapeDtypeStruct((n, 2), F32)),
        grid=grid,
        in_specs=[x_spec, mod_spec(layer), _const_spec((D_MODEL, N_EXPERTS))],
        out_specs=(pl.BlockSpec((tile, 2), lambda i: (i, 0)), pl.BlockSpec((tile, 2), lambda i: (i, 0))),
        compiler_params=_params("arbitrary"),
        name="moe_router",
    )(x, mod, w_router)


def _slot_spec(tile):
    return pl.BlockSpec((1, 1, tile), lambda i: (i, 0, 0), memory_space=pltpu.SMEM)


def _dispatch(layer, xp, mod_p, xs, mod_s, slot_a, slot_b, zero_windows, n_rows):
    n_prompt, n_sample = xp.shape[0] * xp.shape[1], xs.shape[0] * xs.shape[1]
    tile = math.gcd(FFN_TILE, n_sample)
    _, xp_spec, _, modp_spec = _row_blocking(xp, False, tile)
    _, xs_spec, _, mods_spec = _row_blocking(xs, True, tile)
    p_steps, s_steps = n_prompt // tile, n_sample // tile

    def first(spec):
        return pl.BlockSpec(spec.block_shape, lambda i: spec.index_map(jnp.minimum(i, p_steps - 1)))

    def second(spec):
        return pl.BlockSpec(spec.block_shape, lambda i: spec.index_map(jnp.maximum(i - p_steps, 0)))

    return pl.pallas_call(
        functools.partial(_dispatch_kernel, prompt_steps=p_steps),
        out_shape=jax.ShapeDtypeStruct((n_rows * ROW_TILE, LANE), F32),
        grid=(p_steps + s_steps,),
        in_specs=[first(xp_spec), first(modp_spec(layer)), second(xs_spec), second(mods_spec(layer)),
                  _slot_spec(tile), _slot_spec(tile), pl.BlockSpec(memory_space=pltpu.SMEM)],
        out_specs=pl.BlockSpec(memory_space=pl.ANY),
        scratch_shapes=[pltpu.VMEM((max(tile, EXPERT_TILE) * ROW_TILE, LANE), F32), pltpu.SemaphoreType.DMA(())],
        compiler_params=_params("arbitrary"),
        name="moe_dispatch",
    )(xp, mod_p, xs, mod_s, slot_a.reshape(-1, 1, tile), slot_b.reshape(-1, 1, tile), zero_windows)


def _experts(hg, mp, tile_expert, tiles_used, wg, wu, wd):
    tile = EXPERT_TILE
    assert mp % tile == 0 and mp // tile >= 2 and hg.shape[0] >= mp * ROW_TILE
    return pl.pallas_call(
        _expert_kernel,
        out_shape=jax.ShapeDtypeStruct((mp * ROW_TILE, LANE), F32),
        grid_spec=pltpu.PrefetchScalarGridSpec(
            num_scalar_prefetch=2,
            grid=(mp // tile,),
            in_specs=[pl.BlockSpec(memory_space=pl.ANY),
                      pl.BlockSpec((1, D_MODEL, D_FF), lambda i, te, used: (te[i], 0, 0)),
                      pl.BlockSpec((1, D_MODEL, D_FF), lambda i, te, used: (te[i], 0, 0)),
                      pl.BlockSpec((1, D_FF, D_MODEL), lambda i, te, used: (te[i], 0, 0))],
            out_specs=pl.BlockSpec((tile * ROW_TILE, LANE), lambda i, te, used: (i, 0)),
            scratch_shapes=[pltpu.VMEM((2, tile * ROW_TILE, LANE), F32), pltpu.VMEM((2, tile, D_MODEL), BF16),
                            pltpu.SemaphoreType.DMA((2,))]),
        compiler_params=_params("arbitrary"),
        name="moe_experts",
    )(tile_expert, tiles_used, hg, wg, wu, wd)


def _combine(layer, x, mod, per_seq, gate, slot_a, slot_b, yo, l2w, l2b):
    tile = FFN_TILE if not per_seq else CHUNK
    (steps,), x_spec, _, mod_spec = _row_blocking(x, per_seq, tile)

    def late(spec):
        return pl.BlockSpec(spec.block_shape, lambda i: spec.index_map(jnp.maximum(i - 1, 0)))

    def early(spec):
        return pl.BlockSpec(spec.block_shape, lambda i: spec.index_map(jnp.minimum(i, steps - 1)),
                            memory_space=pltpu.SMEM)

    return pl.pallas_call(
        functools.partial(_combine_kernel, per_seq=per_seq),
        out_shape=jax.ShapeDtypeStruct(x.shape, F32),
        grid=(steps + 1,),
        in_specs=[late(x_spec), late(mod_spec(layer)), late(pl.BlockSpec((tile, 2), lambda i: (i, 0))),
                  early(_slot_spec(tile)), early(_slot_spec(tile)), pl.BlockSpec(memory_space=pl.ANY),
                  _const_spec((1, D_MODEL)), _const_spec((1, D_MODEL))],
        out_specs=late(x_spec),
        scratch_shapes=[pltpu.VMEM((2, tile * ROW_TILE, LANE), F32), pltpu.VMEM((2, tile * ROW_TILE, LANE), F32),
                        pltpu.VMEM((tile, D_MODEL), F32), pltpu.SemaphoreType.DMA((2,))],
        compiler_params=_params("arbitrary"),
        name="moe_combine",
    )(x, mod, gate, slot_a.reshape(-1, 1, tile), slot_b.reshape(-1, 1, tile), yo, l2w, l2b)


def _routing_tables(idx, n_pad_rows):
    e_flat = jnp.concatenate([idx[:, 0], idx[:, 1]])
    onehot = (e_flat[:, None] == jnp.arange(N_EXPERTS, dtype=jnp.int32)[None, :]).astype(jnp.int32)
    csum = jnp.cumsum(onehot, axis=0)
    padded = ((csum[-1] + EXPERT_TILE - 1) // EXPERT_TILE) * EXPERT_TILE
    pend = jnp.cumsum(padded)
    slot = jnp.sum(onehot * (csum - 1 + (pend - padded)[None, :]), axis=1).astype(jnp.int32)
    tile_start = jnp.arange(n_pad_rows // EXPERT_TILE, dtype=jnp.int32) * EXPERT_TILE
    tile_expert = jnp.sum((tile_start[:, None] >= pend[None, :]).astype(jnp.int32), axis=1)
    tiles_used = (pend[-1:] // EXPERT_TILE).astype(jnp.int32)
    pad_start = pend - padded + csum[-1]
    tail = jnp.minimum(pend[-1] + tile_start[:TAIL_WINDOWS], n_pad_rows)
    zero_windows = jnp.concatenate([pad_start, tail]).astype(jnp.int32)[None, :]
    return slot, jnp.minimum(tile_expert, N_EXPERTS - 1).astype(jnp.int32), tiles_used, zero_windows


def _round_up(a, b):
    return (a + b - 1) // b * b


def kernel(x_prompt, x_sample, state_hgrn, c_prompt, c_sample, w_ada, b_ada, w_in, w_out, a_ln_w, a_ln_b, a_ws, a_bs, lb_logits, b_norm_w, ln1_w, ln1_b, ln2_w, ln2_b, w_ff_gate, w_ff_up, w_ff_down, w_router, e_gate, e_up, e_down):
    batch, seq, _ = x_prompt.shape
    nseq, seq_len, _ = x_sample.shape
    n_prompt, n_sample = batch * seq, nseq * seq_len
    n_tok = n_prompt + n_sample

    mod = _adaln(jnp.concatenate([c_prompt, c_sample], axis=0), w_ada, b_ada)
    mod_p = mod[:, :batch].reshape(DEPTH, batch, 6, D_MODEL)
    mod_s = mod[:, batch:].reshape(DEPTH, nseq, 6, D_MODEL).transpose(0, 2, 1, 3)

    lv_p = jnp.asarray(_level_ids(CHUNK))
    lv_s = jnp.asarray(_level_ids(seq_len))
    reps = CHUNK // seq_len
    n_pad_rows = _round_up(2 * n_tok, EXPERT_TILE) + N_EXPERTS * EXPERT_TILE

    w_in_b, w_out_b = _cast_bf16(w_in), _cast_bf16(w_out)
    ff_b = tuple(_cast_bf16(w) for w in (w_ff_gate, w_ff_up, w_ff_down))
    assert DEPTH % 2 == 0

    xp, xs = x_prompt, x_sample
    st_p, st_s, cv_p, cv_s = [], jnp.zeros_like(state_hgrn), [], []
    for l in range(DEPTH):
        shared = (a_ln_w[l][None], a_ln_b[l][None])
        tail = (lb_logits, b_norm_w[l][None], ln1_w[l][None], ln1_b[l][None])
        wts_p = (w_in_b, w_out_b) + shared + (a_ws[l], a_bs[l].T) + tail
        ws_s = jnp.tile(a_ws[l][:, :seq_len, :seq_len], (1, reps, reps))
        bs_s = jnp.tile(a_bs[l][:, :seq_len].T, (reps, 1))
        wts_s = (w_in_b, w_out_b) + shared + (ws_s, bs_s) + tail
        if l % 2 == 0:
            xp, sp, vp = _mixer_prompt(l, xp, mod_p, wts_p, lv_p)
        else:
            xp, sp, vp, ed_b = _mixer_prompt(l, xp, mod_p, wts_p, lv_p, (e_down,), l // 2)
        xs, st_s, vs = _mixer_sample(l, xs, mod_s, state_hgrn, wts_s, lv_s, st_s)
        st_p.append(sp), cv_p.append(vp), cv_s.append(vs)
        l2w, l2b = ln2_w[l][None], ln2_b[l][None]
        if l % 2 == 0:
            xp, eg_b, eu_b = _ffn_dense(l, xp, mod_p, False, *ff_b, l2w, l2b, (e_gate, e_up), l // 2)
            xs, = _ffn_dense(l, xs, mod_s, True, *ff_b, l2w, l2b)
        else:
            wr = w_router[l // 2]
            ip, gp = _router(l, xp, mod_p, False, wr)
            is_, gs = _router(l, xs, mod_s, True, wr)
            slot, tile_expert, tiles_used, zero_windows = _routing_tables(
                jnp.concatenate([ip, is_], axis=0), n_pad_rows)
            sa_p, sa_s = slot[:n_prompt], slot[n_prompt:n_tok]
            sb_p, sb_s = slot[n_tok:n_tok + n_prompt], slot[n_tok + n_prompt:]
            hg = _dispatch(l, xp, mod_p, xs, mod_s, slot[:n_tok], slot[n_tok:], zero_windows,
                           n_pad_rows + EXPERT_TILE)
            yo = _experts(hg, n_pad_rows, tile_expert, tiles_used, eg_b, eu_b, ed_b)
            xp = _combine(l, xp, mod_p, False, gp, sa_p, sb_p, yo, l2w, l2b)
            xs = _combine(l, xs, mod_s, True, gs, sa_s, sb_s, yo, l2w, l2b)
    return (xp, xs, jnp.stack(st_p), st_s, jnp.stack(cv_p), jnp.stack(cv_s))
```

```python
import functools
import math

import numpy as np
import jax
import jax.numpy as jnp
from jax import lax
from jax.experimental import pallas as pl
from jax.experimental.pallas import tpu as pltpu

F32 = jnp.float32
BF16 = jnp.bfloat16

D_MODEL = 1024
DEPTH = 4
HALF = 512
N_HEADS = 4
HEAD = 128
CHUNK = 128
IN_COLS = 6 * HALF
D_FF = 2816
N_EXPERTS = 8
ALPHA = (2.0 * DEPTH) ** 0.25
LN_EPS = 1e-5
RMS_EPS = 1e-6
LOG2_E = math.log2(math.e)

VMEM_LIMIT_BYTES = 56 * 1024 * 1024
LANE = 128
SUBLANE = 8
BF16_SUBLANE = 16
ROW_TILE = D_MODEL // LANE
assert ROW_TILE == SUBLANE

PROMPT_TILE = 512
SAMPLE_SEQS = 16
FFN_TILE = 512
EXPERT_TILE = 256
TAIL_WINDOWS = N_EXPERTS + 1
ADALN_COLS = 1536
DMA_ISSUE_UNROLL = 8
FF_SPLITS = ((0, 1024), (1024, 2048), (2048, D_FF))


def _dot(a, b):
    return jnp.dot(a, b, preferred_element_type=F32)


def _dot_nt(a, b):
    return lax.dot_general(a, b, (((1,), (1,)), ((), ())), preferred_element_type=F32)


def _gelu(x):
    return 0.5 * x * (1.0 + lax.erf(x * (1.0 / math.sqrt(2.0))))


def _silu(x):
    return x * jax.nn.sigmoid(x)


def _layer_norm(z, w, b):
    mu = jnp.mean(z, axis=-1, keepdims=True)
    zc = z - mu
    var = jnp.mean(zc * zc, axis=-1, keepdims=True)
    return zc * lax.rsqrt(var + LN_EPS) * w + b


def _level_ids(block):
    t = np.arange(CHUNK)[:, None]
    s = np.arange(CHUNK)[None, :]
    x = t ^ s
    lv = np.where(x == 0, 0, np.floor(np.log2(np.maximum(x, 1))).astype(np.int64) + 1)
    ok = (s <= t) & (x < block)
    return np.where(ok, lv, -1).astype(np.int32)


def _reference_rows(b, m):
    rows, width = b.shape
    two_m = 2 * m
    if two_m >= SUBLANE:
        nb = rows // two_m
        b3 = b.reshape(nb, two_m, width)
        r = jnp.broadcast_to(b3[:, m - 1:m, :], (nb, two_m, width))
        return r.reshape(rows, width)
    t = lax.broadcasted_iota(jnp.int32, (rows, width), 0)
    tm = t & (two_m - 1)
    down1 = pltpu.roll(b, 1, 0)
    if m == 1:
        return jnp.where(tm == 0, b, down1)
    up1 = pltpu.roll(b, rows - 1, 0)
    down2 = pltpu.roll(b, 2, 0)
    return jnp.where(tm == 0, up1, jnp.where(tm == 1, b, jnp.where(tm == 2, down1, down2)))


def _hgrn_intra(qq, kk, gg, vv, lv, block):
    cm = (lv >= 0).astype(BF16)
    g_hi = gg.astype(BF16)
    rem = gg - g_hi.astype(F32)
    g_mid = rem.astype(BF16)
    g_lo = (rem - g_mid.astype(F32)).astype(BF16)
    b = _dot(cm, g_hi) + _dot(cm, g_mid) + _dot(cm, g_lo)

    heads = [slice(h * HEAD, (h + 1) * HEAD) for h in range(N_HEADS)]
    qb = qq.astype(BF16)
    kb = kk.astype(BF16)
    scores = [jnp.where(lv == 0, _dot_nt(qb[:, hs], kb[:, hs]), 0.0) for hs in heads]
    m = block // 2
    while m >= 1:
        level = int(math.log2(m)) + 1
        e = jnp.exp2(-jnp.abs(b - _reference_rows(b, m)))
        qe = (qq * e).astype(BF16)
        ke = (kk * e).astype(BF16)
        for h, hs in enumerate(heads):
            scores[h] = jnp.where(lv == level, _dot_nt(qe[:, hs], ke[:, hs]), scores[h])
        m //= 2
    vb = vv.astype(BF16)
    o = jnp.concatenate([_dot(scores[h].astype(BF16), vb[:, hs]) for h, hs in enumerate(heads)], axis=1)
    return o, b


def _cast_blocks(in_refs, out_refs):
    for src, dst in zip(in_refs, out_refs):
        dst[...] = src[0].astype(BF16)


def _forget_bound(lbl_ref, layer):
    z = lbl_ref[...]
    z = z - jnp.max(z, axis=0, keepdims=True)
    ez = jnp.exp(z)
    p = ez / jnp.sum(ez, axis=0, keepdims=True)
    c = p[0:1]
    for r in range(1, layer + 1):
        c = c + p[r:r + 1]
    return c - p[0:1]


def _mixer_chunk_front(proj_scr, rows, alnw_ref, alnb_ref, ws_ref, bs_ref, lb, lv):
    u = proj_scr[rows, 0 * HALF:1 * HALF]
    v = proj_scr[rows, 1 * HALF:2 * HALF]
    q = proj_scr[rows, 2 * HALF:3 * HALF]
    f = proj_scr[rows, 3 * HALF:4 * HALF]
    ug = _gelu(u)
    vn = _layer_norm(_gelu(v), alnw_ref[...], alnb_ref[...])
    vnb = vn.astype(BF16)
    a_parts = []
    for h in range(N_HEADS):
        hs = slice(h * HEAD, (h + 1) * HEAD)
        w = jnp.where(lv >= 0, ws_ref[h], 0.0).astype(BF16)
        mixed = _dot(w, vnb[:, hs]) + bs_ref[:, h:h + 1]
        a_parts.append(ug[:, hs] * mixed)
    a_out = jnp.concatenate(a_parts, axis=1)
    fg = lb + (1.0 - lb) * jax.nn.sigmoid(f)
    return a_out, vn, _silu(q), 1.0 - fg, jnp.log(fg) * LOG2_E


def _rms_gate(o, bnw, g):
    parts = []
    for h in range(N_HEADS):
        hs = slice(h * HEAD, (h + 1) * HEAD)
        oh = o[:, hs]
        parts.append(oh * lax.rsqrt(jnp.mean(oh * oh, axis=-1, keepdims=True) + RMS_EPS) * bnw)
    return jnp.concatenate(parts, axis=1) * _silu(g)


def _mixer_prompt_kernel(x_ref, mod_ref, xn_ref, modn_ref, win_ref, wout_ref, alnw_ref, alnb_ref, ws_ref, bs_ref,
                         lbl_ref, bnw_ref, l1w_ref, l1b_ref, lv_ref, *rest, layer, tile, n_cast):
    cast_in, rest = rest[:n_cast], rest[n_cast:]
    (x1_ref, st_ref, vn_ref), rest = rest[:3], rest[3:]
    cast_out, (proj_a, proj_b, hn_scr, mix_scr, s_scr) = rest[:n_cast], rest[n_cast:]
    _cast_blocks(cast_in, cast_out)
    step = pl.program_id(1)
    lin = pl.program_id(0) * pl.num_programs(1) + step
    n_chunks = tile // CHUNK
    col_splits = [(IN_COLS * c // n_chunks, IN_COLS * (c + 1) // n_chunks) for c in range(n_chunks)]

    @pl.when(step == 0)
    def _():
        s_scr[...] = jnp.zeros_like(s_scr)

    mod = mod_ref[0, 0]
    x = x_ref[0]

    @pl.when(lin == 0)
    def _():
        proj_a[...] = _dot((x * (1.0 + mod[1:2]) + mod[0:1]).astype(BF16), win_ref[0])

    modn = modn_ref[0, 0]
    hn_scr[...] = (xn_ref[0] * (1.0 + modn[1:2]) + modn[0:1]).astype(BF16)
    lv = lv_ref[...]
    lb = _forget_bound(lbl_ref, layer)

    def chunk(c, proj_scr):
        rows = slice(c * CHUNK, (c + 1) * CHUNK)
        a_out, vn, qq, kk, gg = _mixer_chunk_front(proj_scr, rows, alnw_ref, alnb_ref, ws_ref, bs_ref, lb, lv)
        vn_ref[0] = vn
        vv = proj_scr[rows, 4 * HALF:5 * HALF]
        o_in, b = _hgrn_intra(qq, kk, gg, vv, lv, CHUNK)
        qh = (qq * jnp.exp2(b)).astype(BF16)
        b_last = b[CHUNK - 1:CHUNK, :]
        kdec = kk * jnp.exp2(b_last - b)
        e_last = jnp.exp2(b_last)
        o_parts = []
        for hh in range(N_HEADS):
            hs = slice(hh * HEAD, (hh + 1) * HEAD)
            s_old = s_scr[hh]
            o_parts.append(o_in[:, hs] + _dot(qh[:, hs], s_old.astype(BF16)))
            dec = jnp.broadcast_to(e_last[:, hs], (HEAD, HEAD)).T
            s_scr[hh] = dec * s_old + _dot(kdec[:, hs].T.astype(BF16), vv[:, hs].astype(BF16))
        g = proj_scr[rows, 5 * HALF:6 * HALF]
        b_out = _rms_gate(jnp.concatenate(o_parts, axis=1), bnw_ref[...], g)
        mix_scr[rows, 0:HALF] = a_out.astype(BF16)
        mix_scr[rows, HALF:2 * HALF] = b_out.astype(BF16)

    def run(proj_cur, proj_nxt):
        for c, (c0, c1) in enumerate(col_splits):
            proj_nxt[:, c0:c1] = _dot(hn_scr[...], win_ref[0, :, c0:c1])
            chunk(c, proj_cur)

    @pl.when(lin % 2 == 0)
    def _():
        run(proj_a, proj_b)

    @pl.when(lin % 2 == 1)
    def _():
        run(proj_b, proj_a)

    y = _dot(mix_scr[...], wout_ref[0])
    x1_ref[0] = _layer_norm(ALPHA * x + mod[2:3] * y, l1w_ref[...], l1b_ref[...])

    @pl.when(step == pl.num_programs(1) - 1)
    def _():
        st_ref[0] = s_scr[...]


def _mixer_sample_kernel(x_ref, mod_ref, s0_ref, win_ref, wout_ref, alnw_ref, alnb_ref, ws_ref, bs_ref,
                         lbl_ref, bnw_ref, l1w_ref, l1b_ref, lv_ref, st_in_ref,
                         x1_ref, st_ref, vn_ref,
                         proj_scr, qh_scr, kt_scr, vb_scr, el_scr, o_scr, *, layer, seq_len):
    del st_in_ref
    nseq = SAMPLE_SEQS
    x3 = x_ref[...]
    mod = mod_ref[0]
    h3 = x3 * (1.0 + mod[1][:, None, :]) + mod[0][:, None, :]
    proj_scr[...] = _dot(h3.reshape(CHUNK, D_MODEL).astype(BF16), win_ref[0])
    lv = lv_ref[...]
    lb = _forget_bound(lbl_ref, layer)
    rows = slice(0, CHUNK)
    a_out, vn, qq, kk, gg = _mixer_chunk_front(proj_scr, rows, alnw_ref, alnb_ref, ws_ref, bs_ref, lb, lv)
    vn_ref[...] = vn.reshape(nseq, seq_len, HALF)
    vv = proj_scr[rows, 4 * HALF:5 * HALF]
    o_in, b = _hgrn_intra(qq, kk, gg, vv, lv, seq_len)
    qh_scr[...] = qq * jnp.exp2(b)
    b3 = b.reshape(nseq, seq_len, HALF)
    b_last = jnp.broadcast_to(b3[:, seq_len - 1:seq_len, :], (nseq, seq_len, HALF))
    el_scr[...] = jnp.exp2(b_last)
    kdec = kk * jnp.exp2(b_last.reshape(CHUNK, HALF) - b)
    for hh in range(N_HEADS):
        hs = slice(hh * HEAD, (hh + 1) * HEAD)
        kt_scr[hh] = kdec[:, hs].T.astype(BF16)
    vb_scr[...] = vv.astype(BF16)
    row_seq = lax.broadcasted_iota(jnp.int32, (HEAD, CHUNK), 1) // seq_len

    def per_seq(j, carry):
        rws = pl.ds(pl.multiple_of(j * seq_len, seq_len), seq_len)
        own = row_seq == j
        el = el_scr[j]
        for hh in range(N_HEADS):
            hs = slice(hh * HEAD, (hh + 1) * HEAD)
            s_old = s0_ref[0, j, hh]
            o_scr[rws, hs] = _dot(qh_scr[rws, hs].astype(BF16), s_old.astype(BF16))
            dec = jnp.broadcast_to(el[0:1, hs], (HEAD, HEAD)).T
            kt = jnp.where(own, kt_scr[hh], jnp.zeros((), BF16))
            st_ref[0, j, hh] = dec * s_old + _dot(kt, vb_scr[:, hs])
        return carry

    lax.fori_loop(0, nseq, per_seq, 0)

    g = proj_scr[rows, 5 * HALF:6 * HALF]
    b_out = _rms_gate(o_in + o_scr[...], bnw_ref[...], g)
    mix = jnp.concatenate([a_out, b_out], axis=1).astype(BF16)
    y3 = _dot(mix, wout_ref[0]).reshape(nseq, seq_len, D_MODEL)
    x1_ref[...] = _layer_norm(ALPHA * x3 + mod[2][:, None, :] * y3, l1w_ref[...], l1b_ref[...])


def _mod_rows(mod_ref, per_seq, j):
    if per_seq:
        return mod_ref[0, j][:, None, :]
    return mod_ref[0, 0][j:j + 1][None]


def _swiglu(h, wg_ref, wu_ref, wd_ref):
    acc = None
    for f0, f1 in FF_SPLITS:
        act = (_silu(_dot(h, wg_ref[:, f0:f1])) * _dot(h, wu_ref[:, f0:f1])).astype(BF16)
        part = _dot(act, wd_ref[f0:f1, :])
        acc = part if acc is None else acc + part
    return acc


def _ffn_dense_kernel(x_ref, mod_ref, wg_ref, wu_ref, wd_ref, l2w_ref, l2b_ref, *rest, per_seq, n_cast):
    cast_in, o_ref, cast_out = rest[:n_cast], rest[n_cast], rest[n_cast + 1:]
    _cast_blocks(cast_in, cast_out)
    x3 = x_ref[...]
    groups, rows, _ = x3.shape
    h = (x3 * (1.0 + _mod_rows(mod_ref, per_seq, 4)) + _mod_rows(mod_ref, per_seq, 3))
    h = h.reshape(groups * rows, D_MODEL).astype(BF16)
    f3 = _swiglu(h, wg_ref.at[0], wu_ref.at[0], wd_ref.at[0]).reshape(groups, rows, D_MODEL)
    o_ref[...] = _layer_norm(ALPHA * x3 + _mod_rows(mod_ref, per_seq, 5) * f3, l2w_ref[...], l2b_ref[...])


def _moe_input(x_ref, mod_ref, per_seq):
    x3 = x_ref[...]
    groups, rows, _ = x3.shape
    h3 = x3 * (1.0 + _mod_rows(mod_ref, per_seq, 4)) + _mod_rows(mod_ref, per_seq, 3)
    return h3.reshape(groups * rows, D_MODEL)


def _router_kernel(x_ref, mod_ref, wr_ref, idx_ref, gate_ref, *, per_seq):
    h = _moe_input(x_ref, mod_ref, per_seq)
    n = h.shape[0]
    wr = wr_ref[...]
    h_hi, w_hi = h.astype(BF16), wr.astype(BF16)
    h_lo, w_lo = (h - h_hi.astype(F32)).astype(BF16), (wr - w_hi.astype(F32)).astype(BF16)
    logits = _dot(h_hi, w_hi) + (_dot(h_hi, w_lo) + _dot(h_lo, w_hi))
    z = jnp.exp(logits - jnp.max(logits, axis=-1, keepdims=True))
    p = z / jnp.sum(z, axis=-1, keepdims=True)
    lane = lax.broadcasted_iota(jnp.int32, p.shape, 1)
    p1 = jnp.max(p, axis=-1, keepdims=True)
    i1 = jnp.min(jnp.where(p == p1, lane, N_EXPERTS), axis=-1, keepdims=True)
    rest = jnp.where(lane == i1, -1.0, p)
    p2 = jnp.max(rest, axis=-1, keepdims=True)
    i2 = jnp.min(jnp.where(rest == p2, lane, N_EXPERTS), axis=-1, keepdims=True)
    two = lax.broadcasted_iota(jnp.int32, (n, 2), 1)
    idx_ref[...] = jnp.where(two == 0, i1, i2)
    gate_ref[...] = jnp.where(two == 0, p1, p2) / (p1 + p2)


def _for_each_row(n, fn):
    def body(i, carry):
        fn(i)
        return carry
    lax.fori_loop(0, n, body, 0, unroll=DMA_ISSUE_UNROLL)


def _lane_block(n, s):
    return pl.ds(s, n, stride=ROW_TILE)


def _tile_of_row(r):
    return pl.ds(pl.multiple_of(r * ROW_TILE, ROW_TILE), ROW_TILE)


def _dispatch_kernel(xp_ref, modp_ref, xs_ref, mods_ref, da_ref, db_ref, win_ref, hg_ref, rows_scr, sem, *,
                     prompt_steps):
    step = pl.program_id(0)
    zero_rows = EXPERT_TILE * ROW_TILE

    def zero_fill(w):
        start = pl.multiple_of(win_ref[0, w] * ROW_TILE, ROW_TILE)
        return pltpu.make_async_copy(rows_scr.at[pl.ds(0, zero_rows)], hg_ref.at[pl.ds(start, zero_rows)], sem)

    @pl.when(step == 0)
    def _():
        rows_scr[pl.ds(0, zero_rows), :] = jnp.zeros((zero_rows, LANE), F32)
        for w in range(win_ref.shape[1]):
            zero_fill(w).start()
            zero_fill(w).wait()

    n = da_ref.shape[2]

    def stage(x_ref, mod_ref, per_seq):
        h = _moe_input(x_ref, mod_ref, per_seq)
        for s in range(ROW_TILE):
            rows_scr[_lane_block(n, s), :] = h[:, s * LANE:(s + 1) * LANE]

    @pl.when(step < prompt_steps)
    def _():
        stage(xp_ref, modp_ref, False)

    @pl.when(step >= prompt_steps)
    def _():
        stage(xs_ref, mods_ref, True)

    def copy(i, slot_ref):
        return pltpu.make_async_copy(rows_scr.at[_tile_of_row(i)], hg_ref.at[_tile_of_row(slot_ref[0, 0, i])], sem)

    _for_each_row(n, lambda i: (copy(i, da_ref).start(priority=0), copy(i, db_ref).start(priority=1)))
    all_rows = pltpu.make_async_copy(rows_scr.at[pl.ds(0, n * ROW_TILE)], hg_ref.at[pl.ds(0, n * ROW_TILE)], sem)
    all_rows.wait()
    all_rows.wait()


def _expert_kernel(te_ref, used_ref, hg_ref, wg_ref, wu_ref, wd_ref, o_ref, raw_scr, h_scr, sem):
    del te_ref
    i = pl.program_id(0)
    tile = h_scr.shape[1]
    cur, nxt = i % 2, (i + 1) % 2

    def fetch(t, slot):
        rows = pl.ds(pl.multiple_of(t * tile * ROW_TILE, tile * ROW_TILE), tile * ROW_TILE)
        return pltpu.make_async_copy(hg_ref.at[rows], raw_scr.at[slot], sem.at[slot])

    def rearrange(slot):
        for s in range(ROW_TILE):
            h_scr[slot, :, s * LANE:(s + 1) * LANE] = raw_scr[slot, _lane_block(tile, s), :].astype(BF16)

    used = used_ref[0]

    @pl.when(i == 0)
    def _():
        fetch(0, 0).start()
        raw_scr[1] = jnp.zeros(raw_scr.shape[1:], F32)
        fetch(0, 0).wait()
        rearrange(0)

        @pl.when(1 < used)
        def _():
            fetch(1, 1).start()

    @pl.when(i + 1 < used)
    def _():
        fetch(i + 1, nxt).wait()

    @pl.when(i + 2 < used)
    def _():
        fetch(i + 2, cur).start()

    @pl.when(i < used)
    def _():
        rearrange(nxt)
        y = _swiglu(h_scr[cur], wg_ref.at[0], wu_ref.at[0], wd_ref.at[0])
        for s in range(ROW_TILE):
            o_ref[_lane_block(tile, s), :] = y[:, s * LANE:(s + 1) * LANE]

    @pl.when(i >= used)
    def _():
        o_ref[...] = jnp.zeros_like(o_ref)


def _combine_kernel(x_ref, mod_ref, gate_ref, da_ref, db_ref, yo_ref, l2w_ref, l2b_ref, o_ref,
                    ya_scr, yb_scr, f_scr, sem, *, per_seq):
    step = pl.program_id(0)
    last = pl.num_programs(0) - 1
    groups, rows, _ = x_ref.shape
    n = groups * rows

    def copy(i, slot_ref, dst, par):
        return pltpu.make_async_copy(yo_ref.at[_tile_of_row(slot_ref[0, 0, i])], dst.at[par, _tile_of_row(i)],
                                     sem.at[par])

    @pl.when(step < last)
    def _():
        par = step % 2
        _for_each_row(n, lambda i: (copy(i, da_ref, ya_scr, par).start(priority=0),
                                    copy(i, db_ref, yb_scr, par).start(priority=1)))

    @pl.when(step > 0)
    def _():
        par = (step - 1) % 2
        for dst in (ya_scr, yb_scr):
            pltpu.make_async_copy(yo_ref.at[pl.ds(0, n * ROW_TILE)], dst.at[par], sem.at[par]).wait()
        x3 = x_ref[...]
        ga, gb = gate_ref[:, 0:1], gate_ref[:, 1:2]
        for s in range(ROW_TILE):
            f_scr[:, s * LANE:(s + 1) * LANE] = (ga * ya_scr[par, _lane_block(n, s), :]
                                                 + gb * yb_scr[par, _lane_block(n, s), :])
        f3 = f_scr[...].reshape(groups, rows, D_MODEL)
        o_ref[...] = _layer_norm(ALPHA * x3 + _mod_rows(mod_ref, per_seq, 5) * f3, l2w_ref[...], l2b_ref[...])


def _params(*semantics):
    return pltpu.CompilerParams(dimension_semantics=semantics, vmem_limit_bytes=VMEM_LIMIT_BYTES)


def _const_spec(shape):
    return pl.BlockSpec(shape, lambda *_: (0,) * len(shape), pipeline_mode=pl.Buffered(1))


def _adaln(c_all, w_ada, b_ada):
    nb = c_all.shape[0]
    tn = ADALN_COLS

    def body(c_ref, w_ref, b_ref, o_ref):
        c = c_ref[...]
        o_ref[0] = _dot(_silu(c).astype(BF16), w_ref[0].astype(BF16)) + b_ref[0]

    return pl.pallas_call(
        body,
        out_shape=jax.ShapeDtypeStruct((DEPTH, nb, 6 * D_MODEL), F32),
        grid=(DEPTH, 6 * D_MODEL // tn),
        in_specs=[pl.BlockSpec((nb, D_MODEL), lambda l, j: (0, 0)),
                  pl.BlockSpec((1, D_MODEL, tn), lambda l, j: (l, 0, j)),
                  pl.BlockSpec((1, 1, tn), lambda l, j: (l, 0, j))],
        out_specs=pl.BlockSpec((1, nb, tn), lambda l, j: (l, 0, j)),
        compiler_params=_params("arbitrary", "arbitrary"),
        name="adaln_modulation",
    )(c_all, w_ada, b_ada.reshape(DEPTH, 1, 6 * D_MODEL))


def _cast_bf16(w):
    rows, cols = w.shape[-2:]
    w3 = w.reshape(-1, rows, cols)

    def body(w_ref, o_ref):
        o_ref[...] = w_ref[...].astype(BF16)

    out = pl.pallas_call(
        body,
        out_shape=jax.ShapeDtypeStruct(w3.shape, BF16),
        grid=(w3.shape[0],),
        in_specs=[pl.BlockSpec((1, rows, cols), lambda g: (g, 0, 0))],
        out_specs=pl.BlockSpec((1, rows, cols), lambda g: (g, 0, 0)),
        compiler_params=_params("arbitrary"),
        name="cast_bf16",
    )(w3)
    return out.reshape(w.shape)


def _cast_plan(weights, moe_layer, n_steps, step_of):
    in_specs, out_specs, out_shapes = [], [], []
    for w in weights:
        _, n_exp, rows, cols = w.shape
        assert n_steps % n_exp == 0, (n_steps, n_exp)
        parts = n_steps // n_exp
        rb = rows // parts
        assert rb * parts == rows and rb % BF16_SUBLANE == 0
        in_specs.append(pl.BlockSpec(
            (1, 1, rb, cols), lambda *g, parts=parts: (moe_layer, step_of(*g) // parts, step_of(*g) % parts, 0)))
        out_specs.append(pl.BlockSpec(
            (1, rb, cols), lambda *g, parts=parts: (step_of(*g) // parts, step_of(*g) % parts, 0)))
        out_shapes.append(jax.ShapeDtypeStruct((n_exp, rows, cols), BF16))
    return in_specs, out_specs, out_shapes


def _layer_spec(shape, layer):
    return pl.BlockSpec((1,) + shape, lambda *_: (layer,) + (0,) * len(shape), pipeline_mode=pl.Buffered(1))


def _mixer_weight_specs(layer):
    return [_layer_spec((D_MODEL, IN_COLS), layer), _layer_spec((D_MODEL, D_MODEL), layer),
            _const_spec((1, HALF)), _const_spec((1, HALF)),
            _const_spec((N_HEADS, CHUNK, CHUNK)), _const_spec((CHUNK, N_HEADS)),
            _const_spec((DEPTH, HALF)), _const_spec((1, HEAD)),
            _const_spec((1, D_MODEL)), _const_spec((1, D_MODEL)), _const_spec((CHUNK, CHUNK))]


def _mixer_prompt(layer, x, mod_p, wts, lv, cast=(), cast_layer=0):
    batch, seq, _ = x.shape
    tile = PROMPT_TILE
    rows_out = seq - CHUNK * ((seq - 1) // CHUNK)
    assert seq % tile == 0 and rows_out == CHUNK
    per = seq // tile

    def nxt(b, s):
        return jnp.minimum(b * per + s + 1, batch * per - 1)

    c_in, c_out, c_shapes = _cast_plan(cast, cast_layer, batch * per, lambda b, s: b * per + s)
    return pl.pallas_call(
        functools.partial(_mixer_prompt_kernel, layer=layer, tile=tile, n_cast=len(cast)),
        out_shape=(jax.ShapeDtypeStruct((batch, seq, D_MODEL), F32),
                   jax.ShapeDtypeStruct((batch, N_HEADS, HEAD, HEAD), F32),
                   jax.ShapeDtypeStruct((batch, CHUNK, HALF), F32), *c_shapes),
        grid=(batch, per),
        in_specs=[pl.BlockSpec((1, tile, D_MODEL), lambda b, s: (b, s, 0)),
                  pl.BlockSpec((1, 1, 6, D_MODEL), lambda b, s: (layer, b, 0, 0)),
                  pl.BlockSpec((1, tile, D_MODEL), lambda b, s: (nxt(b, s) // per, nxt(b, s) % per, 0)),
                  pl.BlockSpec((1, 1, 6, D_MODEL), lambda b, s: (layer, nxt(b, s) // per, 0, 0))]
        + _mixer_weight_specs(layer) + c_in,
        out_specs=(pl.BlockSpec((1, tile, D_MODEL), lambda b, s: (b, s, 0)),
                   pl.BlockSpec((1, N_HEADS, HEAD, HEAD), lambda b, s: (b, 0, 0, 0)),
                   pl.BlockSpec((1, CHUNK, HALF), lambda b, s: (b, 0, 0)), *c_out),
        scratch_shapes=[pltpu.VMEM((tile, IN_COLS), F32), pltpu.VMEM((tile, IN_COLS), F32),
                        pltpu.VMEM((tile, D_MODEL), BF16), pltpu.VMEM((tile, D_MODEL), BF16),
                        pltpu.VMEM((N_HEADS, HEAD, HEAD), F32)],
        compiler_params=_params("arbitrary", "arbitrary"),
        name="token_mixer_prompt",
    )(x, mod_p, x, mod_p, *wts, lv, *cast)


def _mixer_sample(layer, x, mod_s, state, wts, lv, states_so_far):
    nseq_all, seq_len, _ = x.shape
    nseq = SAMPLE_SEQS
    assert nseq * seq_len == CHUNK and nseq_all % nseq == 0 and seq_len == SUBLANE
    in_specs = [pl.BlockSpec((nseq, seq_len, D_MODEL), lambda j: (j, 0, 0)),
                pl.BlockSpec((1, 6, nseq, D_MODEL), lambda j: (layer, 0, j, 0)),
                pl.BlockSpec((1, nseq, N_HEADS, HEAD, HEAD), lambda j: (layer, j, 0, 0, 0))]
    in_specs += _mixer_weight_specs(layer) + [pl.BlockSpec(memory_space=pl.ANY)]
    operands = (x, mod_s, state, *wts, lv, states_so_far)
    aliases = {len(operands) - 1: 1}
    return pl.pallas_call(
        functools.partial(_mixer_sample_kernel, layer=layer, seq_len=seq_len),
        out_shape=(jax.ShapeDtypeStruct((nseq_all, seq_len, D_MODEL), F32),
                   jax.ShapeDtypeStruct(state.shape, F32),
                   jax.ShapeDtypeStruct((nseq_all, seq_len, HALF), F32)),
        grid=(nseq_all // nseq,),
        in_specs=in_specs,
        out_specs=(pl.BlockSpec((nseq, seq_len, D_MODEL), lambda j: (j, 0, 0)),
                   pl.BlockSpec((1, nseq, N_HEADS, HEAD, HEAD), lambda j: (layer, j, 0, 0, 0)),
                   pl.BlockSpec((nseq, seq_len, HALF), lambda j: (j, 0, 0))),
        input_output_aliases=aliases,
        scratch_shapes=[pltpu.VMEM((CHUNK, IN_COLS), F32), pltpu.VMEM((CHUNK, HALF), F32),
                        pltpu.VMEM((N_HEADS, HEAD, CHUNK), BF16), pltpu.VMEM((CHUNK, HALF), BF16),
                        pltpu.VMEM((nseq, seq_len, HALF), F32), pltpu.VMEM((CHUNK, HALF), F32)],
        compiler_params=_params("arbitrary"),
        name="token_mixer_sample",
    )(*operands)


def _row_blocking(x, per_seq, tile):
    batch, seq, _ = x.shape
    if per_seq:
        groups = tile // seq
        assert batch % groups == 0
        grid = (batch // groups,)
        x_spec = pl.BlockSpec((groups, seq, D_MODEL), lambda i: (i, 0, 0))
        return grid, x_spec, groups, lambda layer: pl.BlockSpec((1, 6, groups, D_MODEL), lambda i: (layer, 0, i, 0))
    assert seq % tile == 0
    per = seq // tile
    grid = (batch * per,)
    x_spec = pl.BlockSpec((1, tile, D_MODEL), lambda i: (i // per, i % per, 0))
    return grid, x_spec, per, lambda layer: pl.BlockSpec((1, 1, 6, D_MODEL), lambda i: (layer, i // per, 0, 0))


def _ffn_dense(layer, x, mod, per_seq, wg, wu, wd, l2w, l2b, cast=(), cast_layer=0):
    tile = FFN_TILE if not per_seq else CHUNK
    grid, x_spec, _, mod_spec = _row_blocking(x, per_seq, tile)
    c_in, c_out, c_shapes = _cast_plan(cast, cast_layer, grid[0], lambda i: i)
    return pl.pallas_call(
        functools.partial(_ffn_dense_kernel, per_seq=per_seq, n_cast=len(cast)),
        out_shape=(jax.ShapeDtypeStruct(x.shape, F32), *c_shapes),
        grid=grid,
        in_specs=[x_spec, mod_spec(layer),
                  _layer_spec((D_MODEL, D_FF), layer // 2), _layer_spec((D_MODEL, D_FF), layer // 2),
                  _layer_spec((D_FF, D_MODEL), layer // 2),
                  _const_spec((1, D_MODEL)), _const_spec((1, D_MODEL))] + c_in,
        out_specs=(x_spec, *c_out),
        compiler_params=_params("arbitrary"),
        name="ffn_dense",
    )(x, mod, wg, wu, wd, l2w, l2b, *cast)


def _router(layer, x, mod, per_seq, w_router):
    batch, seq, _ = x.shape
    tile = FFN_TILE if not per_seq else CHUNK
    grid, x_spec, _, mod_spec = _row_blocking(x, per_seq, tile)
    n = batch * seq
    return pl.pallas_call(
        functools.partial(_router_kernel, per_seq=per_seq),
        out_shape=(jax.ShapeDtypeStruct((n, 2), jnp.int32), jax.ShapeDtypeStruct((n, 2), F32)),
        grid=grid,
        in_specs=[x_spec, mod_spec(layer), _const_spec((D_MODEL, N_EXPERTS))],
        out_specs=(pl.BlockSpec((tile, 2), lambda i: (i, 0)), pl.BlockSpec((tile, 2), lambda i: (i, 0))),
        compiler_params=_params("arbitrary"),
        name="moe_router",
    )(x, mod, w_router)


def _slot_spec(tile):
    return pl.BlockSpec((1, 1, tile), lambda i: (i, 0, 0), memory_space=pltpu.SMEM)


def _dispatch(layer, xp, mod_p, xs, mod_s, slot_a, slot_b, zero_windows, n_rows):
    n_prompt, n_sample = xp.shape[0] * xp.shape[1], xs.shape[0] * xs.shape[1]
    tile = math.gcd(FFN_TILE, n_sample)
    _, xp_spec, _, modp_spec = _row_blocking(xp, False, tile)
    _, xs_spec, _, mods_spec = _row_blocking(xs, True, tile)
    p_steps, s_steps = n_prompt // tile, n_sample // tile

    def first(spec):
        return pl.BlockSpec(spec.block_shape, lambda i: spec.index_map(jnp.minimum(i, p_steps - 1)))

    def second(spec):
        return pl.BlockSpec(spec.block_shape, lambda i: spec.index_map(jnp.maximum(i - p_steps, 0)))

    return pl.pallas_call(
        functools.partial(_dispatch_kernel, prompt_steps=p_steps),
        out_shape=jax.ShapeDtypeStruct((n_rows * ROW_TILE, LANE), F32),
        grid=(p_steps + s_steps,),
        in_specs=[first(xp_spec), first(modp_spec(layer)), second(xs_spec), second(mods_spec(layer)),
                  _slot_spec(tile), _slot_spec(tile), pl.BlockSpec(memory_space=pltpu.SMEM)],
        out_specs=pl.BlockSpec(memory_space=pl.ANY),
        scratch_shapes=[pltpu.VMEM((max(tile, EXPERT_TILE) * ROW_TILE, LANE), F32), pltpu.SemaphoreType.DMA(())],
        compiler_params=_params("arbitrary"),
        name="moe_dispatch",
    )(xp, mod_p, xs, mod_s, slot_a.reshape(-1, 1, tile), slot_b.reshape(-1, 1, tile), zero_windows)


def _experts(hg, mp, tile_expert, tiles_used, wg, wu, wd):
    tile = EXPERT_TILE
    assert mp % tile == 0 and mp // tile >= 2 and hg.shape[0] >= mp * ROW_TILE
    return pl.pallas_call(
        _expert_kernel,
        out_shape=jax.ShapeDtypeStruct((mp * ROW_TILE, LANE), F32),
        grid_spec=pltpu.PrefetchScalarGridSpec(
            num_scalar_prefetch=2,
            grid=(mp // tile,),
            in_specs=[pl.BlockSpec(memory_space=pl.ANY),
                      pl.BlockSpec((1, D_MODEL, D_FF), lambda i, te, used: (te[i], 0, 0)),
                      pl.BlockSpec((1, D_MODEL, D_FF), lambda i, te, used: (te[i], 0, 0)),
                      pl.BlockSpec((1, D_FF, D_MODEL), lambda i, te, used: (te[i], 0, 0))],
            out_specs=pl.BlockSpec((tile * ROW_TILE, LANE), lambda i, te, used: (i, 0)),
            scratch_shapes=[pltpu.VMEM((2, tile * ROW_TILE, LANE), F32), pltpu.VMEM((2, tile, D_MODEL), BF16),
                            pltpu.SemaphoreType.DMA((2,))]),
        compiler_params=_params("arbitrary"),
        name="moe_experts",
    )(tile_expert, tiles_used, hg, wg, wu, wd)


def _combine(layer, x, mod, per_seq, gate, slot_a, slot_b, yo, l2w, l2b):
    tile = FFN_TILE if not per_seq else CHUNK
    (steps,), x_spec, _, mod_spec = _row_blocking(x, per_seq, tile)

    def late(spec):
        return pl.BlockSpec(spec.block_shape, lambda i: spec.index_map(jnp.maximum(i - 1, 0)))

    def early(spec):
        return pl.BlockSpec(spec.block_shape, lambda i: spec.index_map(jnp.minimum(i, steps - 1)),
                            memory_space=pltpu.SMEM)

    return pl.pallas_call(
        functools.partial(_combine_kernel, per_seq=per_seq),
        out_shape=jax.ShapeDtypeStruct(x.shape, F32),
        grid=(steps + 1,),
        in_specs=[late(x_spec), late(mod_spec(layer)), late(pl.BlockSpec((tile, 2), lambda i: (i, 0))),
                  early(_slot_spec(tile)), early(_slot_spec(tile)), pl.BlockSpec(memory_space=pl.ANY),
                  _const_spec((1, D_MODEL)), _const_spec((1, D_MODEL))],
        out_specs=late(x_spec),
        scratch_shapes=[pltpu.VMEM((2, tile * ROW_TILE, LANE), F32), pltpu.VMEM((2, tile * ROW_TILE, LANE), F32),
                        pltpu.VMEM((tile, D_MODEL), F32), pltpu.SemaphoreType.DMA((2,))],
        compiler_params=_params("arbitrary"),
        name="moe_combine",
    )(x, mod, gate, slot_a.reshape(-1, 1, tile), slot_b.reshape(-1, 1, tile), yo, l2w, l2b)


def _routing_tables(idx, n_pad_rows):
    e_flat = jnp.concatenate([idx[:, 0], idx[:, 1]])
    onehot = (e_flat[:, None] == jnp.arange(N_EXPERTS, dtype=jnp.int32)[None, :]).astype(jnp.int32)
    csum = jnp.cumsum(onehot, axis=0)
    padded = ((csum[-1] + EXPERT_TILE - 1) // EXPERT_TILE) * EXPERT_TILE
    pend = jnp.cumsum(padded)
    slot = jnp.sum(onehot * (csum - 1 + (pend - padded)[None, :]), axis=1).astype(jnp.int32)
    tile_start = jnp.arange(n_pad_rows // EXPERT_TILE, dtype=jnp.int32) * EXPERT_TILE
    tile_expert = jnp.sum((tile_start[:, None] >= pend[None, :]).astype(jnp.int32), axis=1)
    tiles_used = (pend[-1:] // EXPERT_TILE).astype(jnp.int32)
    pad_start = pend - padded + csum[-1]
    tail = jnp.minimum(pend[-1] + tile_start[:TAIL_WINDOWS], n_pad_rows)
    zero_windows = jnp.concatenate([pad_start, tail]).astype(jnp.int32)[None, :]
    return slot, jnp.minimum(tile_expert, N_EXPERTS - 1).astype(jnp.int32), tiles_used, zero_windows


def _round_up(a, b):
    return (a + b - 1) // b * b


def kernel(x_prompt, x_sample, state_hgrn, c_prompt, c_sample, w_ada, b_ada, w_in, w_out, a_ln_w, a_ln_b, a_ws, a_bs, lb_logits, b_norm_w, ln1_w, ln1_b, ln2_w, ln2_b, w_ff_gate, w_ff_up, w_ff_down, w_router, e_gate, e_up, e_down):
    batch, seq, _ = x_prompt.shape
    nseq, seq_len, _ = x_sample.shape
    n_prompt, n_sample = batch * seq, nseq * seq_len
    n_tok = n_prompt + n_sample

    mod = _adaln(jnp.concatenate([c_prompt, c_sample], axis=0), w_ada, b_ada)
    mod_p = mod[:, :batch].reshape(DEPTH, batch, 6, D_MODEL)
    mod_s = mod[:, batch:].reshape(DEPTH, nseq, 6, D_MODEL).transpose(0, 2, 1, 3)

    lv_p = jnp.asarray(_level_ids(CHUNK))
    lv_s = jnp.asarray(_level_ids(seq_len))
    reps = CHUNK // seq_len
    n_pad_rows = _round_up(2 * n_tok, EXPERT_TILE) + N_EXPERTS * EXPERT_TILE

    w_in_b, w_out_b = _cast_bf16(w_in), _cast_bf16(w_out)
    ff_b = tuple(_cast_bf16(w) for w in (w_ff_gate, w_ff_up, w_ff_down))
    assert DEPTH % 2 == 0

    xp, xs = x_prompt, x_sample
    st_p, st_s, cv_p, cv_s = [], jnp.zeros_like(state_hgrn), [], []
    for l in range(DEPTH):
        shared = (a_ln_w[l][None], a_ln_b[l][None])
        tail = (lb_logits, b_norm_w[l][None], ln1_w[l][None], ln1_b[l][None])
        wts_p = (w_in_b, w_out_b) + shared + (a_ws[l], a_bs[l].T) + tail
        ws_s = jnp.tile(a_ws[l][:, :seq_len, :seq_len], (1, reps, reps))
        bs_s = jnp.tile(a_bs[l][:, :seq_len].T, (reps, 1))
        wts_s = (w_in_b, w_out_b) + shared + (ws_s, bs_s) + tail
        if l % 2 == 0:
            xp, sp, vp = _mixer_prompt(l, xp, mod_p, wts_p, lv_p)
        else:
            xp, sp, vp, ed_b = _mixer_prompt(l, xp, mod_p, wts_p, lv_p, (e_down,), l // 2)
        xs, st_s, vs = _mixer_sample(l, xs, mod_s, state_hgrn, wts_s, lv_s, st_s)
        st_p.append(sp), cv_p.append(vp), cv_s.append(vs)
        l2w, l2b = ln2_w[l][None], ln2_b[l][None]
        if l % 2 == 0:
            xp, eg_b, eu_b = _ffn_dense(l, xp, mod_p, False, *ff_b, l2w, l2b, (e_gate, e_up), l // 2)
            xs, = _ffn_dense(l, xs, mod_s, True, *ff_b, l2w, l2b)
        else:
            wr = w_router[l // 2]
            ip, gp = _router(l, xp, mod_p, False, wr)
            is_, gs = _router(l, xs, mod_s, True, wr)
            slot, tile_expert, tiles_used, zero_windows = _routing_tables(
                jnp.concatenate([ip, is_], axis=0), n_pad_rows)
            sa_p, sa_s = slot[:n_prompt], slot[n_prompt:n_tok]
            sb_p, sb_s = slot[n_tok:n_tok + n_prompt], slot[n_tok + n_prompt:]
            hg = _dispatch(l, xp, mod_p, xs, mod_s, slot[:n_tok], slot[n_tok:], zero_windows,
                           n_pad_rows + EXPERT_TILE)
            yo = _experts(hg, n_pad_rows, tile_expert, tiles_used, eg_b, eu_b, ed_b)
            xp = _combine(l, xp, mod_p, False, gp, sa_p, sb_p, yo, l2w, l2b)
            xs = _combine(l, xs, mod_s, True, gs, sa_s, sb_s, yo, l2w, l2b)
    return (xp, xs, jnp.stack(st_p), st_s, jnp.stack(cv_p), jnp.stack(cv_s))
```

```python
import functools
import math

import numpy as np
import jax
import jax.numpy as jnp
from jax import lax
from jax.experimental import pallas as pl
from jax.experimental.pallas import tpu as pltpu

F32 = jnp.float32
BF16 = jnp.bfloat16

D_MODEL = 1024
DEPTH = 4
HALF = 512
N_HEADS = 4
HEAD = 128
CHUNK = 128
IN_COLS = 6 * HALF
D_FF = 2816
N_EXPERTS = 8
ALPHA = (2.0 * DEPTH) ** 0.25
LN_EPS = 1e-5
RMS_EPS = 1e-6
LOG2_E = math.log2(math.e)

VMEM_LIMIT_BYTES = 56 * 1024 * 1024
LANE = 128
SUBLANE = 8
BF16_SUBLANE = 16
ROW_TILE = D_MODEL // LANE
assert ROW_TILE == SUBLANE

PROMPT_TILE = 512
SAMPLE_SEQS = 16
FFN_TILE = 512
EXPERT_TILE = 256
TAIL_WINDOWS = N_EXPERTS + 1
ADALN_COLS = 1536
DMA_ISSUE_UNROLL = 8
FF_SPLITS = ((0, 1024), (1024, 2048), (2048, D_FF))


def _dot(a, b):
    return jnp.dot(a, b, preferred_element_type=F32)


def _dot_nt(a, b):
    return lax.dot_general(a, b, (((1,), (1,)), ((), ())), preferred_element_type=F32)


def _gelu(x):
    return 0.5 * x * (1.0 + lax.erf(x * (1.0 / math.sqrt(2.0))))


def _silu(x):
    return x * jax.nn.sigmoid(x)


def _layer_norm(z, w, b):
    mu = jnp.mean(z, axis=-1, keepdims=True)
    zc = z - mu
    var = jnp.mean(zc * zc, axis=-1, keepdims=True)
    return zc * lax.rsqrt(var + LN_EPS) * w + b


def _level_ids(block):
    t = np.arange(CHUNK)[:, None]
    s = np.arange(CHUNK)[None, :]
    x = t ^ s
    lv = np.where(x == 0, 0, np.floor(np.log2(np.maximum(x, 1))).astype(np.int64) + 1)
    ok = (s <= t) & (x < block)
    return np.where(ok, lv, -1).astype(np.int32)


def _reference_rows(b, m):
    rows, width = b.shape
    two_m = 2 * m
    if two_m >= SUBLANE:
        nb = rows // two_m
        b3 = b.reshape(nb, two_m, width)
        r = jnp.broadcast_to(b3[:, m - 1:m, :], (nb, two_m, width))
        return r.reshape(rows, width)
    t = lax.broadcasted_iota(jnp.int32, (rows, width), 0)
    tm = t & (two_m - 1)
    down1 = pltpu.roll(b, 1, 0)
    if m == 1:
        return jnp.where(tm == 0, b, down1)
    up1 = pltpu.roll(b, rows - 1, 0)
    down2 = pltpu.roll(b, 2, 0)
    return jnp.where(tm == 0, up1, jnp.where(tm == 1, b, jnp.where(tm == 2, down1, down2)))


def _hgrn_intra(qq, kk, gg, vv, lv, block):
    cm = (lv >= 0).astype(BF16)
    g_hi = gg.astype(BF16)
    rem = gg - g_hi.astype(F32)
    g_mid = rem.astype(BF16)
    g_lo = (rem - g_mid.astype(F32)).astype(BF16)
    b = _dot(cm, g_hi) + _dot(cm, g_mid) + _dot(cm, g_lo)

    heads = [slice(h * HEAD, (h + 1) * HEAD) for h in range(N_HEADS)]
    qb = qq.astype(BF16)
    kb = kk.astype(BF16)
    scores = [jnp.where(lv == 0, _dot_nt(qb[:, hs], kb[:, hs]), 0.0) for hs in heads]
    m = block // 2
    while m >= 1:
        level = int(math.log2(m)) + 1
        e = jnp.exp2(-jnp.abs(b - _reference_rows(b, m)))
        qe = (qq * e).astype(BF16)
        ke = (kk * e).astype(BF16)
        for h, hs in enumerate(heads):
            scores[h] = jnp.where(lv == level, _dot_nt(qe[:, hs], ke[:, hs]), scores[h])
        m //= 2
    vb = vv.astype(BF16)
    o = jnp.concatenate([_dot(scores[h].astype(BF16), vb[:, hs]) for h, hs in enumerate(heads)], axis=1)
    return o, b


def _cast_blocks(in_refs, out_refs):
    for src, dst in zip(in_refs, out_refs):
        dst[...] = src[0].astype(BF16)


def _forget_bound(lbl_ref, layer):
    z = lbl_ref[...]
    z = z - jnp.max(z, axis=0, keepdims=True)
    ez = jnp.exp(z)
    p = ez / jnp.sum(ez, axis=0, keepdims=True)
    c = p[0:1]
    for r in range(1, layer + 1):
        c = c + p[r:r + 1]
    return c - p[0:1]


def _mixer_chunk_front(proj_scr, rows, alnw_ref, alnb_ref, ws_ref, bs_ref, lb, lv):
    u = proj_scr[rows, 0 * HALF:1 * HALF]
    v = proj_scr[rows, 1 * HALF:2 * HALF]
    q = proj_scr[rows, 2 * HALF:3 * HALF]
    f = proj_scr[rows, 3 * HALF:4 * HALF]
    ug = _gelu(u)
    vn = _layer_norm(_gelu(v), alnw_ref[...], alnb_ref[...])
    vnb = vn.astype(BF16)
    a_parts = []
    for h in range(N_HEADS):
        hs = slice(h * HEAD, (h + 1) * HEAD)
        w = jnp.where(lv >= 0, ws_ref[h], 0.0).astype(BF16)
        mixed = _dot(w, vnb[:, hs]) + bs_ref[:, h:h + 1]
        a_parts.append(ug[:, hs] * mixed)
    a_out = jnp.concatenate(a_parts, axis=1)
    fg = lb + (1.0 - lb) * jax.nn.sigmoid(f)
    return a_out, vn, _silu(q), 1.0 - fg, jnp.log(fg) * LOG2_E


def _rms_gate(o, bnw, g):
    parts = []
    for h in range(N_HEADS):
        hs = slice(h * HEAD, (h + 1) * HEAD)
        oh = o[:, hs]
        parts.append(oh * lax.rsqrt(jnp.mean(oh * oh, axis=-1, keepdims=True) + RMS_EPS) * bnw)
    return jnp.concatenate(parts, axis=1) * _silu(g)


def _mixer_prompt_kernel(x_ref, mod_ref, xn_ref, modn_ref, win_ref, wout_ref, alnw_ref, alnb_ref, ws_ref, bs_ref,
                         lbl_ref, bnw_ref, l1w_ref, l1b_ref, lv_ref, *rest, layer, tile, n_cast):
    cast_in, rest = rest[:n_cast], rest[n_cast:]
    (x1_ref, st_ref, vn_ref), rest = rest[:3], rest[3:]
    cast_out, (proj_a, proj_b, hn_scr, mix_scr, s_scr) = rest[:n_cast], rest[n_cast:]
    _cast_blocks(cast_in, cast_out)
    step = pl.program_id(1)
    lin = pl.program_id(0) * pl.num_programs(1) + step
    n_chunks = tile // CHUNK
    col_splits = [(IN_COLS * c // n_chunks, IN_COLS * (c + 1) // n_chunks) for c in range(n_chunks)]

    @pl.when(step == 0)
    def _():
        s_scr[...] = jnp.zeros_like(s_scr)

    mod = mod_ref[0, 0]
    x = x_ref[0]

    @pl.when(lin == 0)
    def _():
        proj_a[...] = _dot((x * (1.0 + mod[1:2]) + mod[0:1]).astype(BF16), win_ref[0])

    modn = modn_ref[0, 0]
    hn_scr[...] = (xn_ref[0] * (1.0 + modn[1:2]) + modn[0:1]).astype(BF16)
    lv = lv_ref[...]
    lb = _forget_bound(lbl_ref, layer)

    def chunk(c, proj_scr):
        rows = slice(c * CHUNK, (c + 1) * CHUNK)
        a_out, vn, qq, kk, gg = _mixer_chunk_front(proj_scr, rows, alnw_ref, alnb_ref, ws_ref, bs_ref, lb, lv)
        vn_ref[0] = vn
        vv = proj_scr[rows, 4 * HALF:5 * HALF]
        o_in, b = _hgrn_intra(qq, kk, gg, vv, lv, CHUNK)
        qh = (qq * jnp.exp2(b)).astype(BF16)
        b_last = b[CHUNK - 1:CHUNK, :]
        kdec = kk * jnp.exp2(b_last - b)
        e_last = jnp.exp2(b_last)
        o_parts = []
        for hh in range(N_HEADS):
            hs = slice(hh * HEAD, (hh + 1) * HEAD)
            s_old = s_scr[hh]
            o_parts.append(o_in[:, hs] + _dot(qh[:, hs], s_old.astype(BF16)))
            dec = jnp.broadcast_to(e_last[:, hs], (HEAD, HEAD)).T
            s_scr[hh] = dec * s_old + _dot(kdec[:, hs].T.astype(BF16), vv[:, hs].astype(BF16))
        g = proj_scr[rows, 5 * HALF:6 * HALF]
        b_out = _rms_gate(jnp.concatenate(o_parts, axis=1), bnw_ref[...], g)
        mix_scr[rows, 0:HALF] = a_out.astype(BF16)
        mix_scr[rows, HALF:2 * HALF] = b_out.astype(BF16)

    def finish(r0, r1):
        y = _dot(mix_scr[r0:r1, :], wout_ref[0])
        x1_ref[0, r0:r1, :] = _layer_norm(ALPHA * x_ref[0, r0:r1, :] + mod[2:3] * y, l1w_ref[...], l1b_ref[...])

    def run(proj_cur, proj_nxt):
        for c, (c0, c1) in enumerate(col_splits):
            proj_nxt[:, c0:c1] = _dot(hn_scr[...], win_ref[0, :, c0:c1])
            chunk(c, proj_cur)
            if c % 2 == 1:
                finish((c - 1) * CHUNK, (c + 1) * CHUNK)

    @pl.when(lin % 2 == 0)
    def _():
        run(proj_a, proj_b)

    @pl.when(lin % 2 == 1)
    def _():
        run(proj_b, proj_a)

    @pl.when(step == pl.num_programs(1) - 1)
    def _():
        st_ref[0] = s_scr[...]


def _mixer_sample_kernel(x_ref, mod_ref, s0_ref, win_ref, wout_ref, alnw_ref, alnb_ref, ws_ref, bs_ref,
                         lbl_ref, bnw_ref, l1w_ref, l1b_ref, lv_ref, st_in_ref,
                         x1_ref, st_ref, vn_ref,
                         proj_scr, qh_scr, kt_scr, vb_scr, el_scr, o_scr, *, layer, seq_len):
    del st_in_ref
    nseq = SAMPLE_SEQS
    x3 = x_ref[...]
    mod = mod_ref[0]
    h3 = x3 * (1.0 + mod[1][:, None, :]) + mod[0][:, None, :]
    proj_scr[...] = _dot(h3.reshape(CHUNK, D_MODEL).astype(BF16), win_ref[0])
    lv = lv_ref[...]
    lb = _forget_bound(lbl_ref, layer)
    rows = slice(0, CHUNK)
    a_out, vn, qq, kk, gg = _mixer_chunk_front(proj_scr, rows, alnw_ref, alnb_ref, ws_ref, bs_ref, lb, lv)
    vn_ref[...] = vn.reshape(nseq, seq_len, HALF)
    vv = proj_scr[rows, 4 * HALF:5 * HALF]
    o_in, b = _hgrn_intra(qq, kk, gg, vv, lv, seq_len)
    qh_scr[...] = qq * jnp.exp2(b)
    b3 = b.reshape(nseq, seq_len, HALF)
    b_last = jnp.broadcast_to(b3[:, seq_len - 1:seq_len, :], (nseq, seq_len, HALF))
    el_scr[...] = jnp.exp2(b_last)
    kdec = kk * jnp.exp2(b_last.reshape(CHUNK, HALF) - b)
    for hh in range(N_HEADS):
        hs = slice(hh * HEAD, (hh + 1) * HEAD)
        kt_scr[hh] = kdec[:, hs].T.astype(BF16)
    vb_scr[...] = vv.astype(BF16)
    row_seq = lax.broadcasted_iota(jnp.int32, (HEAD, CHUNK), 1) // seq_len

    def per_seq(j, carry):
        rws = pl.ds(pl.multiple_of(j * seq_len, seq_len), seq_len)
        own = row_seq == j
        el = el_scr[j]
        for hh in range(N_HEADS):
            hs = slice(hh * HEAD, (hh + 1) * HEAD)
            s_old = s0_ref[0, j, hh]
            o_scr[rws, hs] = _dot(qh_scr[rws, hs].astype(BF16), s_old.astype(BF16))
            dec = jnp.broadcast_to(el[0:1, hs], (HEAD, HEAD)).T
            kt = jnp.where(own, kt_scr[hh], jnp.zeros((), BF16))
            st_ref[0, j, hh] = dec * s_old + _dot(kt, vb_scr[:, hs])
        return carry

    lax.fori_loop(0, nseq, per_seq, 0)

    g = proj_scr[rows, 5 * HALF:6 * HALF]
    b_out = _rms_gate(o_in + o_scr[...], bnw_ref[...], g)
    mix = jnp.concatenate([a_out, b_out], axis=1).astype(BF16)
    y3 = _dot(mix, wout_ref[0]).reshape(nseq, seq_len, D_MODEL)
    x1_ref[...] = _layer_norm(ALPHA * x3 + mod[2][:, None, :] * y3, l1w_ref[...], l1b_ref[...])


def _mod_rows(mod_ref, per_seq, j):
    if per_seq:
        return mod_ref[0, j][:, None, :]
    return mod_ref[0, 0][j:j + 1][None]


def _swiglu(h, wg_ref, wu_ref, wd_ref):
    acc = None
    for f0, f1 in FF_SPLITS:
        act = (_silu(_dot(h, wg_ref[:, f0:f1])) * _dot(h, wu_ref[:, f0:f1])).astype(BF16)
        part = _dot(act, wd_ref[f0:f1, :])
        acc = part if acc is None else acc + part
    return acc


def _ffn_dense_kernel(x_ref, mod_ref, wg_ref, wu_ref, wd_ref, l2w_ref, l2b_ref, *rest, per_seq, n_cast):
    cast_in, o_ref, cast_out = rest[:n_cast], rest[n_cast], rest[n_cast + 1:]
    _cast_blocks(cast_in, cast_out)
    x3 = x_ref[...]
    groups, rows, _ = x3.shape
    h = (x3 * (1.0 + _mod_rows(mod_ref, per_seq, 4)) + _mod_rows(mod_ref, per_seq, 3))
    h = h.reshape(groups * rows, D_MODEL).astype(BF16)
    f3 = _swiglu(h, wg_ref.at[0], wu_ref.at[0], wd_ref.at[0]).reshape(groups, rows, D_MODEL)
    o_ref[...] = _layer_norm(ALPHA * x3 + _mod_rows(mod_ref, per_seq, 5) * f3, l2w_ref[...], l2b_ref[...])


def _moe_input(x_ref, mod_ref, per_seq):
    x3 = x_ref[...]
    groups, rows, _ = x3.shape
    h3 = x3 * (1.0 + _mod_rows(mod_ref, per_seq, 4)) + _mod_rows(mod_ref, per_seq, 3)
    return h3.reshape(groups * rows, D_MODEL)


def _router_kernel(x_ref, mod_ref, wr_ref, idx_ref, gate_ref, *, per_seq):
    h = _moe_input(x_ref, mod_ref, per_seq)
    n = h.shape[0]
    wr = wr_ref[...]
    h_hi, w_hi = h.astype(BF16), wr.astype(BF16)
    h_lo, w_lo = (h - h_hi.astype(F32)).astype(BF16), (wr - w_hi.astype(F32)).astype(BF16)
    logits = _dot(h_hi, w_hi) + (_dot(h_hi, w_lo) + _dot(h_lo, w_hi))
    z = jnp.exp(logits - jnp.max(logits, axis=-1, keepdims=True))
    p = z / jnp.sum(z, axis=-1, keepdims=True)
    lane = lax.broadcasted_iota(jnp.int32, p.shape, 1)
    p1 = jnp.max(p, axis=-1, keepdims=True)
    i1 = jnp.min(jnp.where(p == p1, lane, N_EXPERTS), axis=-1, keepdims=True)
    rest = jnp.where(lane == i1, -1.0, p)
    p2 = jnp.max(rest, axis=-1, keepdims=True)
    i2 = jnp.min(jnp.where(rest == p2, lane, N_EXPERTS), axis=-1, keepdims=True)
    two = lax.broadcasted_iota(jnp.int32, (n, 2), 1)
    idx_ref[...] = jnp.where(two == 0, i1, i2)
    gate_ref[...] = jnp.where(two == 0, p1, p2) / (p1 + p2)


def _for_each_row(n, fn):
    def body(i, carry):
        fn(i)
        return carry
    lax.fori_loop(0, n, body, 0, unroll=DMA_ISSUE_UNROLL)


def _lane_block(n, s):
    return pl.ds(s, n, stride=ROW_TILE)


def _tile_of_row(r):
    return pl.ds(pl.multiple_of(r * ROW_TILE, ROW_TILE), ROW_TILE)


def _dispatch_kernel(xp_ref, modp_ref, xs_ref, mods_ref, da_ref, db_ref, win_ref, hg_ref, rows_scr, sem, *,
                     prompt_steps):
    step = pl.program_id(0)
    zero_rows = EXPERT_TILE * ROW_TILE

    def zero_fill(w):
        start = pl.multiple_of(win_ref[0, w] * ROW_TILE, ROW_TILE)
        return pltpu.make_async_copy(rows_scr.at[pl.ds(0, zero_rows)], hg_ref.at[pl.ds(start, zero_rows)], sem)

    @pl.when(step == 0)
    def _():
        rows_scr[pl.ds(0, zero_rows), :] = jnp.zeros((zero_rows, LANE), F32)
        for w in range(win_ref.shape[1]):
            zero_fill(w).start()
            zero_fill(w).wait()

    n = da_ref.shape[2]

    def stage(x_ref, mod_ref, per_seq):
        h = _moe_input(x_ref, mod_ref, per_seq)
        for s in range(ROW_TILE):
            rows_scr[_lane_block(n, s), :] = h[:, s * LANE:(s + 1) * LANE]

    @pl.when(step < prompt_steps)
    def _():
        stage(xp_ref, modp_ref, False)

    @pl.when(step >= prompt_steps)
    def _():
        stage(xs_ref, mods_ref, True)

    def copy(i, slot_ref):
        return pltpu.make_async_copy(rows_scr.at[_tile_of_row(i)], hg_ref.at[_tile_of_row(slot_ref[0, 0, i])], sem)

    _for_each_row(n, lambda i: (copy(i, da_ref).start(priority=0), copy(i, db_ref).start(priority=1)))
    all_rows = pltpu.make_async_copy(rows_scr.at[pl.ds(0, n * ROW_TILE)], hg_ref.at[pl.ds(0, n * ROW_TILE)], sem)
    all_rows.wait()
    all_rows.wait()


def _expert_kernel(te_ref, used_ref, hg_ref, wg_ref, wu_ref, wd_ref, o_ref, raw_scr, h_scr, sem):
    del te_ref
    i = pl.program_id(0)
    tile = h_scr.shape[1]
    cur, nxt = i % 2, (i + 1) % 2

    def fetch(t, slot):
        rows = pl.ds(pl.multiple_of(t * tile * ROW_TILE, tile * ROW_TILE), tile * ROW_TILE)
        return pltpu.make_async_copy(hg_ref.at[rows], raw_scr.at[slot], sem.at[slot])

    def rearrange(slot):
        for s in range(ROW_TILE):
            h_scr[slot, :, s * LANE:(s + 1) * LANE] = raw_scr[slot, _lane_block(tile, s), :].astype(BF16)

    used = used_ref[0]

    @pl.when(i == 0)
    def _():
        fetch(0, 0).start()
        raw_scr[1] = jnp.zeros(raw_scr.shape[1:], F32)
        fetch(0, 0).wait()
        rearrange(0)

        @pl.when(1 < used)
        def _():
            fetch(1, 1).start()

    @pl.when(i + 1 < used)
    def _():
        fetch(i + 1, nxt).wait()

    @pl.when(i + 2 < used)
    def _():
        fetch(i + 2, cur).start()

    @pl.when(i < used)
    def _():
        rearrange(nxt)
        y = _swiglu(h_scr[cur], wg_ref.at[0], wu_ref.at[0], wd_ref.at[0])
        for s in range(ROW_TILE):
            o_ref[_lane_block(tile, s), :] = y[:, s * LANE:(s + 1) * LANE]

    @pl.when(i >= used)
    def _():
        o_ref[...] = jnp.zeros_like(o_ref)


def _combine_kernel(x_ref, mod_ref, gate_ref, da_ref, db_ref, yo_ref, l2w_ref, l2b_ref, o_ref,
                    ya_scr, yb_scr, f_scr, sem, *, per_seq):
    step = pl.program_id(0)
    last = pl.num_programs(0) - 1
    groups, rows, _ = x_ref.shape
    n = groups * rows

    def copy(i, slot_ref, dst, par):
        return pltpu.make_async_copy(yo_ref.at[_tile_of_row(slot_ref[0, 0, i])], dst.at[par, _tile_of_row(i)],
                                     sem.at[par])

    @pl.when(step < last)
    def _():
        par = step % 2
        _for_each_row(n, lambda i: (copy(i, da_ref, ya_scr, par).start(priority=0),
                                    copy(i, db_ref, yb_scr, par).start(priority=1)))

    @pl.when(step > 0)
    def _():
        par = (step - 1) % 2
        for dst in (ya_scr, yb_scr):
            pltpu.make_async_copy(yo_ref.at[pl.ds(0, n * ROW_TILE)], dst.at[par], sem.at[par]).wait()
        x3 = x_ref[...]
        ga, gb = gate_ref[:, 0:1], gate_ref[:, 1:2]
        for s in range(ROW_TILE):
            f_scr[:, s * LANE:(s + 1) * LANE] = (ga * ya_scr[par, _lane_block(n, s), :]
                                                 + gb * yb_scr[par, _lane_block(n, s), :])
        f3 = f_scr[...].reshape(groups, rows, D_MODEL)
        o_ref[...] = _layer_norm(ALPHA * x3 + _mod_rows(mod_ref, per_seq, 5) * f3, l2w_ref[...], l2b_ref[...])


def _params(*semantics):
    return pltpu.CompilerParams(dimension_semantics=semantics, vmem_limit_bytes=VMEM_LIMIT_BYTES)


def _const_spec(shape):
    return pl.BlockSpec(shape, lambda *_: (0,) * len(shape), pipeline_mode=pl.Buffered(1))


def _adaln(c_all, w_ada, b_ada):
    nb = c_all.shape[0]
    tn = ADALN_COLS

    def body(c_ref, w_ref, b_ref, o_ref):
        c = c_ref[...]
        o_ref[0] = _dot(_silu(c).astype(BF16), w_ref[0].astype(BF16)) + b_ref[0]

    return pl.pallas_call(
        body,
        out_shape=jax.ShapeDtypeStruct((DEPTH, nb, 6 * D_MODEL), F32),
        grid=(DEPTH, 6 * D_MODEL // tn),
        in_specs=[pl.BlockSpec((nb, D_MODEL), lambda l, j: (0, 0)),
                  pl.BlockSpec((1, D_MODEL, tn), lambda l, j: (l, 0, j)),
                  pl.BlockSpec((1, 1, tn), lambda l, j: (l, 0, j))],
        out_specs=pl.BlockSpec((1, nb, tn), lambda l, j: (l, 0, j)),
        compiler_params=_params("arbitrary", "arbitrary"),
        name="adaln_modulation",
    )(c_all, w_ada, b_ada.reshape(DEPTH, 1, 6 * D_MODEL))


def _cast_bf16(w):
    rows, cols = w.shape[-2:]
    w3 = w.reshape(-1, rows, cols)

    def body(w_ref, o_ref):
        o_ref[...] = w_ref[...].astype(BF16)

    out = pl.pallas_call(
        body,
        out_shape=jax.ShapeDtypeStruct(w3.shape, BF16),
        grid=(w3.shape[0],),
        in_specs=[pl.BlockSpec((1, rows, cols), lambda g: (g, 0, 0))],
        out_specs=pl.BlockSpec((1, rows, cols), lambda g: (g, 0, 0)),
        compiler_params=_params("arbitrary"),
        name="cast_bf16",
    )(w3)
    return out.reshape(w.shape)


def _cast_plan(weights, moe_layer, n_steps, step_of):
    in_specs, out_specs, out_shapes = [], [], []
    for w in weights:
        _, n_exp, rows, cols = w.shape
        assert n_steps % n_exp == 0, (n_steps, n_exp)
        parts = n_steps // n_exp
        rb = rows // parts
        assert rb * parts == rows and rb % BF16_SUBLANE == 0
        in_specs.append(pl.BlockSpec(
            (1, 1, rb, cols), lambda *g, parts=parts: (moe_layer, step_of(*g) // parts, step_of(*g) % parts, 0)))
        out_specs.append(pl.BlockSpec(
            (1, rb, cols), lambda *g, parts=parts: (step_of(*g) // parts, step_of(*g) % parts, 0)))
        out_shapes.append(jax.ShapeDtypeStruct((n_exp, rows, cols), BF16))
    return in_specs, out_specs, out_shapes


def _layer_spec(shape, layer):
    return pl.BlockSpec((1,) + shape, lambda *_: (layer,) + (0,) * len(shape), pipeline_mode=pl.Buffered(1))


def _mixer_weight_specs(layer):
    return [_layer_spec((D_MODEL, IN_COLS), layer), _layer_spec((D_MODEL, D_MODEL), layer),
            _const_spec((1, HALF)), _const_spec((1, HALF)),
            _const_spec((N_HEADS, CHUNK, CHUNK)), _const_spec((CHUNK, N_HEADS)),
            _const_spec((DEPTH, HALF)), _const_spec((1, HEAD)),
            _const_spec((1, D_MODEL)), _const_spec((1, D_MODEL)), _const_spec((CHUNK, CHUNK))]


def _mixer_prompt(layer, x, mod_p, wts, lv, cast=(), cast_layer=0):
    batch, seq, _ = x.shape
    tile = PROMPT_TILE
    rows_out = seq - CHUNK * ((seq - 1) // CHUNK)
    assert seq % tile == 0 and rows_out == CHUNK
    per = seq // tile

    def nxt(b, s):
        return jnp.minimum(b * per + s + 1, batch * per - 1)

    c_in, c_out, c_shapes = _cast_plan(cast, cast_layer, batch * per, lambda b, s: b * per + s)
    return pl.pallas_call(
        functools.partial(_mixer_prompt_kernel, layer=layer, tile=tile, n_cast=len(cast)),
        out_shape=(jax.ShapeDtypeStruct((batch, seq, D_MODEL), F32),
                   jax.ShapeDtypeStruct((batch, N_HEADS, HEAD, HEAD), F32),
                   jax.ShapeDtypeStruct((batch, CHUNK, HALF), F32), *c_shapes),
        grid=(batch, per),
        in_specs=[pl.BlockSpec((1, tile, D_MODEL), lambda b, s: (b, s, 0)),
                  pl.BlockSpec((1, 1, 6, D_MODEL), lambda b, s: (layer, b, 0, 0)),
                  pl.BlockSpec((1, tile, D_MODEL), lambda b, s: (nxt(b, s) // per, nxt(b, s) % per, 0)),
                  pl.BlockSpec((1, 1, 6, D_MODEL), lambda b, s: (layer, nxt(b, s) // per, 0, 0))]
        + _mixer_weight_specs(layer) + c_in,
        out_specs=(pl.BlockSpec((1, tile, D_MODEL), lambda b, s: (b, s, 0)),
                   pl.BlockSpec((1, N_HEADS, HEAD, HEAD), lambda b, s: (b, 0, 0, 0)),
                   pl.BlockSpec((1, CHUNK, HALF), lambda b, s: (b, 0, 0)), *c_out),
        scratch_shapes=[pltpu.VMEM((tile, IN_COLS), F32), pltpu.VMEM((tile, IN_COLS), F32),
                        pltpu.VMEM((tile, D_MODEL), BF16), pltpu.VMEM((tile, D_MODEL), BF16),
                        pltpu.VMEM((N_HEADS, HEAD, HEAD), F32)],
        compiler_params=_params("arbitrary", "arbitrary"),
        name="token_mixer_prompt",
    )(x, mod_p, x, mod_p, *wts, lv, *cast)


def _mixer_sample(layer, x, mod_s, state, wts, lv, states_so_far):
    nseq_all, seq_len, _ = x.shape
    nseq = SAMPLE_SEQS
    assert nseq * seq_len == CHUNK and nseq_all % nseq == 0 and seq_len == SUBLANE
    in_specs = [pl.BlockSpec((nseq, seq_len, D_MODEL), lambda j: (j, 0, 0)),
                pl.BlockSpec((1, 6, nseq, D_MODEL), lambda j: (layer, 0, j, 0)),
                pl.BlockSpec((1, nseq, N_HEADS, HEAD, HEAD), lambda j: (layer, j, 0, 0, 0))]
    in_specs += _mixer_weight_specs(layer) + [pl.BlockSpec(memory_space=pl.ANY)]
    operands = (x, mod_s, state, *wts, lv, states_so_far)
    aliases = {len(operands) - 1: 1}
    return pl.pallas_call(
        functools.partial(_mixer_sample_kernel, layer=layer, seq_len=seq_len),
        out_shape=(jax.ShapeDtypeStruct((nseq_all, seq_len, D_MODEL), F32),
                   jax.ShapeDtypeStruct(state.shape, F32),
                   jax.ShapeDtypeStruct((nseq_all, seq_len, HALF), F32)),
        grid=(nseq_all // nseq,),
        in_specs=in_specs,
        out_specs=(pl.BlockSpec((nseq, seq_len, D_MODEL), lambda j: (j, 0, 0)),
                   pl.BlockSpec((1, nseq, N_HEADS, HEAD, HEAD), lambda j: (layer, j, 0, 0, 0)),
                   pl.BlockSpec((nseq, seq_len, HALF), lambda j: (j, 0, 0))),
        input_output_aliases=aliases,
        scratch_shapes=[pltpu.VMEM((CHUNK, IN_COLS), F32), pltpu.VMEM((CHUNK, HALF), F32),
                        pltpu.VMEM((N_HEADS, HEAD, CHUNK), BF16), pltpu.VMEM((CHUNK, HALF), BF16),
                        pltpu.VMEM((nseq, seq_len, HALF), F32), pltpu.VMEM((CHUNK, HALF), F32)],
        compiler_params=_params("arbitrary"),
        name="token_mixer_sample",
    )(*operands)


def _row_blocking(x, per_seq, tile):
    batch, seq, _ = x.shape
    if per_seq:
        groups = tile // seq
        assert batch % groups == 0
        grid = (batch // groups,)
        x_spec = pl.BlockSpec((groups, seq, D_MODEL), lambda i: (i, 0, 0))
        return grid, x_spec, groups, lambda layer: pl.BlockSpec((1, 6, groups, D_MODEL), lambda i: (layer, 0, i, 0))
    assert seq % tile == 0
    per = seq // tile
    grid = (batch * per,)
    x_spec = pl.BlockSpec((1, tile, D_MODEL), lambda i: (i // per, i % per, 0))
    return grid, x_spec, per, lambda layer: pl.BlockSpec((1, 1, 6, D_MODEL), lambda i: (layer, i // per, 0, 0))


def _ffn_dense(layer, x, mod, per_seq, wg, wu, wd, l2w, l2b, cast=(), cast_layer=0):
    tile = FFN_TILE if not per_seq else CHUNK
    grid, x_spec, _, mod_spec = _row_blocking(x, per_seq, tile)
    c_in, c_out, c_shapes = _cast_plan(cast, cast_layer, grid[0], lambda i: i)
    return pl.pallas_call(
        functools.partial(_ffn_dense_kernel, per_seq=per_seq, n_cast=len(cast)),
        out_shape=(jax.ShapeDtypeStruct(x.shape, F32), *c_shapes),
        grid=grid,
        in_specs=[x_spec, mod_spec(layer),
                  _layer_spec((D_MODEL, D_FF), layer // 2), _layer_spec((D_MODEL, D_FF), layer // 2),
                  _layer_spec((D_FF, D_MODEL), layer // 2),
                  _const_spec((1, D_MODEL)), _const_spec((1, D_MODEL))] + c_in,
        out_specs=(x_spec, *c_out),
        compiler_params=_params("arbitrary"),
        name="ffn_dense",
    )(x, mod, wg, wu, wd, l2w, l2b, *cast)


def _router(layer, x, mod, per_seq, w_router):
    batch, seq, _ = x.shape
    tile = FFN_TILE if not per_seq else CHUNK
    grid, x_spec, _, mod_spec = _row_blocking(x, per_seq, tile)
    n = batch * seq
    return pl.pallas_call(
        functools.partial(_router_kernel, per_seq=per_seq),
        out_shape=(jax.ShapeDtypeStruct((n, 2), jnp.int32), jax.ShapeDtypeStruct((n, 2), F32)),
        grid=grid,
        in_specs=[x_spec, mod_spec(layer), _const_spec((D_MODEL, N_EXPERTS))],
        out_specs=(pl.BlockSpec((tile, 2), lambda i: (i, 0)), pl.BlockSpec((tile, 2), lambda i: (i, 0))),
        compiler_params=_params("arbitrary"),
        name="moe_router",
    )(x, mod, w_router)


def _slot_spec(tile):
    return pl.BlockSpec((1, 1, tile), lambda i: (i, 0, 0), memory_space=pltpu.SMEM)


def _dispatch(layer, xp, mod_p, xs, mod_s, slot_a, slot_b, zero_windows, n_rows):
    n_prompt, n_sample = xp.shape[0] * xp.shape[1], xs.shape[0] * xs.shape[1]
    tile = math.gcd(FFN_TILE, n_sample)
    _, xp_spec, _, modp_spec = _row_blocking(xp, False, tile)
    _, xs_spec, _, mods_spec = _row_blocking(xs, True, tile)
    p_steps, s_steps = n_prompt // tile, n_sample // tile

    def first(spec):
        return pl.BlockSpec(spec.block_shape, lambda i: spec.index_map(jnp.minimum(i, p_steps - 1)))

    def second(spec):
        return pl.BlockSpec(spec.block_shape, lambda i: spec.index_map(jnp.maximum(i - p_steps, 0)))

    return pl.pallas_call(
        functools.partial(_dispatch_kernel, prompt_steps=p_steps),
        out_shape=jax.ShapeDtypeStruct((n_rows * ROW_TILE, LANE), F32),
        grid=(p_steps + s_steps,),
        in_specs=[first(xp_spec), first(modp_spec(layer)), second(xs_spec), second(mods_spec(layer)),
                  _slot_spec(tile), _slot_spec(tile), pl.BlockSpec(memory_space=pltpu.SMEM)],
        out_specs=pl.BlockSpec(memory_space=pl.ANY),
        scratch_shapes=[pltpu.VMEM((max(tile, EXPERT_TILE) * ROW_TILE, LANE), F32), pltpu.SemaphoreType.DMA(())],
        compiler_params=_params("arbitrary"),
        name="moe_dispatch",
    )(xp, mod_p, xs, mod_s, slot_a.reshape(-1, 1, tile), slot_b.reshape(-1, 1, tile), zero_windows)


def _experts(hg, mp, tile_expert, tiles_used, wg, wu, wd):
    tile = EXPERT_TILE
    assert mp % tile == 0 and mp // tile >= 2 and hg.shape[0] >= mp * ROW_TILE
    return pl.pallas_call(
        _expert_kernel,
        out_shape=jax.ShapeDtypeStruct((mp * ROW_TILE, LANE), F32),
        grid_spec=pltpu.PrefetchScalarGridSpec(
            num_scalar_prefetch=2,
            grid=(mp // tile,),
            in_specs=[pl.BlockSpec(memory_space=pl.ANY),
                      pl.BlockSpec((1, D_MODEL, D_FF), lambda i, te, used: (te[i], 0, 0)),
                      pl.BlockSpec((1, D_MODEL, D_FF), lambda i, te, used: (te[i], 0, 0)),
                      pl.BlockSpec((1, D_FF, D_MODEL), lambda i, te, used: (te[i], 0, 0))],
            out_specs=pl.BlockSpec((tile * ROW_TILE, LANE), lambda i, te, used: (i, 0)),
            scratch_shapes=[pltpu.VMEM((2, tile * ROW_TILE, LANE), F32), pltpu.VMEM((2, tile, D_MODEL), BF16),
                            pltpu.SemaphoreType.DMA((2,))]),
        compiler_params=_params("arbitrary"),
        name="moe_experts",
    )(tile_expert, tiles_used, hg, wg, wu, wd)


def _combine(layer, x, mod, per_seq, gate, slot_a, slot_b, yo, l2w, l2b):
    tile = FFN_TILE if not per_seq else CHUNK
    (steps,), x_spec, _, mod_spec = _row_blocking(x, per_seq, tile)

    def late(spec):
        return pl.BlockSpec(spec.block_shape, lambda i: spec.index_map(jnp.maximum(i - 1, 0)))

    def early(spec):
        return pl.BlockSpec(spec.block_shape, lambda i: spec.index_map(jnp.minimum(i, steps - 1)),
                            memory_space=pltpu.SMEM)

    return pl.pallas_call(
        functools.partial(_combine_kernel, per_seq=per_seq),
        out_shape=jax.ShapeDtypeStruct(x.shape, F32),
        grid=(steps + 1,),
        in_specs=[late(x_spec), late(mod_spec(layer)), late(pl.BlockSpec((tile, 2), lambda i: (i, 0))),
                  early(_slot_spec(tile)), early(_slot_spec(tile)), pl.BlockSpec(memory_space=pl.ANY),
                  _const_spec((1, D_MODEL)), _const_spec((1, D_MODEL))],
        out_specs=late(x_spec),
        scratch_shapes=[pltpu.VMEM((2, tile * ROW_TILE, LANE), F32), pltpu.VMEM((2, tile * ROW_TILE, LANE), F32),
                        pltpu.VMEM((tile, D_MODEL), F32), pltpu.SemaphoreType.DMA((2,))],
        compiler_params=_params("arbitrary"),
        name="moe_combine",
    )(x, mod, gate, slot_a.reshape(-1, 1, tile), slot_b.reshape(-1, 1, tile), yo, l2w, l2b)


def _routing_tables(idx, n_pad_rows):
    e_flat = jnp.concatenate([idx[:, 0], idx[:, 1]])
    onehot = (e_flat[:, None] == jnp.arange(N_EXPERTS, dtype=jnp.int32)[None, :]).astype(jnp.int32)
    csum = jnp.cumsum(onehot, axis=0)
    padded = ((csum[-1] + EXPERT_TILE - 1) // EXPERT_TILE) * EXPERT_TILE
    pend = jnp.cumsum(padded)
    slot = jnp.sum(onehot * (csum - 1 + (pend - padded)[None, :]), axis=1).astype(jnp.int32)
    tile_start = jnp.arange(n_pad_rows // EXPERT_TILE, dtype=jnp.int32) * EXPERT_TILE
    tile_expert = jnp.sum((tile_start[:, None] >= pend[None, :]).astype(jnp.int32), axis=1)
    tiles_used = (pend[-1:] // EXPERT_TILE).astype(jnp.int32)
    pad_start = pend - padded + csum[-1]
    tail = jnp.minimum(pend[-1] + tile_start[:TAIL_WINDOWS], n_pad_rows)
    zero_windows = jnp.concatenate([pad_start, tail]).astype(jnp.int32)[None, :]
    return slot, jnp.minimum(tile_expert, N_EXPERTS - 1).astype(jnp.int32), tiles_used, zero_windows


def _round_up(a, b):
    return (a + b - 1) // b * b


def kernel(x_prompt, x_sample, state_hgrn, c_prompt, c_sample, w_ada, b_ada, w_in, w_out, a_ln_w, a_ln_b, a_ws, a_bs, lb_logits, b_norm_w, ln1_w, ln1_b, ln2_w, ln2_b, w_ff_gate, w_ff_up, w_ff_down, w_router, e_gate, e_up, e_down):
    batch, seq, _ = x_prompt.shape
    nseq, seq_len, _ = x_sample.shape
    n_prompt, n_sample = batch * seq, nseq * seq_len
    n_tok = n_prompt + n_sample

    mod = _adaln(jnp.concatenate([c_prompt, c_sample], axis=0), w_ada, b_ada)
    mod_p = mod[:, :batch].reshape(DEPTH, batch, 6, D_MODEL)
    mod_s = mod[:, batch:].reshape(DEPTH, nseq, 6, D_MODEL).transpose(0, 2, 1, 3)

    lv_p = jnp.asarray(_level_ids(CHUNK))
    lv_s = jnp.asarray(_level_ids(seq_len))
    reps = CHUNK // seq_len
    n_pad_rows = _round_up(2 * n_tok, EXPERT_TILE) + N_EXPERTS * EXPERT_TILE

    w_in_b, w_out_b = _cast_bf16(w_in), _cast_bf16(w_out)
    ff_b = tuple(_cast_bf16(w) for w in (w_ff_gate, w_ff_up, w_ff_down))
    assert DEPTH % 2 == 0

    xp, xs = x_prompt, x_sample
    st_p, st_s, cv_p, cv_s = [], jnp.zeros_like(state_hgrn), [], []
    for l in range(DEPTH):
        shared = (a_ln_w[l][None], a_ln_b[l][None])
        tail = (lb_logits, b_norm_w[l][None], ln1_w[l][None], ln1_b[l][None])
        wts_p = (w_in_b, w_out_b) + shared + (a_ws[l], a_bs[l].T) + tail
        ws_s = jnp.tile(a_ws[l][:, :seq_len, :seq_len], (1, reps, reps))
        bs_s = jnp.tile(a_bs[l][:, :seq_len].T, (reps, 1))
        wts_s = (w_in_b, w_out_b) + shared + (ws_s, bs_s) + tail
        if l % 2 == 0:
            xp, sp, vp = _mixer_prompt(l, xp, mod_p, wts_p, lv_p)
        else:
            xp, sp, vp, ed_b = _mixer_prompt(l, xp, mod_p, wts_p, lv_p, (e_down,), l // 2)
        xs, st_s, vs = _mixer_sample(l, xs, mod_s, state_hgrn, wts_s, lv_s, st_s)
        st_p.append(sp), cv_p.append(vp), cv_s.append(vs)
        l2w, l2b = ln2_w[l][None], ln2_b[l][None]
        if l % 2 == 0:
            xp, eg_b, eu_b = _ffn_dense(l, xp, mod_p, False, *ff_b, l2w, l2b, (e_gate, e_up), l // 2)
            xs, = _ffn_dense(l, xs, mod_s, True, *ff_b, l2w, l2b)
        else:
            wr = w_router[l // 2]
            ip, gp = _router(l, xp, mod_p, False, wr)
            is_, gs = _router(l, xs, mod_s, True, wr)
            slot, tile_expert, tiles_used, zero_windows = _routing_tables(
                jnp.concatenate([ip, is_], axis=0), n_pad_rows)
            sa_p, sa_s = slot[:n_prompt], slot[n_prompt:n_tok]
            sb_p, sb_s = slot[n_tok:n_tok + n_prompt], slot[n_tok + n_prompt:]
            hg = _dispatch(l, xp, mod_p, xs, mod_s, slot[:n_tok], slot[n_tok:], zero_windows,
                           n_pad_rows + EXPERT_TILE)
            yo = _experts(hg, n_pad_rows, tile_expert, tiles_used, eg_b, eu_b, ed_b)
            xp = _combine(l, xp, mod_p, False, gp, sa_p, sb_p, yo, l2w, l2b)
            xs = _combine(l, xs, mod_s, True, gs, sa_s, sb_s, yo, l2w, l2b)
    return (xp, xs, jnp.stack(st_p), st_s, jnp.stack(cv_p), jnp.stack(cv_s))
```

```python
import functools
import math

import numpy as np
import jax
import jax.numpy as jnp
from jax import lax
from jax.experimental import pallas as pl
from jax.experimental.pallas import tpu as pltpu

F32 = jnp.float32
BF16 = jnp.bfloat16

D_MODEL = 1024
DEPTH = 4
HALF = 512
N_HEADS = 4
HEAD = 128
CHUNK = 128
IN_COLS = 6 * HALF
D_FF = 2816
N_EXPERTS = 8
ALPHA = (2.0 * DEPTH) ** 0.25
LN_EPS = 1e-5
RMS_EPS = 1e-6
LOG2_E = math.log2(math.e)

VMEM_LIMIT_BYTES = 56 * 1024 * 1024
LANE = 128
SUBLANE = 8
BF16_SUBLANE = 16
ROW_TILE = D_MODEL // LANE
assert ROW_TILE == SUBLANE

PROMPT_TILE = 512
SAMPLE_SEQS = 16
FFN_TILE = 512
EXPERT_TILE = 256
TAIL_WINDOWS = N_EXPERTS + 1
ADALN_COLS = 1536
DMA_ISSUE_UNROLL = 8
FF_SPLITS = ((0, 1024), (1024, 2048), (2048, D_FF))


def _dot(a, b):
    return jnp.dot(a, b, preferred_element_type=F32)


def _dot_nt(a, b):
    return lax.dot_general(a, b, (((1,), (1,)), ((), ())), preferred_element_type=F32)


def _gelu(x):
    return 0.5 * x * (1.0 + lax.erf(x * (1.0 / math.sqrt(2.0))))


def _silu(x):
    return x * jax.nn.sigmoid(x)


def _layer_norm(z, w, b):
    mu = jnp.mean(z, axis=-1, keepdims=True)
    zc = z - mu
    var = jnp.mean(zc * zc, axis=-1, keepdims=True)
    return zc * lax.rsqrt(var + LN_EPS) * w + b


def _level_ids(block):
    t = np.arange(CHUNK)[:, None]
    s = np.arange(CHUNK)[None, :]
    x = t ^ s
    lv = np.where(x == 0, 0, np.floor(np.log2(np.maximum(x, 1))).astype(np.int64) + 1)
    ok = (s <= t) & (x < block)
    return np.where(ok, lv, -1).astype(np.int32)


def _reference_rows(b, m):
    rows, width = b.shape
    two_m = 2 * m
    if two_m >= SUBLANE:
        nb = rows // two_m
        b3 = b.reshape(nb, two_m, width)
        r = jnp.broadcast_to(b3[:, m - 1:m, :], (nb, two_m, width))
        return r.reshape(rows, width)
    t = lax.broadcasted_iota(jnp.int32, (rows, width), 0)
    tm = t & (two_m - 1)
    down1 = pltpu.roll(b, 1, 0)
    if m == 1:
        return jnp.where(tm == 0, b, down1)
    up1 = pltpu.roll(b, rows - 1, 0)
    down2 = pltpu.roll(b, 2, 0)
    return jnp.where(tm == 0, up1, jnp.where(tm == 1, b, jnp.where(tm == 2, down1, down2)))


def _hgrn_intra(qq, kk, gg, vv, lv, block):
    cm = (lv >= 0).astype(BF16)
    g_hi = gg.astype(BF16)
    rem = gg - g_hi.astype(F32)
    g_mid = rem.astype(BF16)
    g_lo = (rem - g_mid.astype(F32)).astype(BF16)
    b = _dot(cm, g_hi) + _dot(cm, g_mid) + _dot(cm, g_lo)

    heads = [slice(h * HEAD, (h + 1) * HEAD) for h in range(N_HEADS)]
    qb = qq.astype(BF16)
    kb = kk.astype(BF16)
    scores = [jnp.where(lv == 0, _dot_nt(qb[:, hs], kb[:, hs]), 0.0) for hs in heads]
    m = block // 2
    while m >= 1:
        level = int(math.log2(m)) + 1
        e = jnp.exp2(-jnp.abs(b - _reference_rows(b, m)))
        qe = (qq * e).astype(BF16)
        ke = (kk * e).astype(BF16)
        for h, hs in enumerate(heads):
            scores[h] = jnp.where(lv == level, _dot_nt(qe[:, hs], ke[:, hs]), scores[h])
        m //= 2
    vb = vv.astype(BF16)
    o = jnp.concatenate([_dot(scores[h].astype(BF16), vb[:, hs]) for h, hs in enumerate(heads)], axis=1)
    return o, b


def _cast_blocks(in_refs, out_refs):
    for src, dst in zip(in_refs, out_refs):
        dst[...] = src[0].astype(BF16)


def _forget_bound(lbl_ref, layer):
    z = lbl_ref[...]
    z = z - jnp.max(z, axis=0, keepdims=True)
    ez = jnp.exp(z)
    p = ez / jnp.sum(ez, axis=0, keepdims=True)
    c = p[0:1]
    for r in range(1, layer + 1):
        c = c + p[r:r + 1]
    return c - p[0:1]


def _mixer_chunk_front(proj_scr, rows, alnw_ref, alnb_ref, ws_ref, bs_ref, lb, lv):
    u = proj_scr[rows, 0 * HALF:1 * HALF]
    v = proj_scr[rows, 1 * HALF:2 * HALF]
    q = proj_scr[rows, 2 * HALF:3 * HALF]
    f = proj_scr[rows, 3 * HALF:4 * HALF]
    ug = _gelu(u)
    vn = _layer_norm(_gelu(v), alnw_ref[...], alnb_ref[...])
    vnb = vn.astype(BF16)
    a_parts = []
    for h in range(N_HEADS):
        hs = slice(h * HEAD, (h + 1) * HEAD)
        w = jnp.where(lv >= 0, ws_ref[h], 0.0).astype(BF16)
        mixed = _dot(w, vnb[:, hs]) + bs_ref[:, h:h + 1]
        a_parts.append(ug[:, hs] * mixed)
    a_out = jnp.concatenate(a_parts, axis=1)
    fg = lb + (1.0 - lb) * jax.nn.sigmoid(f)
    return a_out, vn, _silu(q), 1.0 - fg, jnp.log(fg) * LOG2_E


def _rms_gate(o, bnw, g):
    parts = []
    for h in range(N_HEADS):
        hs = slice(h * HEAD, (h + 1) * HEAD)
        oh = o[:, hs]
        parts.append(oh * lax.rsqrt(jnp.mean(oh * oh, axis=-1, keepdims=True) + RMS_EPS) * bnw)
    return jnp.concatenate(parts, axis=1) * _silu(g)


def _mixer_prompt_kernel(x_ref, mod_ref, xn_ref, modn_ref, win_ref, wout_ref, alnw_ref, alnb_ref, ws_ref, bs_ref,
                         lbl_ref, bnw_ref, l1w_ref, l1b_ref, lv_ref, *rest, layer, tile, n_cast):
    cast_in, rest = rest[:n_cast], rest[n_cast:]
    (x1_ref, st_ref, vn_ref), rest = rest[:3], rest[3:]
    cast_out, (proj_a, proj_b, hn_scr, mix_scr, s_scr) = rest[:n_cast], rest[n_cast:]
    _cast_blocks(cast_in, cast_out)
    step = pl.program_id(1)
    lin = pl.program_id(0) * pl.num_programs(1) + step
    n_chunks = tile // CHUNK
    col_splits = [(IN_COLS * c // n_chunks, IN_COLS * (c + 1) // n_chunks) for c in range(n_chunks)]

    @pl.when(step == 0)
    def _():
        s_scr[...] = jnp.zeros_like(s_scr)

    mod = mod_ref[0, 0]
    x = x_ref[0]

    @pl.when(lin == 0)
    def _():
        proj_a[...] = _dot((x * (1.0 + mod[1:2]) + mod[0:1]).astype(BF16), win_ref[0])

    modn = modn_ref[0, 0]
    hn_scr[...] = (xn_ref[0] * (1.0 + modn[1:2]) + modn[0:1]).astype(BF16)
    lv = lv_ref[...]
    lb = _forget_bound(lbl_ref, layer)

    def chunk(c, proj_scr):
        rows = slice(c * CHUNK, (c + 1) * CHUNK)
        a_out, vn, qq, kk, gg = _mixer_chunk_front(proj_scr, rows, alnw_ref, alnb_ref, ws_ref, bs_ref, lb, lv)
        vn_ref[0] = vn
        vv = proj_scr[rows, 4 * HALF:5 * HALF]
        o_in, b = _hgrn_intra(qq, kk, gg, vv, lv, CHUNK)
        qh = (qq * jnp.exp2(b)).astype(BF16)
        b_last = b[CHUNK - 1:CHUNK, :]
        kdec = kk * jnp.exp2(b_last - b)
        e_last = jnp.exp2(b_last)
        o_parts = []
        for hh in range(N_HEADS):
            hs = slice(hh * HEAD, (hh + 1) * HEAD)
            s_old = s_scr[hh]
            o_parts.append(o_in[:, hs] + _dot(qh[:, hs], s_old.astype(BF16)))
            dec = jnp.broadcast_to(e_last[:, hs], (HEAD, HEAD)).T
            s_scr[hh] = dec * s_old + _dot(kdec[:, hs].T.astype(BF16), vv[:, hs].astype(BF16))
        g = proj_scr[rows, 5 * HALF:6 * HALF]
        b_out = _rms_gate(jnp.concatenate(o_parts, axis=1), bnw_ref[...], g)
        mix_scr[rows, 0:HALF] = a_out.astype(BF16)
        mix_scr[rows, HALF:2 * HALF] = b_out.astype(BF16)

    def run(proj_cur, proj_nxt):
        for c, (c0, c1) in enumerate(col_splits):
            proj_nxt[:, c0:c1] = _dot(hn_scr[...], win_ref[0, :, c0:c1])
            chunk(c, proj_cur)

    @pl.when(lin % 2 == 0)
    def _():
        run(proj_a, proj_b)

    @pl.when(lin % 2 == 1)
    def _():
        run(proj_b, proj_a)

    y = _dot(mix_scr[...], wout_ref[0])
    x1_ref[0] = _layer_norm(ALPHA * x + mod[2:3] * y, l1w_ref[...], l1b_ref[...])

    @pl.when(step == pl.num_programs(1) - 1)
    def _():
        st_ref[0] = s_scr[...]


def _mixer_sample_kernel(x_ref, mod_ref, s0_ref, win_ref, wout_ref, alnw_ref, alnb_ref, ws_ref, bs_ref,
                         lbl_ref, bnw_ref, l1w_ref, l1b_ref, lv_ref, st_in_ref,
                         x1_ref, st_ref, vn_ref,
                         proj_scr, qh_scr, kt_scr, vb_scr, el_scr, o_scr, *, layer, seq_len):
    del st_in_ref
    nseq = SAMPLE_SEQS
    x3 = x_ref[...]
    mod = mod_ref[0]
    h3 = x3 * (1.0 + mod[1][:, None, :]) + mod[0][:, None, :]
    proj_scr[...] = _dot(h3.reshape(CHUNK, D_MODEL).astype(BF16), win_ref[0])
    lv = lv_ref[...]
    lb = _forget_bound(lbl_ref, layer)
    rows = slice(0, CHUNK)
    a_out, vn, qq, kk, gg = _mixer_chunk_front(proj_scr, rows, alnw_ref, alnb_ref, ws_ref, bs_ref, lb, lv)
    vn_ref[...] = vn.reshape(nseq, seq_len, HALF)
    vv = proj_scr[rows, 4 * HALF:5 * HALF]
    o_in, b = _hgrn_intra(qq, kk, gg, vv, lv, seq_len)
    qh_scr[...] = qq * jnp.exp2(b)
    b3 = b.reshape(nseq, seq_len, HALF)
    b_last = jnp.broadcast_to(b3[:, seq_len - 1:seq_len, :], (nseq, seq_len, HALF))
    el_scr[...] = jnp.exp2(b_last)
    kdec = kk * jnp.exp2(b_last.reshape(CHUNK, HALF) - b)
    for hh in range(N_HEADS):
        hs = slice(hh * HEAD, (hh + 1) * HEAD)
        kt_scr[hh] = kdec[:, hs].T.astype(BF16)
    vb_scr[...] = vv.astype(BF16)
    row_seq = lax.broadcasted_iota(jnp.int32, (HEAD, CHUNK), 1) // seq_len

    def per_seq(j, carry):
        rws = pl.ds(pl.multiple_of(j * seq_len, seq_len), seq_len)
        own = row_seq == j
        el = el_scr[j]
        for hh in range(N_HEADS):
            hs = slice(hh * HEAD, (hh + 1) * HEAD)
            s_old = s0_ref[0, j, hh]
            o_scr[rws, hs] = _dot(qh_scr[rws, hs].astype(BF16), s_old.astype(BF16))
            dec = jnp.broadcast_to(el[0:1, hs], (HEAD, HEAD)).T
            kt = jnp.where(own, kt_scr[hh], jnp.zeros((), BF16))
            st_ref[0, j, hh] = dec * s_old + _dot(kt, vb_scr[:, hs])
        return carry

    lax.fori_loop(0, nseq, per_seq, 0)

    g = proj_scr[rows, 5 * HALF:6 * HALF]
    b_out = _rms_gate(o_in + o_scr[...], bnw_ref[...], g)
    mix = jnp.concatenate([a_out, b_out], axis=1).astype(BF16)
    y3 = _dot(mix, wout_ref[0]).reshape(nseq, seq_len, D_MODEL)
    x1_ref[...] = _layer_norm(ALPHA * x3 + mod[2][:, None, :] * y3, l1w_ref[...], l1b_ref[...])


def _mod_rows(mod_ref, per_seq, j):
    if per_seq:
        return mod_ref[0, j][:, None, :]
    return mod_ref[0, 0][j:j + 1][None]


def _swiglu(h, wg_ref, wu_ref, wd_ref):
    acc = None
    for f0, f1 in FF_SPLITS:
        act = (_silu(_dot(h, wg_ref[:, f0:f1])) * _dot(h, wu_ref[:, f0:f1])).astype(BF16)
        part = _dot(act, wd_ref[f0:f1, :])
        acc = part if acc is None else acc + part
    return acc


def _ffn_dense_kernel(x_ref, mod_ref, wg_ref, wu_ref, wd_ref, l2w_ref, l2b_ref, *rest, per_seq, n_cast):
    cast_in, o_ref, cast_out = rest[:n_cast], rest[n_cast], rest[n_cast + 1:]
    _cast_blocks(cast_in, cast_out)
    x3 = x_ref[...]
    groups, rows, _ = x3.shape
    h = (x3 * (1.0 + _mod_rows(mod_ref, per_seq, 4)) + _mod_rows(mod_ref, per_seq, 3))
    h = h.reshape(groups * rows, D_MODEL).astype(BF16)
    f3 = _swiglu(h, wg_ref.at[0], wu_ref.at[0], wd_ref.at[0]).reshape(groups, rows, D_MODEL)
    o_ref[...] = _layer_norm(ALPHA * x3 + _mod_rows(mod_ref, per_seq, 5) * f3, l2w_ref[...], l2b_ref[...])


def _moe_input(x_ref, mod_ref, per_seq):
    x3 = x_ref[...]
    groups, rows, _ = x3.shape
    h3 = x3 * (1.0 + _mod_rows(mod_ref, per_seq, 4)) + _mod_rows(mod_ref, per_seq, 3)
    return h3.reshape(groups * rows, D_MODEL)


def _router_kernel(x_ref, mod_ref, wr_ref, idx_ref, gate_ref, *, per_seq):
    h = _moe_input(x_ref, mod_ref, per_seq)
    n = h.shape[0]
    wr = wr_ref[...]
    h_hi, w_hi = h.astype(BF16), wr.astype(BF16)
    h_lo, w_lo = (h - h_hi.astype(F32)).astype(BF16), (wr - w_hi.astype(F32)).astype(BF16)
    logits = _dot_nt(w_hi, h_hi) + (_dot_nt(w_lo, h_hi) + _dot_nt(w_hi, h_lo))
    z = jnp.exp(logits - jnp.max(logits, axis=0, keepdims=True))
    p = z / jnp.sum(z, axis=0, keepdims=True)
    row = lax.broadcasted_iota(jnp.int32, p.shape, 0)
    p1 = jnp.max(p, axis=0, keepdims=True)
    i1 = jnp.min(jnp.where(p == p1, row, N_EXPERTS), axis=0, keepdims=True)
    rest = jnp.where(row == i1, -1.0, p)
    p2 = jnp.max(rest, axis=0, keepdims=True)
    i2 = jnp.min(jnp.where(rest == p2, row, N_EXPERTS), axis=0, keepdims=True)
    two = lax.broadcasted_iota(jnp.int32, (2, n), 0)
    idx_ref[...] = jnp.where(two == 0, i1, i2)
    gate_ref[...] = jnp.where(two == 0, p1, p2) / (p1 + p2)


def _for_each_row(n, fn):
    def body(i, carry):
        fn(i)
        return carry
    lax.fori_loop(0, n, body, 0, unroll=DMA_ISSUE_UNROLL)


def _lane_block(n, s):
    return pl.ds(s, n, stride=ROW_TILE)


def _tile_of_row(r):
    return pl.ds(pl.multiple_of(r * ROW_TILE, ROW_TILE), ROW_TILE)


def _dispatch_kernel(xp_ref, modp_ref, xs_ref, mods_ref, da_ref, db_ref, win_ref, hg_ref, rows_scr, sem, *,
                     prompt_steps):
    step = pl.program_id(0)
    zero_rows = EXPERT_TILE * ROW_TILE

    def zero_fill(w):
        start = pl.multiple_of(win_ref[0, w] * ROW_TILE, ROW_TILE)
        return pltpu.make_async_copy(rows_scr.at[pl.ds(0, zero_rows)], hg_ref.at[pl.ds(start, zero_rows)], sem)

    @pl.when(step == 0)
    def _():
        rows_scr[pl.ds(0, zero_rows), :] = jnp.zeros((zero_rows, LANE), F32)
        for w in range(win_ref.shape[1]):
            zero_fill(w).start()
            zero_fill(w).wait()

    n = da_ref.shape[2]

    def stage(x_ref, mod_ref, per_seq):
        h = _moe_input(x_ref, mod_ref, per_seq)
        for s in range(ROW_TILE):
            rows_scr[_lane_block(n, s), :] = h[:, s * LANE:(s + 1) * LANE]

    @pl.when(step < prompt_steps)
    def _():
        stage(xp_ref, modp_ref, False)

    @pl.when(step >= prompt_steps)
    def _():
        stage(xs_ref, mods_ref, True)

    def copy(i, slot_ref):
        return pltpu.make_async_copy(rows_scr.at[_tile_of_row(i)], hg_ref.at[_tile_of_row(slot_ref[0, 0, i])], sem)

    _for_each_row(n, lambda i: (copy(i, da_ref).start(priority=0), copy(i, db_ref).start(priority=1)))
    all_rows = pltpu.make_async_copy(rows_scr.at[pl.ds(0, n * ROW_TILE)], hg_ref.at[pl.ds(0, n * ROW_TILE)], sem)
    all_rows.wait()
    all_rows.wait()


def _expert_kernel(te_ref, used_ref, hg_ref, wg_ref, wu_ref, wd_ref, o_ref, raw_scr, h_scr, sem):
    del te_ref
    i = pl.program_id(0)
    tile = h_scr.shape[1]
    cur, nxt = i % 2, (i + 1) % 2

    def fetch(t, slot):
        rows = pl.ds(pl.multiple_of(t * tile * ROW_TILE, tile * ROW_TILE), tile * ROW_TILE)
        return pltpu.make_async_copy(hg_ref.at[rows], raw_scr.at[slot], sem.at[slot])

    def rearrange(slot):
        for s in range(ROW_TILE):
            h_scr[slot, :, s * LANE:(s + 1) * LANE] = raw_scr[slot, _lane_block(tile, s), :].astype(BF16)

    used = used_ref[0]

    @pl.when(i == 0)
    def _():
        fetch(0, 0).start()
        raw_scr[1] = jnp.zeros(raw_scr.shape[1:], F32)
        fetch(0, 0).wait()
        rearrange(0)

        @pl.when(1 < used)
        def _():
            fetch(1, 1).start()

    @pl.when(i + 1 < used)
    def _():
        fetch(i + 1, nxt).wait()

    @pl.when(i + 2 < used)
    def _():
        fetch(i + 2, cur).start()

    @pl.when(i < used)
    def _():
        rearrange(nxt)
        y = _swiglu(h_scr[cur], wg_ref.at[0], wu_ref.at[0], wd_ref.at[0])
        for s in range(ROW_TILE):
            o_ref[_lane_block(tile, s), :] = y[:, s * LANE:(s + 1) * LANE]

    @pl.when(i >= used)
    def _():
        o_ref[...] = jnp.zeros_like(o_ref)


def _combine_kernel(x_ref, mod_ref, gate_ref, da_ref, db_ref, yo_ref, l2w_ref, l2b_ref, o_ref,
                    ya_scr, yb_scr, f_scr, sem, *, per_seq):
    step = pl.program_id(0)
    last = pl.num_programs(0) - 1
    groups, rows, _ = x_ref.shape
    n = groups * rows

    def copy(i, slot_ref, dst, par):
        return pltpu.make_async_copy(yo_ref.at[_tile_of_row(slot_ref[0, 0, i])], dst.at[par, _tile_of_row(i)],
                                     sem.at[par])

    @pl.when(step < last)
    def _():
        par = step % 2
        _for_each_row(n, lambda i: (copy(i, da_ref, ya_scr, par).start(priority=0),
                                    copy(i, db_ref, yb_scr, par).start(priority=1)))

    @pl.when(step > 0)
    def _():
        par = (step - 1) % 2
        for dst in (ya_scr, yb_scr):
            pltpu.make_async_copy(yo_ref.at[pl.ds(0, n * ROW_TILE)], dst.at[par], sem.at[par]).wait()
        x3 = x_ref[...]
        ga, gb = gate_ref[:, 0:1], gate_ref[:, 1:2]
        for s in range(ROW_TILE):
            f_scr[:, s * LANE:(s + 1) * LANE] = (ga * ya_scr[par, _lane_block(n, s), :]
                                                 + gb * yb_scr[par, _lane_block(n, s), :])
        f3 = f_scr[...].reshape(groups, rows, D_MODEL)
        o_ref[...] = _layer_norm(ALPHA * x3 + _mod_rows(mod_ref, per_seq, 5) * f3, l2w_ref[...], l2b_ref[...])


def _params(*semantics):
    return pltpu.CompilerParams(dimension_semantics=semantics, vmem_limit_bytes=VMEM_LIMIT_BYTES)


def _const_spec(shape):
    return pl.BlockSpec(shape, lambda *_: (0,) * len(shape), pipeline_mode=pl.Buffered(1))


def _adaln(c_all, w_ada, b_ada):
    nb = c_all.shape[0]
    tn = ADALN_COLS

    def body(c_ref, w_ref, b_ref, o_ref):
        c = c_ref[...]
        o_ref[0] = _dot(_silu(c).astype(BF16), w_ref[0].astype(BF16)) + b_ref[0]

    return pl.pallas_call(
        body,
        out_shape=jax.ShapeDtypeStruct((DEPTH, nb, 6 * D_MODEL), F32),
        grid=(DEPTH, 6 * D_MODEL // tn),
        in_specs=[pl.BlockSpec((nb, D_MODEL), lambda l, j: (0, 0)),
                  pl.BlockSpec((1, D_MODEL, tn), lambda l, j: (l, 0, j)),
                  pl.BlockSpec((1, 1, tn), lambda l, j: (l, 0, j))],
        out_specs=pl.BlockSpec((1, nb, tn), lambda l, j: (l, 0, j)),
        compiler_params=_params("arbitrary", "arbitrary"),
        name="adaln_modulation",
    )(c_all, w_ada, b_ada.reshape(DEPTH, 1, 6 * D_MODEL))


def _cast_bf16(w):
    rows, cols = w.shape[-2:]
    w3 = w.reshape(-1, rows, cols)

    def body(w_ref, o_ref):
        o_ref[...] = w_ref[...].astype(BF16)

    out = pl.pallas_call(
        body,
        out_shape=jax.ShapeDtypeStruct(w3.shape, BF16),
        grid=(w3.shape[0],),
        in_specs=[pl.BlockSpec((1, rows, cols), lambda g: (g, 0, 0))],
        out_specs=pl.BlockSpec((1, rows, cols), lambda g: (g, 0, 0)),
        compiler_params=_params("arbitrary"),
        name="cast_bf16",
    )(w3)
    return out.reshape(w.shape)


def _cast_plan(weights, moe_layer, n_steps, step_of):
    in_specs, out_specs, out_shapes = [], [], []
    for w in weights:
        _, n_exp, rows, cols = w.shape
        assert n_steps % n_exp == 0, (n_steps, n_exp)
        parts = n_steps // n_exp
        rb = rows // parts
        assert rb * parts == rows and rb % BF16_SUBLANE == 0
        in_specs.append(pl.BlockSpec(
            (1, 1, rb, cols), lambda *g, parts=parts: (moe_layer, step_of(*g) // parts, step_of(*g) % parts, 0)))
        out_specs.append(pl.BlockSpec(
            (1, rb, cols), lambda *g, parts=parts: (step_of(*g) // parts, step_of(*g) % parts, 0)))
        out_shapes.append(jax.ShapeDtypeStruct((n_exp, rows, cols), BF16))
    return in_specs, out_specs, out_shapes


def _layer_spec(shape, layer):
    return pl.BlockSpec((1,) + shape, lambda *_: (layer,) + (0,) * len(shape), pipeline_mode=pl.Buffered(1))


def _mixer_weight_specs(layer):
    return [_layer_spec((D_MODEL, IN_COLS), layer), _layer_spec((D_MODEL, D_MODEL), layer),
            _const_spec((1, HALF)), _const_spec((1, HALF)),
            _const_spec((N_HEADS, CHUNK, CHUNK)), _const_spec((CHUNK, N_HEADS)),
            _const_spec((DEPTH, HALF)), _const_spec((1, HEAD)),
            _const_spec((1, D_MODEL)), _const_spec((1, D_MODEL)), _const_spec((CHUNK, CHUNK))]


def _mixer_prompt(layer, x, mod_p, wts, lv, cast=(), cast_layer=0):
    batch, seq, _ = x.shape
    tile = PROMPT_TILE
    rows_out = seq - CHUNK * ((seq - 1) // CHUNK)
    assert seq % tile == 0 and rows_out == CHUNK
    per = seq // tile

    def nxt(b, s):
        return jnp.minimum(b * per + s + 1, batch * per - 1)

    c_in, c_out, c_shapes = _cast_plan(cast, cast_layer, batch * per, lambda b, s: b * per + s)
    return pl.pallas_call(
        functools.partial(_mixer_prompt_kernel, layer=layer, tile=tile, n_cast=len(cast)),
        out_shape=(jax.ShapeDtypeStruct((batch, seq, D_MODEL), F32),
                   jax.ShapeDtypeStruct((batch, N_HEADS, HEAD, HEAD), F32),
                   jax.ShapeDtypeStruct((batch, CHUNK, HALF), F32), *c_shapes),
        grid=(batch, per),
        in_specs=[pl.BlockSpec((1, tile, D_MODEL), lambda b, s: (b, s, 0)),
                  pl.BlockSpec((1, 1, 6, D_MODEL), lambda b, s: (layer, b, 0, 0)),
                  pl.BlockSpec((1, tile, D_MODEL), lambda b, s: (nxt(b, s) // per, nxt(b, s) % per, 0)),
                  pl.BlockSpec((1, 1, 6, D_MODEL), lambda b, s: (layer, nxt(b, s) // per, 0, 0))]
        + _mixer_weight_specs(layer) + c_in,
        out_specs=(pl.BlockSpec((1, tile, D_MODEL), lambda b, s: (b, s, 0)),
                   pl.BlockSpec((1, N_HEADS, HEAD, HEAD), lambda b, s: (b, 0, 0, 0)),
                   pl.BlockSpec((1, CHUNK, HALF), lambda b, s: (b, 0, 0)), *c_out),
        scratch_shapes=[pltpu.VMEM((tile, IN_COLS), F32), pltpu.VMEM((tile, IN_COLS), F32),
                        pltpu.VMEM((tile, D_MODEL), BF16), pltpu.VMEM((tile, D_MODEL), BF16),
                        pltpu.VMEM((N_HEADS, HEAD, HEAD), F32)],
        compiler_params=_params("arbitrary", "arbitrary"),
        name="token_mixer_prompt",
    )(x, mod_p, x, mod_p, *wts, lv, *cast)


def _mixer_sample(layer, x, mod_s, state, wts, lv, states_so_far):
    nseq_all, seq_len, _ = x.shape
    nseq = SAMPLE_SEQS
    assert nseq * seq_len == CHUNK and nseq_all % nseq == 0 and seq_len == SUBLANE
    in_specs = [pl.BlockSpec((nseq, seq_len, D_MODEL), lambda j: (j, 0, 0)),
                pl.BlockSpec((1, 6, nseq, D_MODEL), lambda j: (layer, 0, j, 0)),
                pl.BlockSpec((1, nseq, N_HEADS, HEAD, HEAD), lambda j: (layer, j, 0, 0, 0))]
    in_specs += _mixer_weight_specs(layer) + [pl.BlockSpec(memory_space=pl.ANY)]
    operands = (x, mod_s, state, *wts, lv, states_so_far)
    aliases = {len(operands) - 1: 1}
    return pl.pallas_call(
        functools.partial(_mixer_sample_kernel, layer=layer, seq_len=seq_len),
        out_shape=(jax.ShapeDtypeStruct((nseq_all, seq_len, D_MODEL), F32),
                   jax.ShapeDtypeStruct(state.shape, F32),
                   jax.ShapeDtypeStruct((nseq_all, seq_len, HALF), F32)),
        grid=(nseq_all // nseq,),
        in_specs=in_specs,
        out_specs=(pl.BlockSpec((nseq, seq_len, D_MODEL), lambda j: (j, 0, 0)),
                   pl.BlockSpec((1, nseq, N_HEADS, HEAD, HEAD), lambda j: (layer, j, 0, 0, 0)),
                   pl.BlockSpec((nseq, seq_len, HALF), lambda j: (j, 0, 0))),
        input_output_aliases=aliases,
        scratch_shapes=[pltpu.VMEM((CHUNK, IN_COLS), F32), pltpu.VMEM((CHUNK, HALF), F32),
                        pltpu.VMEM((N_HEADS, HEAD, CHUNK), BF16), pltpu.VMEM((CHUNK, HALF), BF16),
                        pltpu.VMEM((nseq, seq_len, HALF), F32), pltpu.VMEM((CHUNK, HALF), F32)],
        compiler_params=_params("arbitrary"),
        name="token_mixer_sample",
    )(*operands)


def _row_blocking(x, per_seq, tile):
    batch, seq, _ = x.shape
    if per_seq:
        groups = tile // seq
        assert batch % groups == 0
        grid = (batch // groups,)
        x_spec = pl.BlockSpec((groups, seq, D_MODEL), lambda i: (i, 0, 0))
        return grid, x_spec, groups, lambda layer: pl.BlockSpec((1, 6, groups, D_MODEL), lambda i: (layer, 0, i, 0))
    assert seq % tile == 0
    per = seq // tile
    grid = (batch * per,)
    x_spec = pl.BlockSpec((1, tile, D_MODEL), lambda i: (i // per, i % per, 0))
    return grid, x_spec, per, lambda layer: pl.BlockSpec((1, 1, 6, D_MODEL), lambda i: (layer, i // per, 0, 0))


def _ffn_dense(layer, x, mod, per_seq, wg, wu, wd, l2w, l2b, cast=(), cast_layer=0):
    tile = FFN_TILE if not per_seq else CHUNK
    grid, x_spec, _, mod_spec = _row_blocking(x, per_seq, tile)
    c_in, c_out, c_shapes = _cast_plan(cast, cast_layer, grid[0], lambda i: i)
    return pl.pallas_call(
        functools.partial(_ffn_dense_kernel, per_seq=per_seq, n_cast=len(cast)),
        out_shape=(jax.ShapeDtypeStruct(x.shape, F32), *c_shapes),
        grid=grid,
        in_specs=[x_spec, mod_spec(layer),
                  _layer_spec((D_MODEL, D_FF), layer // 2), _layer_spec((D_MODEL, D_FF), layer // 2),
                  _layer_spec((D_FF, D_MODEL), layer // 2),
                  _const_spec((1, D_MODEL)), _const_spec((1, D_MODEL))] + c_in,
        out_specs=(x_spec, *c_out),
        compiler_params=_params("arbitrary"),
        name="ffn_dense",
    )(x, mod, wg, wu, wd, l2w, l2b, *cast)


def _router(layer, x, mod, per_seq, w_router):
    batch, seq, _ = x.shape
    tile = FFN_TILE if not per_seq else CHUNK
    grid, x_spec, _, mod_spec = _row_blocking(x, per_seq, tile)
    n = batch * seq
    idx_t, gate_t = pl.pallas_call(
        functools.partial(_router_kernel, per_seq=per_seq),
        out_shape=(jax.ShapeDtypeStruct((2, n), jnp.int32), jax.ShapeDtypeStruct((2, n), F32)),
        grid=grid,
        in_specs=[x_spec, mod_spec(layer), _const_spec((N_EXPERTS, D_MODEL))],
        out_specs=(pl.BlockSpec((2, tile), lambda i: (0, i)), pl.BlockSpec((2, tile), lambda i: (0, i))),
        compiler_params=_params("arbitrary"),
        name="moe_router",
    )(x, mod, w_router.T)
    return idx_t.T, gate_t.T


def _slot_spec(tile):
    return pl.BlockSpec((1, 1, tile), lambda i: (i, 0, 0), memory_space=pltpu.SMEM)


def _dispatch(layer, xp, mod_p, xs, mod_s, slot_a, slot_b, zero_windows, n_rows):
    n_prompt, n_sample = xp.shape[0] * xp.shape[1], xs.shape[0] * xs.shape[1]
    tile = math.gcd(FFN_TILE, n_sample)
    _, xp_spec, _, modp_spec = _row_blocking(xp, False, tile)
    _, xs_spec, _, mods_spec = _row_blocking(xs, True, tile)
    p_steps, s_steps = n_prompt // tile, n_sample // tile

    def first(spec):
        return pl.BlockSpec(spec.block_shape, lambda i: spec.index_map(jnp.minimum(i, p_steps - 1)))

    def second(spec):
        return pl.BlockSpec(spec.block_shape, lambda i: spec.index_map(jnp.maximum(i - p_steps, 0)))

    return pl.pallas_call(
        functools.partial(_dispatch_kernel, prompt_steps=p_steps),
        out_shape=jax.ShapeDtypeStruct((n_rows * ROW_TILE, LANE), F32),
        grid=(p_steps + s_steps,),
        in_specs=[first(xp_spec), first(modp_spec(layer)), second(xs_spec), second(mods_spec(layer)),
                  _slot_spec(tile), _slot_spec(tile), pl.BlockSpec(memory_space=pltpu.SMEM)],
        out_specs=pl.BlockSpec(memory_space=pl.ANY),
        scratch_shapes=[pltpu.VMEM((max(tile, EXPERT_TILE) * ROW_TILE, LANE), F32), pltpu.SemaphoreType.DMA(())],
        compiler_params=_params("arbitrary"),
        name="moe_dispatch",
    )(xp, mod_p, xs, mod_s, slot_a.reshape(-1, 1, tile), slot_b.reshape(-1, 1, tile), zero_windows)


def _experts(hg, mp, tile_expert, tiles_used, wg, wu, wd):
    tile = EXPERT_TILE
    assert mp % tile == 0 and mp // tile >= 2 and hg.shape[0] >= mp * ROW_TILE
    return pl.pallas_call(
        _expert_kernel,
        out_shape=jax.ShapeDtypeStruct((mp * ROW_TILE, LANE), F32),
        grid_spec=pltpu.PrefetchScalarGridSpec(
            num_scalar_prefetch=2,
            grid=(mp // tile,),
            in_specs=[pl.BlockSpec(memory_space=pl.ANY),
                      pl.BlockSpec((1, D_MODEL, D_FF), lambda i, te, used: (te[i], 0, 0)),
                      pl.BlockSpec((1, D_MODEL, D_FF), lambda i, te, used: (te[i], 0, 0)),
                      pl.BlockSpec((1, D_FF, D_MODEL), lambda i, te, used: (te[i], 0, 0))],
            out_specs=pl.BlockSpec((tile * ROW_TILE, LANE), lambda i, te, used: (i, 0)),
            scratch_shapes=[pltpu.VMEM((2, tile * ROW_TILE, LANE), F32), pltpu.VMEM((2, tile, D_MODEL), BF16),
                            pltpu.SemaphoreType.DMA((2,))]),
        compiler_params=_params("arbitrary"),
        name="moe_experts",
    )(tile_expert, tiles_used, hg, wg, wu, wd)


def _combine(layer, x, mod, per_seq, gate, slot_a, slot_b, yo, l2w, l2b):
    tile = FFN_TILE if not per_seq else CHUNK
    (steps,), x_spec, _, mod_spec = _row_blocking(x, per_seq, tile)

    def late(spec):
        return pl.BlockSpec(spec.block_shape, lambda i: spec.index_map(jnp.maximum(i - 1, 0)))

    def early(spec):
        return pl.BlockSpec(spec.block_shape, lambda i: spec.index_map(jnp.minimum(i, steps - 1)),
                            memory_space=pltpu.SMEM)

    return pl.pallas_call(
        functools.partial(_combine_kernel, per_seq=per_seq),
        out_shape=jax.ShapeDtypeStruct(x.shape, F32),
        grid=(steps + 1,),
        in_specs=[late(x_spec), late(mod_spec(layer)), late(pl.BlockSpec((tile, 2), lambda i: (i, 0))),
                  early(_slot_spec(tile)), early(_slot_spec(tile)), pl.BlockSpec(memory_space=pl.ANY),
                  _const_spec((1, D_MODEL)), _const_spec((1, D_MODEL))],
        out_specs=late(x_spec),
        scratch_shapes=[pltpu.VMEM((2, tile * ROW_TILE, LANE), F32), pltpu.VMEM((2, tile * ROW_TILE, LANE), F32),
                        pltpu.VMEM((tile, D_MODEL), F32), pltpu.SemaphoreType.DMA((2,))],
        compiler_params=_params("arbitrary"),
        name="moe_combine",
    )(x, mod, gate, slot_a.reshape(-1, 1, tile), slot_b.reshape(-1, 1, tile), yo, l2w, l2b)


def _routing_tables(idx, n_pad_rows):
    e_flat = jnp.concatenate([idx[:, 0], idx[:, 1]])
    onehot = (e_flat[:, None] == jnp.arange(N_EXPERTS, dtype=jnp.int32)[None, :]).astype(jnp.int32)
    csum = jnp.cumsum(onehot, axis=0)
    padded = ((csum[-1] + EXPERT_TILE - 1) // EXPERT_TILE) * EXPERT_TILE
    pend = jnp.cumsum(padded)
    slot = jnp.sum(onehot * (csum - 1 + (pend - padded)[None, :]), axis=1).astype(jnp.int32)
    tile_start = jnp.arange(n_pad_rows // EXPERT_TILE, dtype=jnp.int32) * EXPERT_TILE
    tile_expert = jnp.sum((tile_start[:, None] >= pend[None, :]).astype(jnp.int32), axis=1)
    tiles_used = (pend[-1:] // EXPERT_TILE).astype(jnp.int32)
    pad_start = pend - padded + csum[-1]
    tail = jnp.minimum(pend[-1] + tile_start[:TAIL_WINDOWS], n_pad_rows)
    zero_windows = jnp.concatenate([pad_start, tail]).astype(jnp.int32)[None, :]
    return slot, jnp.minimum(tile_expert, N_EXPERTS - 1).astype(jnp.int32), tiles_used, zero_windows


def _round_up(a, b):
    return (a + b - 1) // b * b


def kernel(x_prompt, x_sample, state_hgrn, c_prompt, c_sample, w_ada, b_ada, w_in, w_out, a_ln_w, a_ln_b, a_ws, a_bs, lb_logits, b_norm_w, ln1_w, ln1_b, ln2_w, ln2_b, w_ff_gate, w_ff_up, w_ff_down, w_router, e_gate, e_up, e_down):
    batch, seq, _ = x_prompt.shape
    nseq, seq_len, _ = x_sample.shape
    n_prompt, n_sample = batch * seq, nseq * seq_len
    n_tok = n_prompt + n_sample

    mod = _adaln(jnp.concatenate([c_prompt, c_sample], axis=0), w_ada, b_ada)
    mod_p = mod[:, :batch].reshape(DEPTH, batch, 6, D_MODEL)
    mod_s = mod[:, batch:].reshape(DEPTH, nseq, 6, D_MODEL).transpose(0, 2, 1, 3)

    lv_p = jnp.asarray(_level_ids(CHUNK))
    lv_s = jnp.asarray(_level_ids(seq_len))
    reps = CHUNK // seq_len
    n_pad_rows = _round_up(2 * n_tok, EXPERT_TILE) + N_EXPERTS * EXPERT_TILE

    w_in_b, w_out_b = _cast_bf16(w_in), _cast_bf16(w_out)
    ff_b = tuple(_cast_bf16(w) for w in (w_ff_gate, w_ff_up, w_ff_down))
    assert DEPTH % 2 == 0

    xp, xs = x_prompt, x_sample
    st_p, st_s, cv_p, cv_s = [], jnp.zeros_like(state_hgrn), [], []
    for l in range(DEPTH):
        shared = (a_ln_w[l][None], a_ln_b[l][None])
        tail = (lb_logits, b_norm_w[l][None], ln1_w[l][None], ln1_b[l][None])
        wts_p = (w_in_b, w_out_b) + shared + (a_ws[l], a_bs[l].T) + tail
        ws_s = jnp.tile(a_ws[l][:, :seq_len, :seq_len], (1, reps, reps))
        bs_s = jnp.tile(a_bs[l][:, :seq_len].T, (reps, 1))
        wts_s = (w_in_b, w_out_b) + shared + (ws_s, bs_s) + tail
        if l % 2 == 0:
            xp, sp, vp = _mixer_prompt(l, xp, mod_p, wts_p, lv_p)
        else:
            xp, sp, vp, ed_b = _mixer_prompt(l, xp, mod_p, wts_p, lv_p, (e_down,), l // 2)
        xs, st_s, vs = _mixer_sample(l, xs, mod_s, state_hgrn, wts_s, lv_s, st_s)
        st_p.append(sp), cv_p.append(vp), cv_s.append(vs)
        l2w, l2b = ln2_w[l][None], ln2_b[l][None]
        if l % 2 == 0:
            xp, eg_b, eu_b = _ffn_dense(l, xp, mod_p, False, *ff_b, l2w, l2b, (e_gate, e_up), l // 2)
            xs, = _ffn_dense(l, xs, mod_s, True, *ff_b, l2w, l2b)
        else:
            wr = w_router[l // 2]
            ip, gp = _router(l, xp, mod_p, False, wr)
            is_, gs = _router(l, xs, mod_s, True, wr)
            slot, tile_expert, tiles_used, zero_windows = _routing_tables(
                jnp.concatenate([ip, is_], axis=0), n_pad_rows)
            sa_p, sa_s = slot[:n_prompt], slot[n_prompt:n_tok]
            sb_p, sb_s = slot[n_tok:n_tok + n_prompt], slot[n_tok + n_prompt:]
            hg = _dispatch(l, xp, mod_p, xs, mod_s, slot[:n_tok], slot[n_tok:], zero_windows,
                           n_pad_rows + EXPERT_TILE)
            yo = _experts(hg, n_pad_rows, tile_expert, tiles_used, eg_b, eu_b, ed_b)
            xp = _combine(l, xp, mod_p, False, gp, sa_p, sb_p, yo, l2w, l2b)
            xs = _combine(l, xs, mod_s, True, gs, sa_s, sb_s, yo, l2w, l2b)
    return (xp, xs, jnp.stack(st_p), st_s, jnp.stack(cv_p), jnp.stack(cv_s))
```
